```python
import math
import jax, jax.numpy as jnp
from jax import lax
import numpy as np

D_MODEL = 2048
BATCH = 1
SEQ = 8192
DEPTH = 1

NORM_EPS = 1e-6
SSD_HEADS = 32
SSD_HEAD_DIM = 64
SSD_INNER = SSD_HEADS * SSD_HEAD_DIM
SSD_GROUPS = 4
SSD_STATE = 128
SSD_CONV = 4
SSD_CHUNK = 128
SSD_XBC = SSD_INNER + 2 * SSD_GROUPS * SSD_STATE
GLA_HEADS = 4
GLA_KEY_DIM = D_MODEL // 2
GLA_VAL_DIM = D_MODEL
GLA_HEAD_K = GLA_KEY_DIM // GLA_HEADS
GLA_HEAD_V = GLA_VAL_DIM // GLA_HEADS
GLA_GATE_RANK = 16
GLA_GATE_NORM = 16.0
GLA_CHUNK = 64
IN_SPLITS = (SSD_INNER, SSD_XBC, SSD_HEADS,
             GLA_KEY_DIM, GLA_KEY_DIM, GLA_VAL_DIM,
             GLA_VAL_DIM, GLA_GATE_RANK,
             SSD_INNER, GLA_VAL_DIM)
IN_DIM = SSD_INNER + SSD_XBC + SSD_HEADS + 2 * GLA_KEY_DIM + 2 * GLA_VAL_DIM + GLA_GATE_RANK + SSD_INNER + GLA_VAL_DIM
N_EXPERTS = 64
TOP_K = 8
N_EXPERT_GROUPS = 8
TOPK_GROUPS = 4
EXPERT_DIM = 512
SHARED_DIM = 512
ROUTED_SCALE = 2.5
MOE_BLOCK = 128

kernel_name = "hybrid_ssd_gla_moe_adaln_block"


def rms_norm(x, g):
    xf = x.astype(jnp.float32)
    y = xf * lax.rsqrt(jnp.mean(xf * xf, axis=-1, keepdims=True) + NORM_EPS)
    return (y * g.astype(jnp.float32)).astype(x.dtype)


def gated_group_rms_norm(y, z, g):
    yf = y.astype(jnp.float32) * jax.nn.silu(z.astype(jnp.float32))
    yg = yf.reshape(*yf.shape[:-1], SSD_GROUPS, -1)
    yg = yg * lax.rsqrt(jnp.mean(yg * yg, axis=-1, keepdims=True) + NORM_EPS)
    return (yg.reshape(yf.shape) * g.astype(jnp.float32)).astype(z.dtype)


def causal_depthwise_conv(x, w, b):
    ch = x.shape[-1]
    out = lax.conv_general_dilated(x, w[:, None, :].astype(x.dtype), window_strides=(1,),
                                   padding=[(SSD_CONV - 1, 0)],
                                   dimension_numbers=("NWC", "WIO", "NWC"),
                                   feature_group_count=ch)
    return out + b.astype(x.dtype)


def ssd_chunked(xh, dt, a, bmat, cmat):
    b, s = xh.shape[:2]
    L = SSD_CHUNK
    nc = s // L
    g = SSD_GROUPS
    r = SSD_HEADS // g
    xdt = (xh * dt[..., None]).reshape(b, nc, L, g, r, SSD_HEAD_DIM)
    acum = jnp.cumsum((dt * a).reshape(b, nc, L, g, r), axis=2)
    bc = bmat.reshape(b, nc, L, g, SSD_STATE)
    cc = cmat.reshape(b, nc, L, g, SSD_STATE)
    causal = jnp.tril(jnp.ones((L, L), dtype=bool))
    seg = acum[:, :, :, None] - acum[:, :, None]
    decay = jnp.exp(jnp.where(causal[:, :, None, None], seg, -jnp.inf))
    cb = jnp.einsum("bclgn,bcsgn->bclsg", cc, bc)
    y_diag = jnp.einsum("bclsgr,bcsgrp->bclgrp", cb[..., None] * decay, xdt)
    decay_to_end = jnp.exp(acum[:, :, -1:] - acum)
    chunk_states = jnp.einsum("bclgn,bclgrp->bcgrpn", bc, xdt * decay_to_end[..., None])
    chunk_decay = jnp.exp(acum[:, :, -1])

    def step(state, inp):
        st_c, dec_c = inp
        return state * dec_c[..., None, None] + st_c, state

    init = jnp.zeros((b, g, r, SSD_HEAD_DIM, SSD_STATE), jnp.float32)
    _, prev = lax.scan(step, init, (jnp.moveaxis(chunk_states, 1, 0), jnp.moveaxis(chunk_decay, 1, 0)))
    prev = jnp.moveaxis(prev, 0, 1)
    y_off = jnp.einsum("bclgn,bcgrpn->bclgrp", cc, prev) * jnp.exp(acum)[..., None]
    return (y_diag + y_off).reshape(b, s, SSD_HEADS, SSD_HEAD_DIM)


def gla_chunked(q, k, v, gk):
    b, s = q.shape[:2]
    L = GLA_CHUNK
    nc = s // L
    shp_k = (b, nc, L, GLA_HEADS, GLA_HEAD_K)
    q = q.reshape(shp_k) * (GLA_HEAD_K ** -0.5)
    k = k.reshape(shp_k)
    v = v.reshape(b, nc, L, GLA_HEADS, GLA_HEAD_V)
    bcum = jnp.cumsum(gk.reshape(shp_k), axis=2)
    bmid = bcum[:, :, L // 2:L // 2 + 1]
    q_rel = q * jnp.exp(bcum - bmid)
    k_rel = k * jnp.exp(bmid - bcum)
    causal = jnp.tril(jnp.ones((L, L), dtype=bool))
    att = jnp.where(causal, jnp.einsum("bclhd,bcshd->bchls", q_rel, k_rel), 0.0)
    o_intra = jnp.einsum("bchls,bcshv->bclhv", att, v)
    q_inter = q * jnp.exp(bcum)
    k_end = k * jnp.exp(bcum[:, :, -1:] - bcum)
    chunk_decay = jnp.exp(bcum[:, :, -1])

    def step(S, inp):
        qc, kc, vc, dc = inp
        o = jnp.einsum("blhd,bhdv->blhv", qc, S)
        S = S * dc[..., None] + jnp.einsum("blhd,blhv->bhdv", kc, vc)
        return S, o

    init = jnp.zeros((b, GLA_HEADS, GLA_HEAD_K, GLA_HEAD_V), jnp.float32)
    _, o_inter = lax.scan(step, init, (jnp.moveaxis(q_inter, 1, 0), jnp.moveaxis(k_end, 1, 0),
                                       jnp.moveaxis(v, 1, 0), jnp.moveaxis(chunk_decay, 1, 0)))
    o = o_intra + jnp.moveaxis(o_inter, 0, 1)
    return o.reshape(b, s, GLA_HEADS, GLA_HEAD_V)


def token_mixer(h, w_in, conv_w, conv_b, dt_bias, a_log, d_skip, ssd_norm_g,
                gla_w_gate, gla_b_gate, gla_norm_g, w_out):
    b, s, _ = h.shape
    f32 = jnp.float32
    offsets = np.cumsum(IN_SPLITS)[:-1].tolist()
    proj = h @ w_in
    z, xbc, dt_raw, q, k, v, r, g_lr, m_a, m_b = jnp.split(proj, offsets, axis=-1)
    xbc = jax.nn.silu(causal_depthwise_conv(xbc, conv_w, conv_b))
    xs, bm, cm = jnp.split(xbc, [SSD_INNER, SSD_INNER + SSD_GROUPS * SSD_STATE], axis=-1)
    dt = jax.nn.softplus(dt_raw.astype(f32) + dt_bias.astype(f32))
    a = -jnp.exp(a_log.astype(f32))
    xh = xs.reshape(b, s, SSD_HEADS, SSD_HEAD_DIM).astype(f32)
    y = ssd_chunked(xh, dt, a,
                    bm.reshape(b, s, SSD_GROUPS, SSD_STATE).astype(f32),
                    cm.reshape(b, s, SSD_GROUPS, SSD_STATE).astype(f32))
    y = y + d_skip.astype(f32)[:, None] * xh
    y_ssd = gated_group_rms_norm(y.reshape(b, s, SSD_INNER), z, ssd_norm_g)
    gk = jax.nn.log_sigmoid((g_lr @ gla_w_gate + gla_b_gate).astype(f32)) / GLA_GATE_NORM
    o = gla_chunked(q.reshape(b, s, GLA_HEADS, GLA_HEAD_K).astype(f32),
                    k.reshape(b, s, GLA_HEADS, GLA_HEAD_K).astype(f32),
                    v.reshape(b, s, GLA_HEADS, GLA_HEAD_V).astype(f32),
                    gk.reshape(b, s, GLA_HEADS, GLA_HEAD_K))
    o = rms_norm(o, gla_norm_g).astype(h.dtype) * jax.nn.silu(r.reshape(b, s, GLA_HEADS, GLA_HEAD_V))
    y_gla = o.reshape(b, s, GLA_VAL_DIM)
    mixed = jax.nn.sigmoid(m_a) * y_ssd + jax.nn.sigmoid(m_b) * y_gla
    return mixed @ w_out


def route(h2d, w_router, router_bias):
    T = h2d.shape[0]
    scores = jax.nn.sigmoid((h2d @ w_router).astype(jnp.float32))
    choice = scores + router_bias.astype(jnp.float32)
    grouped = choice.reshape(T, N_EXPERT_GROUPS, N_EXPERTS // N_EXPERT_GROUPS)
    group_score = lax.top_k(grouped, 2)[0].sum(-1)
    _, top_groups = lax.top_k(group_score, TOPK_GROUPS)
    group_mask = jnp.any(top_groups[..., None] == jnp.arange(N_EXPERT_GROUPS), axis=-2)
    expert_mask = jnp.repeat(group_mask, N_EXPERTS // N_EXPERT_GROUPS, axis=-1)
    _, idx = lax.top_k(jnp.where(expert_mask, choice, -jnp.inf), TOP_K)
    w = jnp.take_along_axis(scores, idx, axis=-1)
    w = w / (w.sum(-1, keepdims=True) + 1e-20) * ROUTED_SCALE
    return idx, w


def routed_experts(h2d, idx, w, w_e_gate, w_e_up, w_e_down):
    T, D = h2d.shape
    A = T * TOP_K
    e_flat = idx.reshape(-1)
    tok_flat = jnp.repeat(jnp.arange(T, dtype=jnp.int32), TOP_K)
    w_flat = w.reshape(-1)
    order = jnp.argsort(e_flat)
    e_sorted, tok_sorted, w_sorted = e_flat[order], tok_flat[order], w_flat[order]
    counts = jnp.bincount(e_flat, length=N_EXPERTS)
    starts = jnp.cumsum(counts) - counts
    padded = (counts + MOE_BLOCK - 1) // MOE_BLOCK * MOE_BLOCK
    pends = jnp.cumsum(padded)
    pstarts = pends - padded
    dest = pstarts[e_sorted] + (jnp.arange(A) - starts[e_sorted])
    nb = -(-A // MOE_BLOCK) + N_EXPERTS
    P = nb * MOE_BLOCK
    row_tok = jnp.full((P,), T, jnp.int32).at[dest].set(tok_sorted)
    row_w = jnp.zeros((P,), w.dtype).at[dest].set(w_sorted)
    block_exp = jnp.minimum(jnp.searchsorted(pends, jnp.arange(nb) * MOE_BLOCK, side="right"),
                            N_EXPERTS - 1)
    h_pad = jnp.concatenate([h2d, jnp.zeros((1, D), h2d.dtype)], axis=0)

    def expert_block(args):
        e, toks, wb = args
        xb = h_pad[toks]
        hid = jax.nn.silu(xb @ w_e_gate[e]) * (xb @ w_e_up[e])
        return ((hid @ w_e_down[e]) * wb[:, None]).astype(h2d.dtype)

    out = lax.map(expert_block, (block_exp, row_tok.reshape(nb, MOE_BLOCK), row_w.reshape(nb, MOE_BLOCK)))
    return jax.ops.segment_sum(out.reshape(P, D), row_tok, num_segments=T + 1)[:T]


def moe_ffn(h, w_router, router_bias, w_e_gate, w_e_up, w_e_down, w_s_gate, w_s_up, w_s_down):
    b, s, d = h.shape
    h2d = h.reshape(b * s, d)
    idx, w = route(h2d, w_router, router_bias)
    shared = (jax.nn.silu(h2d @ w_s_gate) * (h2d @ w_s_up)) @ w_s_down
    y = routed_experts(h2d, idx, w, w_e_gate, w_e_up, w_e_down) + shared
    return y.reshape(b, s, d)


def setup_inputs(seed: int = 0) -> dict:
    key = jax.random.key(seed)
    ks = jax.random.split(key, 32)
    f32 = jnp.float32
    L = DEPTH

    def nrm(k, shape, scale):
        return jax.random.normal(k, shape, f32) * scale

    dt0 = jnp.exp(jax.random.uniform(ks[8], (L, SSD_HEADS), f32, math.log(1e-3), math.log(1e-1)))
    return {
        "x": nrm(ks[0], (BATCH, SEQ, D_MODEL), 1.0),
        "c": nrm(ks[1], (BATCH, D_MODEL), 1.0),
        "w_ada": nrm(ks[2], (L, D_MODEL, 6 * D_MODEL), 0.5 * D_MODEL ** -0.5),
        "b_ada": nrm(ks[3], (L, 6 * D_MODEL), 0.01),
        "norm1_g": 1.0 + nrm(ks[4], (L, D_MODEL), 0.02),
        "w_in": nrm(ks[5], (L, D_MODEL, IN_DIM), D_MODEL ** -0.5),
        "conv_w": nrm(ks[6], (L, SSD_CONV, SSD_XBC), SSD_CONV ** -0.5),
        "conv_b": nrm(ks[7], (L, SSD_XBC), 0.01),
        "dt_bias": dt0 + jnp.log(-jnp.expm1(-dt0)),
        "a_log": jnp.log(jax.random.uniform(ks[9], (L, SSD_HEADS), f32, 1.0, 16.0)),
        "d_skip": 1.0 + nrm(ks[10], (L, SSD_HEADS), 0.1),
        "ssd_norm_g": 1.0 + nrm(ks[11], (L, SSD_INNER), 0.02),
        "gla_w_gate": nrm(ks[12], (L, GLA_GATE_RANK, GLA_KEY_DIM), GLA_GATE_RANK ** -0.5),
        "gla_b_gate": nrm(ks[13], (L, GLA_KEY_DIM), 0.01),
        "gla_norm_g": 1.0 + nrm(ks[14], (L, GLA_HEAD_V), 0.02),
        "w_out": nrm(ks[15], (L, D_MODEL, D_MODEL), D_MODEL ** -0.5),
        "norm2_g": 1.0 + nrm(ks[16], (L, D_MODEL), 0.02),
        "w_router": nrm(ks[17], (L, D_MODEL, N_EXPERTS), D_MODEL ** -0.5),
        "router_bias": nrm(ks[18], (L, N_EXPERTS), 0.01),
        "w_e_gate": nrm(ks[19], (L, N_EXPERTS, D_MODEL, EXPERT_DIM), D_MODEL ** -0.5),
        "w_e_up": nrm(ks[20], (L, N_EXPERTS, D_MODEL, EXPERT_DIM), D_MODEL ** -0.5),
        "w_e_down": nrm(ks[21], (L, N_EXPERTS, EXPERT_DIM, D_MODEL), EXPERT_DIM ** -0.5),
        "w_s_gate": nrm(ks[22], (L, D_MODEL, SHARED_DIM), D_MODEL ** -0.5),
        "w_s_up": nrm(ks[23], (L, D_MODEL, SHARED_DIM), D_MODEL ** -0.5),
        "w_s_down": nrm(ks[24], (L, SHARED_DIM, D_MODEL), SHARED_DIM ** -0.5),
        "normf_g": 1.0 + nrm(ks[25], (D_MODEL,), 0.02),
    }


def reference(x, c, w_ada, b_ada, norm1_g, w_in, conv_w, conv_b, dt_bias, a_log, d_skip,
              ssd_norm_g, gla_w_gate, gla_b_gate, gla_norm_g, w_out, norm2_g, w_router,
              router_bias, w_e_gate, w_e_up, w_e_down, w_s_gate, w_s_up, w_s_down, normf_g):
    c_act = jax.nn.silu(c)
    for layer in range(DEPTH):
        mod = (c_act @ w_ada[layer] + b_ada[layer])[:, None, :]
        sh1, sc1, g1, sh2, sc2, g2 = jnp.split(mod, 6, axis=-1)
        h = rms_norm(x, norm1_g[layer]) * (1.0 + sc1) + sh1
        x = x + g1 * token_mixer(h, w_in[layer], conv_w[layer], conv_b[layer], dt_bias[layer],
                                 a_log[layer], d_skip[layer], ssd_norm_g[layer], gla_w_gate[layer],
                                 gla_b_gate[layer], gla_norm_g[layer], w_out[layer])
        h = rms_norm(x, norm2_g[layer]) * (1.0 + sc2) + sh2
        x = x + g2 * moe_ffn(h, w_router[layer], router_bias[layer], w_e_gate[layer], w_e_up[layer],
                             w_e_down[layer], w_s_gate[layer], w_s_up[layer], w_s_down[layer])
    return rms_norm(x, normf_g)
```

```python
import functools

import jax
import jax.numpy as jnp
from jax import lax
from jax.experimental import pallas as pl
from jax.experimental.pallas import tpu as pltpu

F32 = jnp.float32
BF16 = jnp.bfloat16
I32 = jnp.int32

D_MODEL = 2048
SEQ = 8192
NORM_EPS = 1e-6
SSD_HEADS = 32
SSD_HEAD_DIM = 64
SSD_INNER = 2048
SSD_GROUPS = 4
SSD_STATE = 128
SSD_CHUNK = 128
HEADS_PER_GROUP = SSD_HEADS // SSD_GROUPS
GROUP_COLS = HEADS_PER_GROUP * SSD_HEAD_DIM
GLA_HEADS = 4
GLA_KEY_DIM = 1024
GLA_VAL_DIM = 2048
GLA_HEAD_K = 256
GLA_HEAD_V = 512
GLA_GATE_RANK = 16
GLA_GATE_NORM = 16.0
GLA_CHUNK = 64
N_EXPERTS = 64
TOP_K = 8
N_EXPERT_GROUPS = 8
TOPK_GROUPS = 4
EXPERT_DIM = 512
ROUTED_SCALE = 2.5

VMEM_LIMIT_BYTES = 56 * 1024 * 1024

COL_XS, COL_Z, COL_V, COL_R, COL_MA, COL_MB, COL_BC, COL_Q, COL_K = (
    0, 2048, 4096, 6144, 8192, 10240, 12288, 13312, 14336)
PROJ_COLS = 15360
SMALL_COLS = 128

MOE_BM = 256
MOE_NB = SEQ * TOP_K // MOE_BM + N_EXPERTS
MOE_ROWS = MOE_NB * MOE_BM


def _cparams(sem):
    return pltpu.CompilerParams(dimension_semantics=sem, vmem_limit_bytes=VMEM_LIMIT_BYTES)


def _split3(a):
    hi = a.astype(BF16)
    r1 = a - hi.astype(F32)
    mid = r1.astype(BF16)
    lo = (r1 - mid.astype(F32)).astype(BF16)
    return hi, mid, lo


def _dot(a, b):
    return jnp.dot(a, b, preferred_element_type=F32)


def _dot_nt(a, b):
    return lax.dot_general(a, b, (((1,), (1,)), ((), ())), preferred_element_type=F32)


def _dot_tn(a, b):
    return lax.dot_general(a, b, (((0,), (0,)), ((), ())), preferred_element_type=F32)


def _sel_dot(sel_bf16, a):
    hi, mid, lo = _split3(a)
    return _dot(sel_bf16, hi) + _dot(sel_bf16, mid) + _dot(sel_bf16, lo)


def _dot_sel(a, sel_bf16):
    hi, mid, lo = _split3(a)
    return _dot(hi, sel_bf16) + _dot(mid, sel_bf16) + _dot(lo, sel_bf16)


def _dot3(a, b):
    ah = a.astype(BF16)
    al = (a - ah.astype(F32)).astype(BF16)
    bh = b.astype(BF16)
    bl = (b - bh.astype(F32)).astype(BF16)
    return _dot(ah, bh) + _dot(ah, bl) + _dot(al, bh)


def _dot3_nt(a, b):
    ah = a.astype(BF16)
    al = (a - ah.astype(F32)).astype(BF16)
    bh = b.astype(BF16)
    bl = (b - bh.astype(F32)).astype(BF16)
    return _dot_nt(ah, bh) + _dot_nt(ah, bl) + _dot_nt(al, bh)


def _silu(x):
    return x * jax.nn.sigmoid(x)


def _softplus(x):
    return jnp.maximum(x, 0.0) + jnp.log1p(jnp.exp(-jnp.abs(x)))


def _rms(x, g):
    ms = jnp.mean(x * x, axis=-1, keepdims=True)
    return x * lax.rsqrt(ms + NORM_EPS) * g


ADA_TN = 1024


def _ada_kernel(c_ref, w_ref, b_ref, o_ref):
    ca = _silu(c_ref[...])
    o_ref[...] = jnp.sum(w_ref[...] * ca, axis=0, keepdims=True) + b_ref[...]


def _ada(c_col, w_ada, b_ada):
    n = w_ada.shape[1]
    return pl.pallas_call(
        _ada_kernel,
        grid=(n // ADA_TN,),
        in_specs=[
            pl.BlockSpec((D_MODEL, 1), lambda j: (0, 0)),
            pl.BlockSpec((D_MODEL, ADA_TN), lambda j: (0, j)),
            pl.BlockSpec((1, ADA_TN), lambda j: (0, j)),
        ],
        out_specs=pl.BlockSpec((1, ADA_TN), lambda j: (0, j)),
        out_shape=jax.ShapeDtypeStruct((1, n), F32),
        compiler_params=_cparams(("arbitrary",)),
        name="ada_mod",
    )(c_col, w_ada, b_ada)


INPROJ_TM = 1024
INPROJ_TN = 1024


def _inproj_kernel(x_ref, g_ref, sc_ref, sh_ref, wb_ref, ws_ref, o_ref, os_ref, h_ref):
    @pl.when(pl.program_id(1) == 0)
    def _():
        h = _rms(x_ref[...], g_ref[...]) * (1.0 + sc_ref[...]) + sh_ref[...]
        hb = h.astype(BF16)
        h_ref[...] = hb
        os_ref[...] = _dot(hb, ws_ref[...])

    o_ref[...] = _dot(h_ref[...], wb_ref[...]).astype(BF16)


def _inproj(x2d, g, sc, sh, w_big, w_small):
    vec = pl.BlockSpec((1, D_MODEL), lambda i, j: (0, 0))
    return pl.pallas_call(
        _inproj_kernel,
        grid=(SEQ // INPROJ_TM, PROJ_COLS // INPROJ_TN),
        in_specs=[
            pl.BlockSpec((INPROJ_TM, D_MODEL), lambda i, j: (i, 0)),
            vec, vec, vec,
            pl.BlockSpec((D_MODEL, INPROJ_TN), lambda i, j: (0, j)),
            pl.BlockSpec((D_MODEL, SMALL_COLS), lambda i, j: (0, 0)),
        ],
        out_specs=[
            pl.BlockSpec((INPROJ_TM, INPROJ_TN), lambda i, j: (i, j)),
            pl.BlockSpec((INPROJ_TM, SMALL_COLS), lambda i, j: (i, 0)),
        ],
        out_shape=[
            jax.ShapeDtypeStruct((SEQ, PROJ_COLS), BF16),
            jax.ShapeDtypeStruct((SEQ, SMALL_COLS), F32),
        ],
        scratch_shapes=[pltpu.VMEM((INPROJ_TM, D_MODEL), BF16)],
        compiler_params=_cparams(("arbitrary", "arbitrary")),
        name="inproj",
    )(x2d, g, sc, sh, w_big, w_small)


def _causal_conv_silu(cur, prev, w_ref, b_ref):
    row = lax.broadcasted_iota(I32, cur.shape, 0)
    acc = cur * w_ref[3:4, :] + b_ref[...]
    for s in (1, 2, 3):
        shifted = jnp.where(row >= s, pltpu.roll(cur, s, 0), pltpu.roll(prev, s, 0))
        acc = acc + shifted * w_ref[3 - s:4 - s, :]
    return _silu(acc)


def _ssd_kernel(xs_ref, bc_ref, z_ref, ma_ref, sm_ref, cwx_ref, cwb_ref, cbx_ref, cbb_ref,
                dtb_ref, alog_ref, dexp_ref, ng_ref, o_ref, px_ref, pb_ref, st_ref):
    L = SSD_CHUNK

    @pl.when(pl.program_id(0) == 0)
    def _():
        px_ref[...] = jnp.zeros_like(px_ref)
        pb_ref[...] = jnp.zeros_like(pb_ref)
        st_ref[...] = jnp.zeros_like(st_ref)

    xs_raw = xs_ref[...].astype(F32)
    bc_raw = bc_ref[...].astype(F32)
    xs = _causal_conv_silu(xs_raw, px_ref[...], cwx_ref, cbx_ref)
    bc = _causal_conv_silu(bc_raw, pb_ref[...], cwb_ref, cbb_ref)
    px_ref[...] = xs_raw
    pb_ref[...] = bc_raw

    dt = _softplus(sm_ref[:, 0:SSD_HEADS] + dtb_ref[...])
    d_a = dt * (-jnp.exp(alog_ref[...]))
    ri = lax.broadcasted_iota(I32, (L, L), 0)
    ci = lax.broadcasted_iota(I32, (L, L), 1)
    causal = ri >= ci
    tril = jnp.where(causal, 1.0, 0.0).astype(BF16)
    acum = _sel_dot(tril, d_a)
    hi32 = lax.broadcasted_iota(I32, (SSD_HEADS, SSD_HEADS), 0)
    hj32 = lax.broadcasted_iota(I32, (SSD_HEADS, SSD_HEADS), 1)
    eye32 = jnp.where(hi32 == hj32, 1.0, 0.0).astype(BF16)
    a_hi, a_mid, a_lo = _split3(acum)
    acum_t = _dot_nt(eye32, a_hi) + _dot_nt(eye32, a_mid) + _dot_nt(eye32, a_lo)
    last = acum[L - 1:L, :]
    e_a = jnp.exp(acum)
    to_end = jnp.exp(last - acum)
    eh = lax.broadcasted_iota(I32, (SSD_HEADS, SSD_INNER), 0)
    ec = lax.broadcasted_iota(I32, (SSD_HEADS, SSD_INNER), 1)
    expand = jnp.where((ec >> 6) == eh, 1.0, 0.0).astype(BF16)
    dt_x = _dot_sel(dt, expand)
    dte_x = _dot_sel(dt * to_end, expand)
    ea_x = _dot_sel(e_a, expand)
    cd_x = ea_x[L - 1:L, :]

    xdt = xs * dt_x
    lane = lax.broadcasted_iota(I32, xdt.shape, 1)
    left = (lane & 64) == 0
    xdt_l = jnp.where(left, xdt, 0.0).astype(BF16)
    xdt_r = jnp.where(left, 0.0, xdt).astype(BF16)
    xdte = (xs * dte_x).astype(BF16)

    y_diag = []
    y_off = []
    for g in range(SSD_GROUPS):
        bg = bc[:, g * SSD_STATE:(g + 1) * SSD_STATE].astype(BF16)
        cg = bc[:, 512 + g * SSD_STATE:512 + (g + 1) * SSD_STATE].astype(BF16)
        cb = _dot_nt(cg, bg)
        gsl = slice(g * GROUP_COLS, (g + 1) * GROUP_COLS)
        st = st_ref[g]
        y_off.append(_dot(cg, st.astype(BF16)))
        st_ref[g] = st * cd_x[:, gsl] + _dot_tn(bg, xdte[:, gsl])
        for p in range(HEADS_PER_GROUP // 2):
            ms = []
            for h in (g * HEADS_PER_GROUP + 2 * p, g * HEADS_PER_GROUP + 2 * p + 1):
                seg = acum[:, h:h + 1] - acum_t[h:h + 1, :]
                decay = jnp.exp(jnp.where(causal, seg, -jnp.inf))
                ms.append((cb * decay).astype(BF16))
            m_cat = jnp.concatenate(ms, axis=1)
            psl = slice((g * 4 + p) * 128, (g * 4 + p + 1) * 128)
            x_cat = jnp.concatenate([xdt_l[:, psl], xdt_r[:, psl]], axis=0)
            y_diag.append(_dot(m_cat, x_cat))

    y = (jnp.concatenate(y_diag, axis=1) + jnp.concatenate(y_off, axis=1) * ea_x
         + dexp_ref[...] * xs)
    yf = y * _silu(z_ref[...].astype(F32))
    outs = []
    for g in range(SSD_GROUPS):
        seg = yf[:, g * GROUP_COLS:(g + 1) * GROUP_COLS]
        ms = jnp.mean(seg * seg, axis=-1, keepdims=True)
        outs.append(seg * lax.rsqrt(ms + NORM_EPS))
    y_ssd = jnp.concatenate(outs, axis=1) * ng_ref[...]
    o_ref[...] = (jax.nn.sigmoid(ma_ref[...].astype(F32)) * y_ssd).astype(BF16)


def _ssd(proj, small, conv_w, conv_b, dt_bias, a_log, d_exp, norm_g):
    L = SSD_CHUNK
    full = lambda shape: pl.BlockSpec(shape, lambda i: (0, 0))
    return pl.pallas_call(
        _ssd_kernel,
        grid=(SEQ // L,),
        in_specs=[
            pl.BlockSpec((L, 2048), lambda i: (i, COL_XS // 2048)),
            pl.BlockSpec((L, 1024), lambda i: (i, COL_BC // 1024)),
            pl.BlockSpec((L, 2048), lambda i: (i, COL_Z // 2048)),
            pl.BlockSpec((L, 2048), lambda i: (i, COL_MA // 2048)),
            pl.BlockSpec((L, SMALL_COLS), lambda i: (i, 0)),
            pl.BlockSpec((4, 2048), lambda i: (0, 0)),
            pl.BlockSpec((4, 1024), lambda i: (0, 2)),
            pl.BlockSpec((1, 2048), lambda i: (0, 0)),
            pl.BlockSpec((1, 1024), lambda i: (0, 2)),
            full((1, SSD_HEADS)), full((1, SSD_HEADS)),
            full((1, SSD_INNER)), full((1, SSD_INNER)),
        ],
        out_specs=pl.BlockSpec((L, SSD_INNER), lambda i: (i, 0)),
        out_shape=jax.ShapeDtypeStruct((SEQ, SSD_INNER), BF16),
        scratch_shapes=[
            pltpu.VMEM((L, 2048), F32),
            pltpu.VMEM((L, 1024), F32),
            pltpu.VMEM((SSD_GROUPS, SSD_STATE, GROUP_COLS), F32),
        ],
        compiler_params=_cparams(("arbitrary",)),
        name="ssd",
    )(proj, proj, proj, proj, small, conv_w, conv_w, conv_b, conv_b, dt_bias, a_log, d_exp, norm_g)


def _gla_kernel(q_ref, k_ref, v_ref, r_ref, mb_ref, sm_ref, wg_ref, bg_ref, ng_ref, o_ref, s_ref):
    L = GLA_CHUNK

    @pl.when(pl.program_id(0) == 0)
    def _():
        s_ref[...] = jnp.zeros_like(s_ref)

    glr = sm_ref[:, SSD_HEADS:SSD_HEADS + GLA_GATE_RANK]
    pre = _dot3(glr, wg_ref[...]) + bg_ref[...]
    gk = (jnp.minimum(pre, 0.0) - jnp.log1p(jnp.exp(-jnp.abs(pre)))) / GLA_GATE_NORM
    ri = lax.broadcasted_iota(I32, (L, L), 0)
    ci = lax.broadcasted_iota(I32, (L, L), 1)
    causal = ri >= ci
    tril = jnp.where(causal, 1.0, 0.0).astype(BF16)
    bcum = _sel_dot(tril, gk)
    bmid = bcum[L // 2:L // 2 + 1, :]
    blast = bcum[L - 1:L, :]
    q = q_ref[...].astype(F32) * (GLA_HEAD_K ** -0.5)
    k = k_ref[...].astype(F32)
    q_rel = (q * jnp.exp(bcum - bmid)).astype(BF16)
    k_rel = (k * jnp.exp(bmid - bcum)).astype(BF16)
    q_int = (q * jnp.exp(bcum)).astype(BF16)
    k_end = (k * jnp.exp(blast - bcum)).astype(BF16)
    dec = jnp.exp(blast)
    v = v_ref[...]

    outs = []
    for h in range(GLA_HEADS):
        ks = slice(h * GLA_HEAD_K, (h + 1) * GLA_HEAD_K)
        vs = slice(h * GLA_HEAD_V, (h + 1) * GLA_HEAD_V)
        att = jnp.where(causal, _dot_nt(q_rel[:, ks], k_rel[:, ks]), 0.0)
        s_t = s_ref[h]
        o_h = _dot(att.astype(BF16), v[:, vs]) + _dot_nt(q_int[:, ks], s_t.astype(BF16))
        s_ref[h] = s_t * dec[:, ks] + _dot_tn(v[:, vs], k_end[:, ks])
        outs.append(_rms(o_h, ng_ref[...]))
    o = jnp.concatenate(outs, axis=1)
    o = o * _silu(r_ref[...].astype(F32)) * jax.nn.sigmoid(mb_ref[...].astype(F32))
    o_ref[...] = o.astype(BF16)


def _gla(proj, small, w_gate, b_gate, norm_g):
    L = GLA_CHUNK
    full = lambda shape: pl.BlockSpec(shape, lambda i: (0, 0))
    return pl.pallas_call(
        _gla_kernel,
        grid=(SEQ // L,),
        in_specs=[
            pl.BlockSpec((L, 1024), lambda i: (i, COL_Q // 1024)),
            pl.BlockSpec((L, 1024), lambda i: (i, COL_K // 1024)),
            pl.BlockSpec((L, 2048), lambda i: (i, COL_V // 2048)),
            pl.BlockSpec((L, 2048), lambda i: (i, COL_R // 2048)),
            pl.BlockSpec((L, 2048), lambda i: (i, COL_MB // 2048)),
            pl.BlockSpec((L, SMALL_COLS), lambda i: (i, 0)),
            full((GLA_GATE_RANK, GLA_KEY_DIM)), full((1, GLA_KEY_DIM)), full((1, GLA_HEAD_V)),
        ],
        out_specs=pl.BlockSpec((L, GLA_VAL_DIM), lambda i: (i, 0)),
        out_shape=jax.ShapeDtypeStruct((SEQ, GLA_VAL_DIM), BF16),
        scratch_shapes=[pltpu.VMEM((GLA_HEADS, GLA_HEAD_V, GLA_HEAD_K), F32)],
        compiler_params=_cparams(("arbitrary",)),
        name="gla",
    )(proj, proj, proj, proj, proj, small, w_gate, b_gate, norm_g)


OUTPROJ_TM = 256


def _outproj_kernel(a_ref, b_ref, x_ref, w_ref, g1_ref, n2_ref, sc_ref, sh_ref, wr_ref,
                    x1_ref, h2_ref, lg_ref):
    mixed = (a_ref[...].astype(F32) + b_ref[...].astype(F32)).astype(BF16)
    x1 = x_ref[...] + g1_ref[...] * _dot(mixed, w_ref[...])
    x1_ref[...] = x1
    h2 = _rms(x1, n2_ref[...]) * (1.0 + sc_ref[...]) + sh_ref[...]
    h2_ref[...] = h2.astype(BF16)
    lg_ref[...] = _dot3_nt(wr_ref[...], h2)


def _outproj(ya, yb, x2d, w_out_bf, g1, n2g, sc2, sh2, w_router_t):
    tm = OUTPROJ_TM
    vec = pl.BlockSpec((1, D_MODEL), lambda i: (0, 0))
    row = pl.BlockSpec((tm, D_MODEL), lambda i: (i, 0))
    return pl.pallas_call(
        _outproj_kernel,
        grid=(SEQ // tm,),
        in_specs=[row, row, row,
                  pl.BlockSpec((D_MODEL, D_MODEL), lambda i: (0, 0)),
                  vec, vec, vec, vec,
                  pl.BlockSpec((N_EXPERTS, D_MODEL), lambda i: (0, 0))],
        out_specs=[row, row, pl.BlockSpec((N_EXPERTS, tm), lambda i: (0, i))],
        out_shape=[jax.ShapeDtypeStruct((SEQ, D_MODEL), F32),
                   jax.ShapeDtypeStruct((SEQ, D_MODEL), BF16),
                   jax.ShapeDtypeStruct((N_EXPERTS, SEQ), F32)],
        compiler_params=_cparams(("arbitrary",)),
        name="outproj_router",
    )(ya, yb, x2d, w_out_bf, g1, n2g, sc2, sh2, w_router_t)


ROUTE_TN = 512


def _route_kernel(lg_ref, bias_ref, ek_ref, wk_ref, rk_ref, cnt_ref, carry_ref):
    tn = ROUTE_TN
    gsz = N_EXPERTS // N_EXPERT_GROUPS

    @pl.when(pl.program_id(0) == 0)
    def _():
        carry_ref[...] = jnp.zeros_like(carry_ref)

    s = jax.nn.sigmoid(lg_ref[...])
    choice = s + bias_ref[...]
    r8 = lax.broadcasted_iota(I32, (gsz, tn), 0)
    neg = -jnp.inf

    def top1(cur, rows, nrows):
        m = jnp.max(cur, axis=0, keepdims=True)
        idx = jnp.min(jnp.where(cur == m, rows, nrows), axis=0, keepdims=True)
        return m, idx

    gscores = []
    for g in range(N_EXPERT_GROUPS):
        blk = choice[g * gsz:(g + 1) * gsz, :]
        m1, i1 = top1(blk, r8, gsz)
        m2, _ = top1(jnp.where(r8 == i1, neg, blk), r8, gsz)
        gscores.append(m1 + m2)
    cur = jnp.concatenate(gscores, axis=0)
    gsel = jnp.zeros((N_EXPERT_GROUPS, tn), F32)
    for _ in range(TOPK_GROUPS):
        _, idx = top1(cur, r8, N_EXPERT_GROUPS)
        hit = r8 == idx
        gsel = jnp.where(hit, 1.0, gsel)
        cur = jnp.where(hit, neg, cur)
    emask = jnp.concatenate(
        [jnp.broadcast_to(gsel[g:g + 1, :], (gsz, tn)) for g in range(N_EXPERT_GROUPS)], axis=0)
    rows = lax.broadcasted_iota(I32, (N_EXPERTS, tn), 0)
    cur = jnp.where(emask > 0.5, choice, neg)
    sel = jnp.zeros((N_EXPERTS, tn), F32)
    hits, eks, wks = [], [], []
    for _ in range(TOP_K):
        _, idx = top1(cur, rows, N_EXPERTS)
        hit = rows == idx
        hits.append(hit)
        eks.append(idx)
        wks.append(jnp.sum(jnp.where(hit, s, 0.0), axis=0, keepdims=True))
        cur = jnp.where(hit, neg, cur)
        sel = jnp.where(hit, 1.0, sel)
    wsum = wks[0]
    for w in wks[1:]:
        wsum = wsum + w
    scale = ROUTED_SCALE / (wsum + 1e-20)

    ti = lax.broadcasted_iota(I32, (tn, tn), 0)
    tj = lax.broadcasted_iota(I32, (tn, tn), 1)
    before = jnp.where(ti < tj, 1.0, 0.0).astype(BF16)
    rank = _dot(sel.astype(BF16), before) + carry_ref[...]
    carry_ref[...] = carry_ref[...] + jnp.sum(sel, axis=1, keepdims=True)
    rks = [jnp.sum(jnp.where(hit, rank, 0.0), axis=0, keepdims=True) for hit in hits]

    ek_ref[...] = jnp.concatenate(eks, axis=0)
    wk_ref[...] = jnp.concatenate(wks, axis=0) * scale
    rk_ref[...] = jnp.concatenate(rks, axis=0).astype(I32)
    cnt_ref[...] = carry_ref[...].astype(I32)


def _route(logits_t, bias_col):
    tn = ROUTE_TN
    kt = pl.BlockSpec((TOP_K, tn), lambda i: (0, i))
    return pl.pallas_call(
        _route_kernel,
        grid=(SEQ // tn,),
        in_specs=[pl.BlockSpec((N_EXPERTS, tn), lambda i: (0, i)),
                  pl.BlockSpec((N_EXPERTS, 1), lambda i: (0, 0))],
        out_specs=[kt, kt, kt, pl.BlockSpec((N_EXPERTS, 1), lambda i: (0, 0))],
        out_shape=[jax.ShapeDtypeStruct((TOP_K, SEQ), I32),
                   jax.ShapeDtypeStruct((TOP_K, SEQ), F32),
                   jax.ShapeDtypeStruct((TOP_K, SEQ), I32),
                   jax.ShapeDtypeStruct((N_EXPERTS, 1), I32)],
        scratch_shapes=[pltpu.VMEM((N_EXPERTS, 1), F32)],
        compiler_params=_cparams(("arbitrary",)),
        name="route",
    )(logits_t, bias_col)


def _experts_kernel(bexp_ref, nused_ref, x_ref, wg_ref, wu_ref, wd_ref, o_ref,
                    wgb_ref, wub_ref, wdb_ref):
    b = pl.program_id(0)
    prev = bexp_ref[jnp.maximum(b - 1, 0)]
    fresh = jnp.logical_or(b == 0, bexp_ref[b] != prev)

    @pl.when(jnp.logical_and(fresh, b < nused_ref[0]))
    def _():
        wgb_ref[...] = wg_ref[...].astype(BF16)
        wub_ref[...] = wu_ref[...].astype(BF16)
        wdb_ref[...] = wd_ref[...].astype(BF16)

    @pl.when(b < nused_ref[0])
    def _():
        x = x_ref[...]
        gate = _dot(x, wgb_ref[...])
        up = _dot(x, wub_ref[...])
        hid = (_silu(gate) * up).astype(BF16)
        o_ref[...] = _dot(hid, wdb_ref[...]).astype(BF16)


def _experts(block_exp, nused, xs_perm, w_gate, w_up, w_down):
    bm = MOE_BM

    def xmap(b, be, nu):
        return (jnp.minimum(b, nu[0] - 1), 0)

    def wmap(b, be, nu):
        return (be[b], 0, 0)

    grid_spec = pltpu.PrefetchScalarGridSpec(
        num_scalar_prefetch=2,
        grid=(MOE_NB,),
        in_specs=[
            pl.BlockSpec((bm, D_MODEL), xmap),
            pl.BlockSpec((None, D_MODEL, EXPERT_DIM), wmap),
            pl.BlockSpec((None, D_MODEL, EXPERT_DIM), wmap),
            pl.BlockSpec((None, EXPERT_DIM, D_MODEL), wmap),
        ],
        out_specs=pl.BlockSpec((bm, D_MODEL), xmap),
        scratch_shapes=[
            pltpu.VMEM((D_MODEL, EXPERT_DIM), BF16),
            pltpu.VMEM((D_MODEL, EXPERT_DIM), BF16),
            pltpu.VMEM((EXPERT_DIM, D_MODEL), BF16),
        ],
    )
    return pl.pallas_call(
        _experts_kernel,
        grid_spec=grid_spec,
        out_shape=jax.ShapeDtypeStruct((MOE_ROWS, D_MODEL), BF16),
        compiler_params=_cparams(("arbitrary",)),
        name="experts",
    )(block_exp, nused, xs_perm, w_gate, w_up, w_down)


FINAL_TM = 512


def _final_kernel(h2_ref, x1_ref, y_ref, wg_ref, wu_ref, wd_ref, g2_ref, nf_ref, o_ref):
    h = h2_ref[...]
    hid = (_silu(_dot(h, wg_ref[...])) * _dot(h, wu_ref[...])).astype(BF16)
    y = y_ref[...] + _dot(hid, wd_ref[...])
    x2 = x1_ref[...] + g2_ref[...] * y
    o_ref[...] = _rms(x2, nf_ref[...])


def _final(h2, x1, y_moe, wsg, wsu, wsd, g2, nf):
    tm = FINAL_TM
    vec = pl.BlockSpec((1, D_MODEL), lambda i: (0, 0))
    row = pl.BlockSpec((tm, D_MODEL), lambda i: (i, 0))
    return pl.pallas_call(
        _final_kernel,
        grid=(SEQ // tm,),
        in_specs=[row, row, row,
                  pl.BlockSpec((D_MODEL, EXPERT_DIM), lambda i: (0, 0)),
                  pl.BlockSpec((D_MODEL, EXPERT_DIM), lambda i: (0, 0)),
                  pl.BlockSpec((EXPERT_DIM, D_MODEL), lambda i: (0, 0)),
                  vec, vec],
        out_specs=row,
        out_shape=jax.ShapeDtypeStruct((SEQ, D_MODEL), F32),
        compiler_params=_cparams(("arbitrary",)),
        name="shared_final",
    )(h2, x1, y_moe, wsg, wsu, wsd, g2, nf)


def kernel(x, c, w_ada, b_ada, norm1_g, w_in, conv_w, conv_b, dt_bias, a_log, d_skip, ssd_norm_g, gla_w_gate, gla_b_gate, gla_norm_g, w_out, norm2_g, w_router, router_bias, w_e_gate, w_e_up, w_e_down, w_s_gate, w_s_up, w_s_down, normf_g):
    layer = 0
    x2d = x.reshape(SEQ, D_MODEL)
    mod = _ada(c.reshape(D_MODEL, 1), w_ada[layer], b_ada[layer].reshape(1, -1))
    sh1, sc1, g1, sh2, sc2, g2 = [mod[:, i * D_MODEL:(i + 1) * D_MODEL] for i in range(6)]

    wi = w_in[layer]
    o_z, o_xbc, o_dt, o_q, o_k, o_v, o_r, o_g, o_ma, o_mb, o_end = (
        0, 2048, 5120, 5152, 6176, 7200, 9248, 11296, 11312, 13360, 15408)
    w_big = jnp.concatenate([
        wi[:, o_xbc:o_xbc + 2048], wi[:, o_z:o_xbc], wi[:, o_v:o_r], wi[:, o_r:o_g],
        wi[:, o_ma:o_mb], wi[:, o_mb:o_end], wi[:, o_xbc + 2048:o_dt], wi[:, o_q:o_k],
        wi[:, o_k:o_v]], axis=1).astype(BF16)
    w_small = jnp.concatenate([
        wi[:, o_dt:o_q], wi[:, o_g:o_ma],
        jnp.zeros((D_MODEL, SMALL_COLS - SSD_HEADS - GLA_GATE_RANK), F32)], axis=1).astype(BF16)

    proj, small = _inproj(x2d, norm1_g[layer].reshape(1, -1), sc1, sh1, w_big, w_small)

    d_exp = jnp.repeat(d_skip[layer], SSD_HEAD_DIM).reshape(1, -1)
    y_a = _ssd(proj, small, conv_w[layer], conv_b[layer].reshape(1, -1),
               dt_bias[layer].reshape(1, -1), a_log[layer].reshape(1, -1), d_exp,
               ssd_norm_g[layer].reshape(1, -1))
    y_b = _gla(proj, small, gla_w_gate[layer], gla_b_gate[layer].reshape(1, -1),
               gla_norm_g[layer].reshape(1, -1))

    x1, h2, logits_t = _outproj(y_a, y_b, x2d, w_out[layer].astype(BF16), g1,
                                norm2_g[layer].reshape(1, -1), sc2, sh2, w_router[layer].T)

    ek, wk, rk, counts = _route(logits_t, router_bias[layer].reshape(-1, 1))

    counts = counts.reshape(-1)
    padded = (counts + MOE_BM - 1) // MOE_BM * MOE_BM
    pends = jnp.cumsum(padded)
    pstarts = pends - padded
    block_exp = jnp.minimum(
        jnp.searchsorted(pends, jnp.arange(MOE_NB, dtype=I32) * MOE_BM, side="right"),
        N_EXPERTS - 1).astype(I32)
    nused = (pends[-1:] // MOE_BM).astype(I32)
    dest = pstarts[ek].astype(I32) + rk

    tok = jnp.broadcast_to(jnp.arange(SEQ, dtype=I32)[None, :], dest.shape)
    row_tok = jnp.zeros((MOE_ROWS,), I32).at[dest.reshape(-1)].set(tok.reshape(-1))
    xs_perm = h2[row_tok]

    eo = _experts(block_exp, nused, xs_perm, w_e_gate[layer], w_e_up[layer], w_e_down[layer])

    y_moe = jnp.einsum("kt,ktd->td", wk, eo[dest].astype(F32))

    out = _final(h2, x1, y_moe, w_s_gate[layer].astype(BF16), w_s_up[layer].astype(BF16),
                 w_s_down[layer].astype(BF16), g2, normf_g.reshape(1, -1))
    return out.reshape(x.shape)
```

```python
import functools

import jax
import jax.numpy as jnp
from jax import lax
from jax.experimental import pallas as pl
from jax.experimental.pallas import tpu as pltpu

F32 = jnp.float32
BF16 = jnp.bfloat16
I32 = jnp.int32

D_MODEL = 2048
SEQ = 8192
NORM_EPS = 1e-6
SSD_HEADS = 32
SSD_HEAD_DIM = 64
SSD_INNER = 2048
SSD_GROUPS = 4
SSD_STATE = 128
SSD_CHUNK = 128
HEADS_PER_GROUP = SSD_HEADS // SSD_GROUPS
GROUP_COLS = HEADS_PER_GROUP * SSD_HEAD_DIM
GLA_HEADS = 4
GLA_KEY_DIM = 1024
GLA_VAL_DIM = 2048
GLA_HEAD_K = 256
GLA_HEAD_V = 512
GLA_GATE_RANK = 16
GLA_GATE_NORM = 16.0
GLA_CHUNK = 64
N_EXPERTS = 64
TOP_K = 8
N_EXPERT_GROUPS = 8
TOPK_GROUPS = 4
EXPERT_DIM = 512
ROUTED_SCALE = 2.5

VMEM_LIMIT_BYTES = 56 * 1024 * 1024

COL_XS, COL_Z, COL_V, COL_R, COL_MA, COL_MB, COL_BC, COL_Q, COL_K = (
    0, 2048, 4096, 6144, 8192, 10240, 12288, 13312, 14336)
PROJ_COLS = 15360
SMALL_COLS = 128

MOE_BM = 256
MOE_NB = SEQ * TOP_K // MOE_BM + N_EXPERTS
MOE_ROWS = MOE_NB * MOE_BM


def _cparams(sem):
    return pltpu.CompilerParams(dimension_semantics=sem, vmem_limit_bytes=VMEM_LIMIT_BYTES)


def _split3(a):
    hi = a.astype(BF16)
    r1 = a - hi.astype(F32)
    mid = r1.astype(BF16)
    lo = (r1 - mid.astype(F32)).astype(BF16)
    return hi, mid, lo


def _dot(a, b):
    return jnp.dot(a, b, preferred_element_type=F32)


def _dot_nt(a, b):
    return lax.dot_general(a, b, (((1,), (1,)), ((), ())), preferred_element_type=F32)


def _dot_tn(a, b):
    return lax.dot_general(a, b, (((0,), (0,)), ((), ())), preferred_element_type=F32)


def _sel_dot(sel_bf16, a):
    hi, mid, lo = _split3(a)
    return _dot(sel_bf16, hi) + _dot(sel_bf16, mid) + _dot(sel_bf16, lo)


def _dot_sel(a, sel_bf16):
    hi, mid, lo = _split3(a)
    return _dot(hi, sel_bf16) + _dot(mid, sel_bf16) + _dot(lo, sel_bf16)


def _dot3(a, b):
    ah = a.astype(BF16)
    al = (a - ah.astype(F32)).astype(BF16)
    bh = b.astype(BF16)
    bl = (b - bh.astype(F32)).astype(BF16)
    return _dot(ah, bh) + _dot(ah, bl) + _dot(al, bh)


def _dot3_nt(a, b):
    ah = a.astype(BF16)
    al = (a - ah.astype(F32)).astype(BF16)
    bh = b.astype(BF16)
    bl = (b - bh.astype(F32)).astype(BF16)
    return _dot_nt(ah, bh) + _dot_nt(ah, bl) + _dot_nt(al, bh)


def _silu(x):
    return x * jax.nn.sigmoid(x)


def _softplus(x):
    return jnp.maximum(x, 0.0) + jnp.log1p(jnp.exp(-jnp.abs(x)))


def _rms(x, g):
    ms = jnp.mean(x * x, axis=-1, keepdims=True)
    return x * lax.rsqrt(ms + NORM_EPS) * g


ADA_TN = 1024


def _ada_kernel(c_ref, w_ref, b_ref, o_ref):
    ca = _silu(c_ref[...])
    o_ref[...] = jnp.sum(w_ref[...] * ca, axis=0, keepdims=True) + b_ref[...]


def _ada(c_col, w_ada, b_ada):
    n = w_ada.shape[1]
    return pl.pallas_call(
        _ada_kernel,
        grid=(n // ADA_TN,),
        in_specs=[
            pl.BlockSpec((D_MODEL, 1), lambda j: (0, 0)),
            pl.BlockSpec((D_MODEL, ADA_TN), lambda j: (0, j)),
            pl.BlockSpec((1, ADA_TN), lambda j: (0, j)),
        ],
        out_specs=pl.BlockSpec((1, ADA_TN), lambda j: (0, j)),
        out_shape=jax.ShapeDtypeStruct((1, n), F32),
        compiler_params=_cparams(("arbitrary",)),
        name="ada_mod",
    )(c_col, w_ada, b_ada)


INPROJ_TM = 1024
INPROJ_TN = 1024


def _inproj_kernel(x_ref, g_ref, sc_ref, sh_ref, wb_ref, ws_ref, o_ref, os_ref, h_ref):
    @pl.when(pl.program_id(1) == 0)
    def _():
        h = _rms(x_ref[...], g_ref[...]) * (1.0 + sc_ref[...]) + sh_ref[...]
        hb = h.astype(BF16)
        h_ref[...] = hb
        os_ref[...] = _dot(hb, ws_ref[...])

    o_ref[...] = _dot(h_ref[...], wb_ref[...]).astype(BF16)


def _inproj(x2d, g, sc, sh, w_big, w_small):
    vec = pl.BlockSpec((1, D_MODEL), lambda i, j: (0, 0))
    return pl.pallas_call(
        _inproj_kernel,
        grid=(SEQ // INPROJ_TM, PROJ_COLS // INPROJ_TN),
        in_specs=[
            pl.BlockSpec((INPROJ_TM, D_MODEL), lambda i, j: (i, 0)),
            vec, vec, vec,
            pl.BlockSpec((D_MODEL, INPROJ_TN), lambda i, j: (0, j)),
            pl.BlockSpec((D_MODEL, SMALL_COLS), lambda i, j: (0, 0)),
        ],
        out_specs=[
            pl.BlockSpec((INPROJ_TM, INPROJ_TN), lambda i, j: (i, j)),
            pl.BlockSpec((INPROJ_TM, SMALL_COLS), lambda i, j: (i, 0)),
        ],
        out_shape=[
            jax.ShapeDtypeStruct((SEQ, PROJ_COLS), BF16),
            jax.ShapeDtypeStruct((SEQ, SMALL_COLS), F32),
        ],
        scratch_shapes=[pltpu.VMEM((INPROJ_TM, D_MODEL), BF16)],
        compiler_params=_cparams(("arbitrary", "arbitrary")),
        name="inproj",
    )(x2d, g, sc, sh, w_big, w_small)


def _causal_conv_silu(cur, prev, w_ref, b_ref):
    row = lax.broadcasted_iota(I32, cur.shape, 0)
    acc = cur * w_ref[3:4, :] + b_ref[...]
    for s in (1, 2, 3):
        shifted = jnp.where(row >= s, pltpu.roll(cur, s, 0), pltpu.roll(prev, s, 0))
        acc = acc + shifted * w_ref[3 - s:4 - s, :]
    return _silu(acc)


def _ssd_kernel(xs_ref, bc_ref, z_ref, ma_ref, sm_ref, cwx_ref, cwb_ref, cbx_ref, cbb_ref,
                dtb_ref, alog_ref, dexp_ref, ng_ref, o_ref, px_ref, pb_ref, st_ref):
    L = SSD_CHUNK

    @pl.when(pl.program_id(0) == 0)
    def _():
        px_ref[...] = jnp.zeros_like(px_ref)
        pb_ref[...] = jnp.zeros_like(pb_ref)
        st_ref[...] = jnp.zeros_like(st_ref)

    xs_raw = xs_ref[...].astype(F32)
    bc_raw = bc_ref[...].astype(F32)
    xs = _causal_conv_silu(xs_raw, px_ref[...], cwx_ref, cbx_ref)
    bc = _causal_conv_silu(bc_raw, pb_ref[...], cwb_ref, cbb_ref)
    px_ref[...] = xs_raw
    pb_ref[...] = bc_raw

    dt = _softplus(sm_ref[:, 0:SSD_HEADS] + dtb_ref[...])
    d_a = dt * (-jnp.exp(alog_ref[...]))
    ri = lax.broadcasted_iota(I32, (L, L), 0)
    ci = lax.broadcasted_iota(I32, (L, L), 1)
    causal = ri >= ci
    tril = jnp.where(causal, 1.0, 0.0).astype(BF16)
    acum = _sel_dot(tril, d_a)
    hi32 = lax.broadcasted_iota(I32, (SSD_HEADS, SSD_HEADS), 0)
    hj32 = lax.broadcasted_iota(I32, (SSD_HEADS, SSD_HEADS), 1)
    eye32 = jnp.where(hi32 == hj32, 1.0, 0.0).astype(BF16)
    a_hi, a_mid, a_lo = _split3(acum)
    acum_t = _dot_nt(eye32, a_hi) + _dot_nt(eye32, a_mid) + _dot_nt(eye32, a_lo)
    last = acum[L - 1:L, :]
    e_a = jnp.exp(acum)
    to_end = jnp.exp(last - acum)
    eh = lax.broadcasted_iota(I32, (SSD_HEADS, SSD_INNER), 0)
    ec = lax.broadcasted_iota(I32, (SSD_HEADS, SSD_INNER), 1)
    expand = jnp.where((ec >> 6) == eh, 1.0, 0.0).astype(BF16)
    dt_x = _dot_sel(dt, expand)
    dte_x = _dot_sel(dt * to_end, expand)
    ea_x = _dot_sel(e_a, expand)
    cd_x = ea_x[L - 1:L, :]

    xdt = xs * dt_x
    lane = lax.broadcasted_iota(I32, xdt.shape, 1)
    left = (lane & 64) == 0
    xdt_l = jnp.where(left, xdt, 0.0).astype(BF16)
    xdt_r = jnp.where(left, 0.0, xdt).astype(BF16)
    xdte = (xs * dte_x).astype(BF16)

    y_diag = []
    y_off = []
    for g in range(SSD_GROUPS):
        bg = bc[:, g * SSD_STATE:(g + 1) * SSD_STATE].astype(BF16)
        cg = bc[:, 512 + g * SSD_STATE:512 + (g + 1) * SSD_STATE].astype(BF16)
        cb = _dot_nt(cg, bg)
        gsl = slice(g * GROUP_COLS, (g + 1) * GROUP_COLS)
        st = st_ref[g]
        y_off.append(_dot(cg, st.astype(BF16)))
        st_ref[g] = st * cd_x[:, gsl] + _dot_tn(bg, xdte[:, gsl])
        for p in range(HEADS_PER_GROUP // 2):
            ms = []
            for h in (g * HEADS_PER_GROUP + 2 * p, g * HEADS_PER_GROUP + 2 * p + 1):
                seg = acum[:, h:h + 1] - acum_t[h:h + 1, :]
                decay = jnp.exp(jnp.where(causal, seg, -jnp.inf))
                ms.append((cb * decay).astype(BF16))
            m_cat = jnp.concatenate(ms, axis=1)
            psl = slice((g * 4 + p) * 128, (g * 4 + p + 1) * 128)
            x_cat = jnp.concatenate([xdt_l[:, psl], xdt_r[:, psl]], axis=0)
            y_diag.append(_dot(m_cat, x_cat))

    y = (jnp.concatenate(y_diag, axis=1) + jnp.concatenate(y_off, axis=1) * ea_x
         + dexp_ref[...] * xs)
    yf = y * _silu(z_ref[...].astype(F32))
    outs = []
    for g in range(SSD_GROUPS):
        seg = yf[:, g * GROUP_COLS:(g + 1) * GROUP_COLS]
        ms = jnp.mean(seg * seg, axis=-1, keepdims=True)
        outs.append(seg * lax.rsqrt(ms + NORM_EPS))
    y_ssd = jnp.concatenate(outs, axis=1) * ng_ref[...]
    o_ref[...] = (jax.nn.sigmoid(ma_ref[...].astype(F32)) * y_ssd).astype(BF16)


def _ssd(proj, small, conv_w, conv_b, dt_bias, a_log, d_exp, norm_g):
    L = SSD_CHUNK
    full = lambda shape: pl.BlockSpec(shape, lambda i: (0, 0))
    return pl.pallas_call(
        _ssd_kernel,
        grid=(SEQ // L,),
        in_specs=[
            pl.BlockSpec((L, 2048), lambda i: (i, COL_XS // 2048)),
            pl.BlockSpec((L, 1024), lambda i: (i, COL_BC // 1024)),
            pl.BlockSpec((L, 2048), lambda i: (i, COL_Z // 2048)),
            pl.BlockSpec((L, 2048), lambda i: (i, COL_MA // 2048)),
            pl.BlockSpec((L, SMALL_COLS), lambda i: (i, 0)),
            pl.BlockSpec((4, 2048), lambda i: (0, 0)),
            pl.BlockSpec((4, 1024), lambda i: (0, 2)),
            pl.BlockSpec((1, 2048), lambda i: (0, 0)),
            pl.BlockSpec((1, 1024), lambda i: (0, 2)),
            full((1, SSD_HEADS)), full((1, SSD_HEADS)),
            full((1, SSD_INNER)), full((1, SSD_INNER)),
        ],
        out_specs=pl.BlockSpec((L, SSD_INNER), lambda i: (i, 0)),
        out_shape=jax.ShapeDtypeStruct((SEQ, SSD_INNER), BF16),
        scratch_shapes=[
            pltpu.VMEM((L, 2048), F32),
            pltpu.VMEM((L, 1024), F32),
            pltpu.VMEM((SSD_GROUPS, SSD_STATE, GROUP_COLS), F32),
        ],
        compiler_params=_cparams(("arbitrary",)),
        name="ssd",
    )(proj, proj, proj, proj, small, conv_w, conv_w, conv_b, conv_b, dt_bias, a_log, d_exp, norm_g)


def _gla_kernel(q_ref, k_ref, v_ref, r_ref, mb_ref, sm_ref, wg_ref, bg_ref, ng_ref, o_ref, s_ref):
    L = GLA_CHUNK

    @pl.when(pl.program_id(0) == 0)
    def _():
        s_ref[...] = jnp.zeros_like(s_ref)

    glr = sm_ref[:, SSD_HEADS:SSD_HEADS + GLA_GATE_RANK]
    pre = _dot3(glr, wg_ref[...]) + bg_ref[...]
    gk = (jnp.minimum(pre, 0.0) - jnp.log1p(jnp.exp(-jnp.abs(pre)))) / GLA_GATE_NORM
    ri = lax.broadcasted_iota(I32, (L, L), 0)
    ci = lax.broadcasted_iota(I32, (L, L), 1)
    causal = ri >= ci
    tril = jnp.where(causal, 1.0, 0.0).astype(BF16)
    bcum = _sel_dot(tril, gk)
    bmid = bcum[L // 2:L // 2 + 1, :]
    blast = bcum[L - 1:L, :]
    q = q_ref[...].astype(F32) * (GLA_HEAD_K ** -0.5)
    k = k_ref[...].astype(F32)
    q_rel = (q * jnp.exp(bcum - bmid)).astype(BF16)
    k_rel = (k * jnp.exp(bmid - bcum)).astype(BF16)
    q_int = (q * jnp.exp(bcum)).astype(BF16)
    k_end = (k * jnp.exp(blast - bcum)).astype(BF16)
    dec = jnp.exp(blast)
    v = v_ref[...]

    outs = []
    for h in range(GLA_HEADS):
        ks = slice(h * GLA_HEAD_K, (h + 1) * GLA_HEAD_K)
        vs = slice(h * GLA_HEAD_V, (h + 1) * GLA_HEAD_V)
        att = jnp.where(causal, _dot_nt(q_rel[:, ks], k_rel[:, ks]), 0.0)
        s_t = s_ref[h]
        o_h = _dot(att.astype(BF16), v[:, vs]) + _dot_nt(q_int[:, ks], s_t.astype(BF16))
        s_ref[h] = s_t * dec[:, ks] + _dot_tn(v[:, vs], k_end[:, ks])
        outs.append(_rms(o_h, ng_ref[...]))
    o = jnp.concatenate(outs, axis=1)
    o = o * _silu(r_ref[...].astype(F32)) * jax.nn.sigmoid(mb_ref[...].astype(F32))
    o_ref[...] = o.astype(BF16)


def _gla(proj, small, w_gate, b_gate, norm_g):
    L = GLA_CHUNK
    full = lambda shape: pl.BlockSpec(shape, lambda i: (0, 0))
    return pl.pallas_call(
        _gla_kernel,
        grid=(SEQ // L,),
        in_specs=[
            pl.BlockSpec((L, 1024), lambda i: (i, COL_Q // 1024)),
            pl.BlockSpec((L, 1024), lambda i: (i, COL_K // 1024)),
            pl.BlockSpec((L, 2048), lambda i: (i, COL_V // 2048)),
            pl.BlockSpec((L, 2048), lambda i: (i, COL_R // 2048)),
            pl.BlockSpec((L, 2048), lambda i: (i, COL_MB // 2048)),
            pl.BlockSpec((L, SMALL_COLS), lambda i: (i, 0)),
            full((GLA_GATE_RANK, GLA_KEY_DIM)), full((1, GLA_KEY_DIM)), full((1, GLA_HEAD_V)),
        ],
        out_specs=pl.BlockSpec((L, GLA_VAL_DIM), lambda i: (i, 0)),
        out_shape=jax.ShapeDtypeStruct((SEQ, GLA_VAL_DIM), BF16),
        scratch_shapes=[pltpu.VMEM((GLA_HEADS, GLA_HEAD_V, GLA_HEAD_K), F32)],
        compiler_params=_cparams(("arbitrary",)),
        name="gla",
    )(proj, proj, proj, proj, proj, small, w_gate, b_gate, norm_g)


OUTPROJ_TM = 256


def _outproj_kernel(a_ref, b_ref, x_ref, w_ref, g1_ref, n2_ref, sc_ref, sh_ref, wr_ref,
                    x1_ref, h2_ref, lg_ref):
    mixed = (a_ref[...].astype(F32) + b_ref[...].astype(F32)).astype(BF16)
    x1 = x_ref[...] + g1_ref[...] * _dot(mixed, w_ref[...])
    x1_ref[...] = x1
    h2 = _rms(x1, n2_ref[...]) * (1.0 + sc_ref[...]) + sh_ref[...]
    h2_ref[...] = h2
    lg_ref[...] = _dot3_nt(wr_ref[...], h2)


def _outproj(ya, yb, x2d, w_out_bf, g1, n2g, sc2, sh2, w_router_t):
    tm = OUTPROJ_TM
    vec = pl.BlockSpec((1, D_MODEL), lambda i: (0, 0))
    row = pl.BlockSpec((tm, D_MODEL), lambda i: (i, 0))
    return pl.pallas_call(
        _outproj_kernel,
        grid=(SEQ // tm,),
        in_specs=[row, row, row,
                  pl.BlockSpec((D_MODEL, D_MODEL), lambda i: (0, 0)),
                  vec, vec, vec, vec,
                  pl.BlockSpec((N_EXPERTS, D_MODEL), lambda i: (0, 0))],
        out_specs=[row, row, pl.BlockSpec((N_EXPERTS, tm), lambda i: (0, i))],
        out_shape=[jax.ShapeDtypeStruct((SEQ, D_MODEL), F32),
                   jax.ShapeDtypeStruct((SEQ, D_MODEL), F32),
                   jax.ShapeDtypeStruct((N_EXPERTS, SEQ), F32)],
        compiler_params=_cparams(("arbitrary",)),
        name="outproj_router",
    )(ya, yb, x2d, w_out_bf, g1, n2g, sc2, sh2, w_router_t)


ROUTE_TN = 512


def _route_kernel(lg_ref, bias_ref, ek_ref, wk_ref, rk_ref, cnt_ref, carry_ref):
    tn = ROUTE_TN
    gsz = N_EXPERTS // N_EXPERT_GROUPS

    @pl.when(pl.program_id(0) == 0)
    def _():
        carry_ref[...] = jnp.zeros_like(carry_ref)

    s = jax.nn.sigmoid(lg_ref[...])
    choice = s + bias_ref[...]
    r8 = lax.broadcasted_iota(I32, (gsz, tn), 0)
    neg = -jnp.inf

    def top1(cur, rows, nrows):
        m = jnp.max(cur, axis=0, keepdims=True)
        idx = jnp.min(jnp.where(cur == m, rows, nrows), axis=0, keepdims=True)
        return m, idx

    gscores = []
    for g in range(N_EXPERT_GROUPS):
        blk = choice[g * gsz:(g + 1) * gsz, :]
        m1, i1 = top1(blk, r8, gsz)
        m2, _ = top1(jnp.where(r8 == i1, neg, blk), r8, gsz)
        gscores.append(m1 + m2)
    cur = jnp.concatenate(gscores, axis=0)
    gsel = jnp.zeros((N_EXPERT_GROUPS, tn), F32)
    for _ in range(TOPK_GROUPS):
        _, idx = top1(cur, r8, N_EXPERT_GROUPS)
        hit = r8 == idx
        gsel = jnp.where(hit, 1.0, gsel)
        cur = jnp.where(hit, neg, cur)
    emask = jnp.concatenate(
        [jnp.broadcast_to(gsel[g:g + 1, :], (gsz, tn)) for g in range(N_EXPERT_GROUPS)], axis=0)
    rows = lax.broadcasted_iota(I32, (N_EXPERTS, tn), 0)
    cur = jnp.where(emask > 0.5, choice, neg)
    sel = jnp.zeros((N_EXPERTS, tn), F32)
    hits, eks, wks = [], [], []
    for _ in range(TOP_K):
        _, idx = top1(cur, rows, N_EXPERTS)
        hit = rows == idx
        hits.append(hit)
        eks.append(idx)
        wks.append(jnp.sum(jnp.where(hit, s, 0.0), axis=0, keepdims=True))
        cur = jnp.where(hit, neg, cur)
        sel = jnp.where(hit, 1.0, sel)
    wsum = wks[0]
    for w in wks[1:]:
        wsum = wsum + w
    scale = ROUTED_SCALE / (wsum + 1e-20)

    ti = lax.broadcasted_iota(I32, (tn, tn), 0)
    tj = lax.broadcasted_iota(I32, (tn, tn), 1)
    before = jnp.where(ti < tj, 1.0, 0.0).astype(BF16)
    rank = _dot(sel.astype(BF16), before) + carry_ref[...]
    carry_ref[...] = carry_ref[...] + jnp.sum(sel, axis=1, keepdims=True)
    rks = [jnp.sum(jnp.where(hit, rank, 0.0), axis=0, keepdims=True) for hit in hits]

    ek_ref[...] = jnp.concatenate(eks, axis=0)
    wk_ref[...] = jnp.concatenate(wks, axis=0) * scale
    rk_ref[...] = jnp.concatenate(rks, axis=0).astype(I32)
    cnt_ref[...] = carry_ref[...].astype(I32)


def _route(logits_t, bias_col):
    tn = ROUTE_TN
    kt = pl.BlockSpec((TOP_K, tn), lambda i: (0, i))
    return pl.pallas_call(
        _route_kernel,
        grid=(SEQ // tn,),
        in_specs=[pl.BlockSpec((N_EXPERTS, tn), lambda i: (0, i)),
                  pl.BlockSpec((N_EXPERTS, 1), lambda i: (0, 0))],
        out_specs=[kt, kt, kt, pl.BlockSpec((N_EXPERTS, 1), lambda i: (0, 0))],
        out_shape=[jax.ShapeDtypeStruct((TOP_K, SEQ), I32),
                   jax.ShapeDtypeStruct((TOP_K, SEQ), F32),
                   jax.ShapeDtypeStruct((TOP_K, SEQ), I32),
                   jax.ShapeDtypeStruct((N_EXPERTS, 1), I32)],
        scratch_shapes=[pltpu.VMEM((N_EXPERTS, 1), F32)],
        compiler_params=_cparams(("arbitrary",)),
        name="route",
    )(logits_t, bias_col)


def _dest_kernel(ek_ref, rk_ref, ps_ref, o_ref):
    tn = ROUTE_TN
    rows = lax.broadcasted_iota(I32, (N_EXPERTS, tn), 0)
    outs = []
    for k in range(TOP_K):
        hit = rows == ek_ref[k:k + 1, :]
        outs.append(jnp.sum(jnp.where(hit, ps_ref[...], 0), axis=0, keepdims=True))
    o_ref[...] = jnp.concatenate(outs, axis=0) + rk_ref[...]


def _dest(ek, rk, pstarts_col):
    tn = ROUTE_TN
    kt = pl.BlockSpec((TOP_K, tn), lambda i: (0, i))
    return pl.pallas_call(
        _dest_kernel,
        grid=(SEQ // tn,),
        in_specs=[kt, kt, pl.BlockSpec((N_EXPERTS, 1), lambda i: (0, 0))],
        out_specs=kt,
        out_shape=jax.ShapeDtypeStruct((TOP_K, SEQ), I32),
        compiler_params=_cparams(("arbitrary",)),
        name="dest_rows",
    )(ek, rk, pstarts_col)


DISPATCH_TT = 256
SUBLANES = 8
PAD_BITS = tuple(1 << b for b in range(MOE_BM.bit_length() - 2, 2, -1))


def _dispatch_kernel(dest_ref, padoff_ref, padlen_ref, h_ref, out_ref, zeros_ref, sem, zsem):
    tt = DISPATCH_TT
    i = pl.program_id(0)
    base = i * tt

    def issue(t, carry):
        for k in range(TOP_K):
            d = dest_ref[k * SEQ + base + t]
            pltpu.make_async_copy(h_ref.at[pl.ds(t, 1), :], out_ref.at[pl.ds(d, 1), :], sem).start()
        return carry

    lax.fori_loop(0, tt, issue, 0)

    def pad_copy(off, n):
        return pltpu.make_async_copy(zeros_ref.at[pl.ds(0, n), :], out_ref.at[pl.ds(off, n), :], zsem)

    def pad_rows(wait):
        def body(e, carry):
            start = padoff_ref[e]
            head = (-start) & (SUBLANES - 1)
            for j in range(SUBLANES - 1):
                @pl.when(j < head)
                def _():
                    cp = pad_copy(start + j, 1)
                    cp.wait() if wait else cp.start()
            off = start + head
            n = padlen_ref[e] - head
            for bit in PAD_BITS:
                @pl.when((n & bit) != 0)
                def _():
                    cp = pad_copy(pl.multiple_of(off, SUBLANES), bit)
                    cp.wait() if wait else cp.start()
                off = off + (n & bit)
            return carry
        lax.fori_loop(0, N_EXPERTS, body, 0)

    @pl.when(i == 0)
    def _():
        zeros_ref[...] = jnp.zeros_like(zeros_ref)
        pad_rows(wait=False)
        pad_rows(wait=True)

    for _ in range(TOP_K):
        pltpu.make_async_copy(h_ref, out_ref.at[pl.ds(0, tt), :], sem).wait()


def _dispatch(dest_flat, pad_off, pad_len, h2):
    tt = DISPATCH_TT
    grid_spec = pltpu.PrefetchScalarGridSpec(
        num_scalar_prefetch=3,
        grid=(SEQ // tt,),
        in_specs=[pl.BlockSpec((tt, D_MODEL), lambda i, d, po, pn: (i, 0))],
        out_specs=pl.BlockSpec(memory_space=pl.ANY),
        scratch_shapes=[
            pltpu.VMEM((MOE_BM // 2, D_MODEL), F32),
            pltpu.SemaphoreType.DMA,
            pltpu.SemaphoreType.DMA,
        ],
    )
    return pl.pallas_call(
        _dispatch_kernel,
        grid_spec=grid_spec,
        out_shape=jax.ShapeDtypeStruct((MOE_ROWS, D_MODEL), F32),
        compiler_params=_cparams(("arbitrary",)),
        name="dispatch",
    )(dest_flat, pad_off, pad_len, h2)


def _experts_kernel(bexp_ref, nused_ref, x_ref, wg_ref, wu_ref, wd_ref, o_ref,
                    wgb_ref, wub_ref, wdb_ref):
    b = pl.program_id(0)
    prev = bexp_ref[jnp.maximum(b - 1, 0)]
    fresh = jnp.logical_or(b == 0, bexp_ref[b] != prev)

    @pl.when(jnp.logical_and(fresh, b < nused_ref[0]))
    def _():
        wgb_ref[...] = wg_ref[...].astype(BF16)
        wub_ref[...] = wu_ref[...].astype(BF16)
        wdb_ref[...] = wd_ref[...].astype(BF16)

    @pl.when(b < nused_ref[0])
    def _():
        x = x_ref[...].astype(BF16)
        gate = _dot(x, wgb_ref[...])
        up = _dot(x, wub_ref[...])
        hid = (_silu(gate) * up).astype(BF16)
        o_ref[...] = _dot(hid, wdb_ref[...])


def _experts(block_exp, nused, xs_perm, w_gate, w_up, w_down):
    bm = MOE_BM

    def xmap(b, be, nu):
        return (jnp.minimum(b, nu[0] - 1), 0)

    def wmap(b, be, nu):
        return (be[b], 0, 0)

    grid_spec = pltpu.PrefetchScalarGridSpec(
        num_scalar_prefetch=2,
        grid=(MOE_NB,),
        in_specs=[
            pl.BlockSpec((bm, D_MODEL), xmap),
            pl.BlockSpec((None, D_MODEL, EXPERT_DIM), wmap),
            pl.BlockSpec((None, D_MODEL, EXPERT_DIM), wmap),
            pl.BlockSpec((None, EXPERT_DIM, D_MODEL), wmap),
        ],
        out_specs=pl.BlockSpec((bm, D_MODEL), xmap),
        scratch_shapes=[
            pltpu.VMEM((D_MODEL, EXPERT_DIM), BF16),
            pltpu.VMEM((D_MODEL, EXPERT_DIM), BF16),
            pltpu.VMEM((EXPERT_DIM, D_MODEL), BF16),
        ],
    )
    return pl.pallas_call(
        _experts_kernel,
        grid_spec=grid_spec,
        out_shape=jax.ShapeDtypeStruct((MOE_ROWS, D_MODEL), F32),
        compiler_params=_cparams(("arbitrary",)),
        name="experts",
    )(block_exp, nused, xs_perm, w_gate, w_up, w_down)


FINAL_TM = 128


def _final_kernel(dest_ref, h2_ref, x1_ref, wk_ref, wg_ref, wu_ref, wd_ref, g2_ref, nf_ref, eo_ref,
                  o_ref, buf_ref, sem):
    tm = FINAL_TM
    i = pl.program_id(0)
    slot = i % 2

    def issue(tile, slot_):
        base = tile * tm

        def body(t, carry):
            for k in range(TOP_K):
                d = dest_ref[k * SEQ + base + t]
                pltpu.make_async_copy(eo_ref.at[pl.ds(d, 1), :],
                                      buf_ref.at[slot_, k, pl.ds(t, 1), :], sem.at[slot_]).start()
            return carry

        lax.fori_loop(0, tm, body, 0)

    @pl.when(i == 0)
    def _():
        issue(0, 0)

    @pl.when(i + 1 < pl.num_programs(0))
    def _():
        issue(i + 1, 1 - slot)

    h = h2_ref[...].astype(BF16)
    hid = (_silu(_dot(h, wg_ref[...])) * _dot(h, wu_ref[...])).astype(BF16)
    y = _dot(hid, wd_ref[...])

    for k in range(TOP_K):
        pltpu.make_async_copy(eo_ref.at[pl.ds(0, tm), :], buf_ref.at[slot, k], sem.at[slot]).wait()
    wk = wk_ref[...]
    for k in range(TOP_K):
        y = y + wk[:, k:k + 1] * buf_ref[slot, k]
    x2 = x1_ref[...] + g2_ref[...] * y
    o_ref[...] = _rms(x2, nf_ref[...])


def _final(dest_flat, h2, x1, wk_t, wsg, wsu, wsd, g2, nf, eo):
    tm = FINAL_TM
    vec = pl.BlockSpec((1, D_MODEL), lambda i, d: (0, 0))
    row = pl.BlockSpec((tm, D_MODEL), lambda i, d: (i, 0))
    grid_spec = pltpu.PrefetchScalarGridSpec(
        num_scalar_prefetch=1,
        grid=(SEQ // tm,),
        in_specs=[row, row,
                  pl.BlockSpec((tm, TOP_K), lambda i, d: (i, 0)),
                  pl.BlockSpec((D_MODEL, EXPERT_DIM), lambda i, d: (0, 0)),
                  pl.BlockSpec((D_MODEL, EXPERT_DIM), lambda i, d: (0, 0)),
                  pl.BlockSpec((EXPERT_DIM, D_MODEL), lambda i, d: (0, 0)),
                  vec, vec,
                  pl.BlockSpec(memory_space=pl.ANY)],
        out_specs=row,
        scratch_shapes=[
            pltpu.VMEM((2, TOP_K, tm, D_MODEL), F32),
            pltpu.SemaphoreType.DMA((2,)),
        ],
    )
    return pl.pallas_call(
        _final_kernel,
        grid_spec=grid_spec,
        out_shape=jax.ShapeDtypeStruct((SEQ, D_MODEL), F32),
        compiler_params=_cparams(("arbitrary",)),
        name="combine_final",
    )(dest_flat, h2, x1, wk_t, wsg, wsu, wsd, g2, nf, eo)


def kernel(x, c, w_ada, b_ada, norm1_g, w_in, conv_w, conv_b, dt_bias, a_log, d_skip, ssd_norm_g, gla_w_gate, gla_b_gate, gla_norm_g, w_out, norm2_g, w_router, router_bias, w_e_gate, w_e_up, w_e_down, w_s_gate, w_s_up, w_s_down, normf_g):
    layer = 0
    x2d = x.reshape(SEQ, D_MODEL)
    mod = _ada(c.reshape(D_MODEL, 1), w_ada[layer], b_ada[layer].reshape(1, -1))
    sh1, sc1, g1, sh2, sc2, g2 = [mod[:, i * D_MODEL:(i + 1) * D_MODEL] for i in range(6)]

    wi = w_in[layer]
    o_z, o_xbc, o_dt, o_q, o_k, o_v, o_r, o_g, o_ma, o_mb, o_end = (
        0, 2048, 5120, 5152, 6176, 7200, 9248, 11296, 11312, 13360, 15408)
    w_big = jnp.concatenate([
        wi[:, o_xbc:o_xbc + 2048], wi[:, o_z:o_xbc], wi[:, o_v:o_r], wi[:, o_r:o_g],
        wi[:, o_ma:o_mb], wi[:, o_mb:o_end], wi[:, o_xbc + 2048:o_dt], wi[:, o_q:o_k],
        wi[:, o_k:o_v]], axis=1).astype(BF16)
    w_small = jnp.concatenate([
        wi[:, o_dt:o_q], wi[:, o_g:o_ma],
        jnp.zeros((D_MODEL, SMALL_COLS - SSD_HEADS - GLA_GATE_RANK), F32)], axis=1).astype(BF16)

    proj, small = _inproj(x2d, norm1_g[layer].reshape(1, -1), sc1, sh1, w_big, w_small)

    d_exp = jnp.repeat(d_skip[layer], SSD_HEAD_DIM).reshape(1, -1)
    y_a = _ssd(proj, small, conv_w[layer], conv_b[layer].reshape(1, -1),
               dt_bias[layer].reshape(1, -1), a_log[layer].reshape(1, -1), d_exp,
               ssd_norm_g[layer].reshape(1, -1))
    y_b = _gla(proj, small, gla_w_gate[layer], gla_b_gate[layer].reshape(1, -1),
               gla_norm_g[layer].reshape(1, -1))

    x1, h2, logits_t = _outproj(y_a, y_b, x2d, w_out[layer].astype(BF16), g1,
                                norm2_g[layer].reshape(1, -1), sc2, sh2, w_router[layer].T)

    ek, wk, rk, counts = _route(logits_t, router_bias[layer].reshape(-1, 1))

    counts = counts.reshape(-1)
    padded = (counts + MOE_BM - 1) // MOE_BM * MOE_BM
    pends = jnp.cumsum(padded)
    pstarts = pends - padded
    block_start = jnp.arange(MOE_NB, dtype=I32) * MOE_BM
    block_exp = jnp.minimum(
        jnp.sum((pends[None, :] <= block_start[:, None]).astype(I32), axis=1), N_EXPERTS - 1)
    nused = (pends[-1:] // MOE_BM).astype(I32)

    dest_flat = _dest(ek, rk, pstarts.reshape(-1, 1).astype(I32)).reshape(-1)
    xs_perm = _dispatch(dest_flat, (pstarts + counts).astype(I32), (padded - counts).astype(I32), h2)
    eo = _experts(block_exp, nused, xs_perm, w_e_gate[layer], w_e_up[layer], w_e_down[layer])
    out = _final(dest_flat, h2, x1, wk.T, w_s_gate[layer].astype(BF16), w_s_up[layer].astype(BF16),
                 w_s_down[layer].astype(BF16), g2, normf_g.reshape(1, -1), eo)
    return out.reshape(x.shape)
```

```python
import functools

import jax
import jax.numpy as jnp
from jax import lax
from jax.experimental import pallas as pl
from jax.experimental.pallas import tpu as pltpu

F32 = jnp.float32
BF16 = jnp.bfloat16
I32 = jnp.int32

D_MODEL = 2048
SEQ = 8192
NORM_EPS = 1e-6
SSD_HEADS = 32
SSD_HEAD_DIM = 64
SSD_INNER = 2048
SSD_GROUPS = 4
SSD_STATE = 128
SSD_CHUNK = 128
HEADS_PER_GROUP = SSD_HEADS // SSD_GROUPS
GROUP_COLS = HEADS_PER_GROUP * SSD_HEAD_DIM
GLA_HEADS = 4
GLA_KEY_DIM = 1024
GLA_VAL_DIM = 2048
GLA_HEAD_K = 256
GLA_HEAD_V = 512
GLA_GATE_RANK = 16
GLA_GATE_NORM = 16.0
GLA_CHUNK = 64
N_EXPERTS = 64
TOP_K = 8
N_EXPERT_GROUPS = 8
TOPK_GROUPS = 4
EXPERT_DIM = 512
ROUTED_SCALE = 2.5

VMEM_LIMIT_BYTES = 56 * 1024 * 1024

COL_Z, COL_XS, COL_BC, COL_Q, COL_V, COL_R, COL_MA, COL_MB, COL_K = (
    0, 2048, 4096, 5120, 6144, 8192, 10240, 12288, 14336)
PROJ_COLS = 15360
SMALL_COLS = 128

MOE_BM = 256
MOE_NB = SEQ * TOP_K // MOE_BM + N_EXPERTS
MOE_ROWS = MOE_NB * MOE_BM


def _cparams(sem):
    return pltpu.CompilerParams(dimension_semantics=sem, vmem_limit_bytes=VMEM_LIMIT_BYTES)


def _split3(a):
    hi = a.astype(BF16)
    r1 = a - hi.astype(F32)
    mid = r1.astype(BF16)
    lo = (r1 - mid.astype(F32)).astype(BF16)
    return hi, mid, lo


def _dot(a, b):
    return jnp.dot(a, b, preferred_element_type=F32)


def _dot_nt(a, b):
    return lax.dot_general(a, b, (((1,), (1,)), ((), ())), preferred_element_type=F32)


def _dot_tn(a, b):
    return lax.dot_general(a, b, (((0,), (0,)), ((), ())), preferred_element_type=F32)


def _sel_dot(sel_bf16, a):
    hi, mid, lo = _split3(a)
    return _dot(sel_bf16, hi) + _dot(sel_bf16, mid) + _dot(sel_bf16, lo)


def _dot_sel(a, sel_bf16):
    hi, mid, lo = _split3(a)
    return _dot(hi, sel_bf16) + _dot(mid, sel_bf16) + _dot(lo, sel_bf16)


def _dot3(a, b):
    ah = a.astype(BF16)
    al = (a - ah.astype(F32)).astype(BF16)
    bh = b.astype(BF16)
    bl = (b - bh.astype(F32)).astype(BF16)
    return _dot(ah, bh) + _dot(ah, bl) + _dot(al, bh)


def _dot3_nt(a, b):
    ah = a.astype(BF16)
    al = (a - ah.astype(F32)).astype(BF16)
    bh = b.astype(BF16)
    bl = (b - bh.astype(F32)).astype(BF16)
    return _dot_nt(ah, bh) + _dot_nt(ah, bl) + _dot_nt(al, bh)


def _silu(x):
    return x * jax.nn.sigmoid(x)


def _softplus(x):
    return jnp.maximum(x, 0.0) + jnp.log1p(jnp.exp(-jnp.abs(x)))


def _rms(x, g):
    ms = jnp.mean(x * x, axis=-1, keepdims=True)
    return x * lax.rsqrt(ms + NORM_EPS) * g


ADA_TN = 1024


def _ada_kernel(c_ref, w_ref, b_ref, o_ref):
    ca = _silu(c_ref[...])
    o_ref[...] = jnp.sum(w_ref[...] * ca, axis=0, keepdims=True) + b_ref[...]


def _ada(c_col, w_ada, b_ada):
    n = w_ada.shape[1]
    return pl.pallas_call(
        _ada_kernel,
        grid=(n // ADA_TN,),
        in_specs=[
            pl.BlockSpec((D_MODEL, 1), lambda j: (0, 0)),
            pl.BlockSpec((D_MODEL, ADA_TN), lambda j: (0, j)),
            pl.BlockSpec((1, ADA_TN), lambda j: (0, j)),
        ],
        out_specs=pl.BlockSpec((1, ADA_TN), lambda j: (0, j)),
        out_shape=jax.ShapeDtypeStruct((1, n), F32),
        compiler_params=_cparams(("arbitrary",)),
        name="ada_mod",
    )(c_col, w_ada, b_ada)


INPROJ_TM = 1024
INPROJ_TN = 1024


def _inproj_kernel(x_ref, g_ref, sc_ref, sh_ref, wb_ref, ws_ref, o_ref, os_ref, h_ref):
    @pl.when(pl.program_id(1) == 0)
    def _():
        h = _rms(x_ref[...], g_ref[...]) * (1.0 + sc_ref[...]) + sh_ref[...]
        hb = h.astype(BF16)
        h_ref[...] = hb
        os_ref[...] = _dot(hb, ws_ref[...])

    o_ref[...] = _dot(h_ref[...], wb_ref[...]).astype(BF16)


def _inproj(x2d, g, sc, sh, w_big, w_small):
    vec = pl.BlockSpec((1, D_MODEL), lambda i, j: (0, 0))
    return pl.pallas_call(
        _inproj_kernel,
        grid=(SEQ // INPROJ_TM, PROJ_COLS // INPROJ_TN),
        in_specs=[
            pl.BlockSpec((INPROJ_TM, D_MODEL), lambda i, j: (i, 0)),
            vec, vec, vec,
            pl.BlockSpec((D_MODEL, INPROJ_TN), lambda i, j: (0, j)),
            pl.BlockSpec((D_MODEL, SMALL_COLS), lambda i, j: (0, 0)),
        ],
        out_specs=[
            pl.BlockSpec((INPROJ_TM, INPROJ_TN), lambda i, j: (i, j)),
            pl.BlockSpec((INPROJ_TM, SMALL_COLS), lambda i, j: (i, 0)),
        ],
        out_shape=[
            jax.ShapeDtypeStruct((SEQ, PROJ_COLS), BF16),
            jax.ShapeDtypeStruct((SEQ, SMALL_COLS), F32),
        ],
        scratch_shapes=[pltpu.VMEM((INPROJ_TM, D_MODEL), BF16)],
        compiler_params=_cparams(("arbitrary", "arbitrary")),
        name="inproj",
    )(x2d, g, sc, sh, w_big, w_small)


def _causal_conv_silu(cur, prev, w_ref, b_ref):
    row = lax.broadcasted_iota(I32, cur.shape, 0)
    acc = cur * w_ref[3:4, :] + b_ref[...]
    for s in (1, 2, 3):
        shifted = jnp.where(row >= s, pltpu.roll(cur, s, 0), pltpu.roll(prev, s, 0))
        acc = acc + shifted * w_ref[3 - s:4 - s, :]
    return _silu(acc)


def _ssd_kernel(xs_ref, bc_ref, z_ref, ma_ref, sm_ref, cwx_ref, cwb_ref, cbx_ref, cbb_ref,
                dtb_ref, alog_ref, dexp_ref, ng_ref, o_ref, px_ref, pb_ref, st_ref):
    L = SSD_CHUNK

    @pl.when(pl.program_id(0) == 0)
    def _():
        px_ref[...] = jnp.zeros_like(px_ref)
        pb_ref[...] = jnp.zeros_like(pb_ref)
        st_ref[...] = jnp.zeros_like(st_ref)

    xs_raw = xs_ref[...].astype(F32)
    bc_raw = bc_ref[...].astype(F32)
    xs = _causal_conv_silu(xs_raw, px_ref[...], cwx_ref, cbx_ref)
    bc = _causal_conv_silu(bc_raw, pb_ref[...], cwb_ref, cbb_ref)
    px_ref[...] = xs_raw
    pb_ref[...] = bc_raw

    dt = _softplus(sm_ref[:, 0:SSD_HEADS] + dtb_ref[...])
    d_a = dt * (-jnp.exp(alog_ref[...]))
    ri = lax.broadcasted_iota(I32, (L, L), 0)
    ci = lax.broadcasted_iota(I32, (L, L), 1)
    causal = ri >= ci
    tril = jnp.where(causal, 1.0, 0.0).astype(BF16)
    acum = _sel_dot(tril, d_a)
    hi32 = lax.broadcasted_iota(I32, (SSD_HEADS, SSD_HEADS), 0)
    hj32 = lax.broadcasted_iota(I32, (SSD_HEADS, SSD_HEADS), 1)
    eye32 = jnp.where(hi32 == hj32, 1.0, 0.0).astype(BF16)
    a_hi, a_mid, a_lo = _split3(acum)
    acum_t = _dot_nt(eye32, a_hi) + _dot_nt(eye32, a_mid) + _dot_nt(eye32, a_lo)
    last = acum[L - 1:L, :]
    e_a = jnp.exp(acum)
    to_end = jnp.exp(last - acum)
    eh = lax.broadcasted_iota(I32, (SSD_HEADS, SSD_INNER), 0)
    ec = lax.broadcasted_iota(I32, (SSD_HEADS, SSD_INNER), 1)
    expand = jnp.where((ec >> 6) == eh, 1.0, 0.0).astype(BF16)
    dt_x = _dot_sel(dt, expand)
    dte_x = _dot_sel(dt * to_end, expand)
    ea_x = _dot_sel(e_a, expand)
    cd_x = ea_x[L - 1:L, :]

    xdt = xs * dt_x
    lane = lax.broadcasted_iota(I32, xdt.shape, 1)
    left = (lane & 64) == 0
    xdt_l = jnp.where(left, xdt, 0.0).astype(BF16)
    xdt_r = jnp.where(left, 0.0, xdt).astype(BF16)
    xdte = (xs * dte_x).astype(BF16)

    y_diag = []
    y_off = []
    for g in range(SSD_GROUPS):
        bg = bc[:, g * SSD_STATE:(g + 1) * SSD_STATE].astype(BF16)
        cg = bc[:, 512 + g * SSD_STATE:512 + (g + 1) * SSD_STATE].astype(BF16)
        cb = _dot_nt(cg, bg)
        gsl = slice(g * GROUP_COLS, (g + 1) * GROUP_COLS)
        st = st_ref[g]
        y_off.append(_dot(cg, st.astype(BF16)))
        st_ref[g] = st * cd_x[:, gsl] + _dot_tn(bg, xdte[:, gsl])
        for p in range(HEADS_PER_GROUP // 2):
            ms = []
            for h in (g * HEADS_PER_GROUP + 2 * p, g * HEADS_PER_GROUP + 2 * p + 1):
                seg = acum[:, h:h + 1] - acum_t[h:h + 1, :]
                decay = jnp.exp(jnp.where(causal, seg, -jnp.inf))
                ms.append((cb * decay).astype(BF16))
            m_cat = jnp.concatenate(ms, axis=1)
            psl = slice((g * 4 + p) * 128, (g * 4 + p + 1) * 128)
            x_cat = jnp.concatenate([xdt_l[:, psl], xdt_r[:, psl]], axis=0)
            y_diag.append(_dot(m_cat, x_cat))

    y = (jnp.concatenate(y_diag, axis=1) + jnp.concatenate(y_off, axis=1) * ea_x
         + dexp_ref[...] * xs)
    yf = y * _silu(z_ref[...].astype(F32))
    outs = []
    for g in range(SSD_GROUPS):
        seg = yf[:, g * GROUP_COLS:(g + 1) * GROUP_COLS]
        ms = jnp.mean(seg * seg, axis=-1, keepdims=True)
        outs.append(seg * lax.rsqrt(ms + NORM_EPS))
    y_ssd = jnp.concatenate(outs, axis=1) * ng_ref[...]
    o_ref[...] = (jax.nn.sigmoid(ma_ref[...].astype(F32)) * y_ssd).astype(BF16)


def _ssd(proj, small, conv_w, conv_b, dt_bias, a_log, d_exp, norm_g):
    L = SSD_CHUNK
    full = lambda shape: pl.BlockSpec(shape, lambda i: (0, 0))
    return pl.pallas_call(
        _ssd_kernel,
        grid=(SEQ // L,),
        in_specs=[
            pl.BlockSpec((L, 2048), lambda i: (i, COL_XS // 2048)),
            pl.BlockSpec((L, 1024), lambda i: (i, COL_BC // 1024)),
            pl.BlockSpec((L, 2048), lambda i: (i, COL_Z // 2048)),
            pl.BlockSpec((L, 2048), lambda i: (i, COL_MA // 2048)),
            pl.BlockSpec((L, SMALL_COLS), lambda i: (i, 0)),
            pl.BlockSpec((4, 2048), lambda i: (0, 0)),
            pl.BlockSpec((4, 1024), lambda i: (0, 2)),
            pl.BlockSpec((1, 2048), lambda i: (0, 0)),
            pl.BlockSpec((1, 1024), lambda i: (0, 2)),
            full((1, SSD_HEADS)), full((1, SSD_HEADS)),
            full((1, SSD_INNER)), full((1, SSD_INNER)),
        ],
        out_specs=pl.BlockSpec((L, SSD_INNER), lambda i: (i, 0)),
        out_shape=jax.ShapeDtypeStruct((SEQ, SSD_INNER), BF16),
        scratch_shapes=[
            pltpu.VMEM((L, 2048), F32),
            pltpu.VMEM((L, 1024), F32),
            pltpu.VMEM((SSD_GROUPS, SSD_STATE, GROUP_COLS), F32),
        ],
        compiler_params=_cparams(("arbitrary",)),
        name="ssd",
    )(proj, proj, proj, proj, small, conv_w, conv_w, conv_b, conv_b, dt_bias, a_log, d_exp, norm_g)


def _gla_kernel(q_ref, k_ref, v_ref, r_ref, mb_ref, sm_ref, wg_ref, bg_ref, ng_ref, o_ref, s_ref):
    L = GLA_CHUNK

    @pl.when(pl.program_id(0) == 0)
    def _():
        s_ref[...] = jnp.zeros_like(s_ref)

    glr = sm_ref[:, SSD_HEADS:SSD_HEADS + GLA_GATE_RANK]
    pre = _dot3(glr, wg_ref[...]) + bg_ref[...]
    gk = (jnp.minimum(pre, 0.0) - jnp.log1p(jnp.exp(-jnp.abs(pre)))) / GLA_GATE_NORM
    ri = lax.broadcasted_iota(I32, (L, L), 0)
    ci = lax.broadcasted_iota(I32, (L, L), 1)
    causal = ri >= ci
    tril = jnp.where(causal, 1.0, 0.0).astype(BF16)
    bcum = _sel_dot(tril, gk)
    bmid = bcum[L // 2:L // 2 + 1, :]
    blast = bcum[L - 1:L, :]
    q = q_ref[...].astype(F32) * (GLA_HEAD_K ** -0.5)
    k = k_ref[...].astype(F32)
    q_rel = (q * jnp.exp(bcum - bmid)).astype(BF16)
    k_rel = (k * jnp.exp(bmid - bcum)).astype(BF16)
    q_int = (q * jnp.exp(bcum)).astype(BF16)
    k_end = (k * jnp.exp(blast - bcum)).astype(BF16)
    dec = jnp.exp(blast)
    v = v_ref[...]

    outs = []
    for h in range(GLA_HEADS):
        ks = slice(h * GLA_HEAD_K, (h + 1) * GLA_HEAD_K)
        vs = slice(h * GLA_HEAD_V, (h + 1) * GLA_HEAD_V)
        att = jnp.where(causal, _dot_nt(q_rel[:, ks], k_rel[:, ks]), 0.0)
        s_t = s_ref[h]
        o_h = _dot(att.astype(BF16), v[:, vs]) + _dot_nt(q_int[:, ks], s_t.astype(BF16))
        s_ref[h] = s_t * dec[:, ks] + _dot_tn(v[:, vs], k_end[:, ks])
        outs.append(_rms(o_h, ng_ref[...]))
    o = jnp.concatenate(outs, axis=1)
    o = o * _silu(r_ref[...].astype(F32)) * jax.nn.sigmoid(mb_ref[...].astype(F32))
    o_ref[...] = o.astype(BF16)


def _gla(proj, small, w_gate, b_gate, norm_g):
    L = GLA_CHUNK
    full = lambda shape: pl.BlockSpec(shape, lambda i: (0, 0))
    return pl.pallas_call(
        _gla_kernel,
        grid=(SEQ // L,),
        in_specs=[
            pl.BlockSpec((L, 1024), lambda i: (i, COL_Q // 1024)),
            pl.BlockSpec((L, 1024), lambda i: (i, COL_K // 1024)),
            pl.BlockSpec((L, 2048), lambda i: (i, COL_V // 2048)),
            pl.BlockSpec((L, 2048), lambda i: (i, COL_R // 2048)),
            pl.BlockSpec((L, 2048), lambda i: (i, COL_MB // 2048)),
            pl.BlockSpec((L, SMALL_COLS), lambda i: (i, 0)),
            full((GLA_GATE_RANK, GLA_KEY_DIM)), full((1, GLA_KEY_DIM)), full((1, GLA_HEAD_V)),
        ],
        out_specs=pl.BlockSpec((L, GLA_VAL_DIM), lambda i: (i, 0)),
        out_shape=jax.ShapeDtypeStruct((SEQ, GLA_VAL_DIM), BF16),
        scratch_shapes=[pltpu.VMEM((GLA_HEADS, GLA_HEAD_V, GLA_HEAD_K), F32)],
        compiler_params=_cparams(("arbitrary",)),
        name="gla",
    )(proj, proj, proj, proj, proj, small, w_gate, b_gate, norm_g)


OUTPROJ_TM = 256


def _outproj_kernel(a_ref, b_ref, x_ref, w_ref, g1_ref, n2_ref, sc_ref, sh_ref, wr_ref,
                    x1_ref, h2_ref, lg_ref):
    mixed = (a_ref[...].astype(F32) + b_ref[...].astype(F32)).astype(BF16)
    x1 = x_ref[...] + g1_ref[...] * _dot(mixed, w_ref[...])
    x1_ref[...] = x1
    h2 = _rms(x1, n2_ref[...]) * (1.0 + sc_ref[...]) + sh_ref[...]
    h2_ref[...] = h2
    lg_ref[...] = _dot3_nt(wr_ref[...], h2)


def _outproj(ya, yb, x2d, w_out_bf, g1, n2g, sc2, sh2, w_router_t):
    tm = OUTPROJ_TM
    vec = pl.BlockSpec((1, D_MODEL), lambda i: (0, 0))
    row = pl.BlockSpec((tm, D_MODEL), lambda i: (i, 0))
    return pl.pallas_call(
        _outproj_kernel,
        grid=(SEQ // tm,),
        in_specs=[row, row, row,
                  pl.BlockSpec((D_MODEL, D_MODEL), lambda i: (0, 0)),
                  vec, vec, vec, vec,
                  pl.BlockSpec((N_EXPERTS, D_MODEL), lambda i: (0, 0))],
        out_specs=[row, row, pl.BlockSpec((N_EXPERTS, tm), lambda i: (0, i))],
        out_shape=[jax.ShapeDtypeStruct((SEQ, D_MODEL), F32),
                   jax.ShapeDtypeStruct((SEQ, D_MODEL), F32),
                   jax.ShapeDtypeStruct((N_EXPERTS, SEQ), F32)],
        compiler_params=_cparams(("arbitrary",)),
        name="outproj_router",
    )(ya, yb, x2d, w_out_bf, g1, n2g, sc2, sh2, w_router_t)


ROUTE_TN = 512


def _route_kernel(lg_ref, bias_ref, ek_ref, wk_ref, rk_ref, cnt_ref, carry_ref):
    tn = ROUTE_TN
    gsz = N_EXPERTS // N_EXPERT_GROUPS

    @pl.when(pl.program_id(0) == 0)
    def _():
        carry_ref[...] = jnp.zeros_like(carry_ref)

    s = jax.nn.sigmoid(lg_ref[...])
    choice = s + bias_ref[...]
    r8 = lax.broadcasted_iota(I32, (gsz, tn), 0)
    neg = -jnp.inf

    def top1(cur, rows, nrows):
        m = jnp.max(cur, axis=0, keepdims=True)
        idx = jnp.min(jnp.where(cur == m, rows, nrows), axis=0, keepdims=True)
        return m, idx

    gscores = []
    for g in range(N_EXPERT_GROUPS):
        blk = choice[g * gsz:(g + 1) * gsz, :]
        m1, i1 = top1(blk, r8, gsz)
        m2, _ = top1(jnp.where(r8 == i1, neg, blk), r8, gsz)
        gscores.append(m1 + m2)
    cur = jnp.concatenate(gscores, axis=0)
    gsel = jnp.zeros((N_EXPERT_GROUPS, tn), F32)
    for _ in range(TOPK_GROUPS):
        _, idx = top1(cur, r8, N_EXPERT_GROUPS)
        hit = r8 == idx
        gsel = jnp.where(hit, 1.0, gsel)
        cur = jnp.where(hit, neg, cur)
    emask = jnp.concatenate(
        [jnp.broadcast_to(gsel[g:g + 1, :], (gsz, tn)) for g in range(N_EXPERT_GROUPS)], axis=0)
    rows = lax.broadcasted_iota(I32, (N_EXPERTS, tn), 0)
    cur = jnp.where(emask > 0.5, choice, neg)
    sel = jnp.zeros((N_EXPERTS, tn), F32)
    hits, eks, wks = [], [], []
    for _ in range(TOP_K):
        _, idx = top1(cur, rows, N_EXPERTS)
        hit = rows == idx
        hits.append(hit)
        eks.append(idx)
        wks.append(jnp.sum(jnp.where(hit, s, 0.0), axis=0, keepdims=True))
        cur = jnp.where(hit, neg, cur)
        sel = jnp.where(hit, 1.0, sel)
    wsum = wks[0]
    for w in wks[1:]:
        wsum = wsum + w
    scale = ROUTED_SCALE / (wsum + 1e-20)

    ti = lax.broadcasted_iota(I32, (tn, tn), 0)
    tj = lax.broadcasted_iota(I32, (tn, tn), 1)
    before = jnp.where(ti < tj, 1.0, 0.0).astype(BF16)
    rank = _dot(sel.astype(BF16), before) + carry_ref[...]
    carry_ref[...] = carry_ref[...] + jnp.sum(sel, axis=1, keepdims=True)
    rks = [jnp.sum(jnp.where(hit, rank, 0.0), axis=0, keepdims=True) for hit in hits]

    ek_ref[...] = jnp.concatenate(eks, axis=0)
    wk_ref[...] = jnp.concatenate(wks, axis=0) * scale
    rk_ref[...] = jnp.concatenate(rks, axis=0).astype(I32)
    cnt_ref[...] = carry_ref[...].astype(I32)


def _route(logits_t, bias_col):
    tn = ROUTE_TN
    kt = pl.BlockSpec((TOP_K, tn), lambda i: (0, i))
    return pl.pallas_call(
        _route_kernel,
        grid=(SEQ // tn,),
        in_specs=[pl.BlockSpec((N_EXPERTS, tn), lambda i: (0, i)),
                  pl.BlockSpec((N_EXPERTS, 1), lambda i: (0, 0))],
        out_specs=[kt, kt, kt, pl.BlockSpec((N_EXPERTS, 1), lambda i: (0, 0))],
        out_shape=[jax.ShapeDtypeStruct((TOP_K, SEQ), I32),
                   jax.ShapeDtypeStruct((TOP_K, SEQ), F32),
                   jax.ShapeDtypeStruct((TOP_K, SEQ), I32),
                   jax.ShapeDtypeStruct((N_EXPERTS, 1), I32)],
        scratch_shapes=[pltpu.VMEM((N_EXPERTS, 1), F32)],
        compiler_params=_cparams(("arbitrary",)),
        name="route",
    )(logits_t, bias_col)


def _dest_kernel(ek_ref, rk_ref, ps_ref, o_ref):
    tn = ROUTE_TN
    rows = lax.broadcasted_iota(I32, (N_EXPERTS, tn), 0)
    outs = []
    for k in range(TOP_K):
        hit = rows == ek_ref[k:k + 1, :]
        outs.append(jnp.sum(jnp.where(hit, ps_ref[...], 0), axis=0, keepdims=True))
    o_ref[...] = jnp.concatenate(outs, axis=0) + rk_ref[...]


def _dest(ek, rk, pstarts_col):
    tn = ROUTE_TN
    kt = pl.BlockSpec((TOP_K, tn), lambda i: (0, i))
    return pl.pallas_call(
        _dest_kernel,
        grid=(SEQ // tn,),
        in_specs=[kt, kt, pl.BlockSpec((N_EXPERTS, 1), lambda i: (0, 0))],
        out_specs=kt,
        out_shape=jax.ShapeDtypeStruct((TOP_K, SEQ), I32),
        compiler_params=_cparams(("arbitrary",)),
        name="dest_rows",
    )(ek, rk, pstarts_col)


DISPATCH_TT = 256
SUBLANES = 8
PAD_BITS = tuple(1 << b for b in range(MOE_BM.bit_length() - 2, 2, -1))


def _dispatch_kernel(dest_ref, padoff_ref, padlen_ref, h_ref, out_ref, zeros_ref, sem, zsem):
    tt = DISPATCH_TT
    i = pl.program_id(0)
    base = i * tt

    def issue(t, carry):
        for k in range(TOP_K):
            d = dest_ref[k * SEQ + base + t]
            pltpu.make_async_copy(h_ref.at[pl.ds(t, 1), :], out_ref.at[pl.ds(d, 1), :], sem).start()
        return carry

    lax.fori_loop(0, tt, issue, 0)

    def pad_copy(off, n):
        return pltpu.make_async_copy(zeros_ref.at[pl.ds(0, n), :], out_ref.at[pl.ds(off, n), :], zsem)

    def pad_rows(wait):
        def body(e, carry):
            start = padoff_ref[e]
            head = (-start) & (SUBLANES - 1)
            for j in range(SUBLANES - 1):
                @pl.when(j < head)
                def _():
                    cp = pad_copy(start + j, 1)
                    cp.wait() if wait else cp.start()
            off = start + head
            n = padlen_ref[e] - head
            for bit in PAD_BITS:
                @pl.when((n & bit) != 0)
                def _():
                    cp = pad_copy(pl.multiple_of(off, SUBLANES), bit)
                    cp.wait() if wait else cp.start()
                off = off + (n & bit)
            return carry
        lax.fori_loop(0, N_EXPERTS, body, 0)

    @pl.when(i == 0)
    def _():
        zeros_ref[...] = jnp.zeros_like(zeros_ref)
        pad_rows(wait=False)
        pad_rows(wait=True)

    for _ in range(TOP_K):
        pltpu.make_async_copy(h_ref, out_ref.at[pl.ds(0, tt), :], sem).wait()


def _dispatch(dest_flat, pad_off, pad_len, h2):
    tt = DISPATCH_TT
    grid_spec = pltpu.PrefetchScalarGridSpec(
        num_scalar_prefetch=3,
        grid=(SEQ // tt,),
        in_specs=[pl.BlockSpec((tt, D_MODEL), lambda i, d, po, pn: (i, 0))],
        out_specs=pl.BlockSpec(memory_space=pl.ANY),
        scratch_shapes=[
            pltpu.VMEM((MOE_BM // 2, D_MODEL), F32),
            pltpu.SemaphoreType.DMA,
            pltpu.SemaphoreType.DMA,
        ],
    )
    return pl.pallas_call(
        _dispatch_kernel,
        grid_spec=grid_spec,
        out_shape=jax.ShapeDtypeStruct((MOE_ROWS, D_MODEL), F32),
        compiler_params=_cparams(("arbitrary",)),
        name="dispatch",
    )(dest_flat, pad_off, pad_len, h2)


def _experts_kernel(bexp_ref, first_ref, slot_ref, next_ref, nused_ref,
                    x_ref, wg_hbm, wu_hbm, wd_hbm, o_ref,
                    wgf_ref, wuf_ref, wdf_ref, wgb_ref, wub_ref, wdb_ref, sem):
    b = pl.program_id(0)
    active = b < nused_ref[0]

    def weight_copies(e, s):
        return (pltpu.make_async_copy(wg_hbm.at[e], wgf_ref.at[s], sem.at[s]),
                pltpu.make_async_copy(wu_hbm.at[e], wuf_ref.at[s], sem.at[s]),
                pltpu.make_async_copy(wd_hbm.at[e], wdf_ref.at[s], sem.at[s]))

    @pl.when(jnp.logical_and(active, b == 0))
    def _():
        for cp in weight_copies(bexp_ref[0], 0):
            cp.start()

    @pl.when(jnp.logical_and(active, first_ref[b] == 1))
    def _():
        s = slot_ref[b]
        nxt = next_ref[b]

        @pl.when(nxt < N_EXPERTS)
        def _():
            for cp in weight_copies(nxt, 1 - s):
                cp.start()

        for cp in weight_copies(bexp_ref[b], s):
            cp.wait()
        wgb_ref[...] = wgf_ref[s].astype(BF16)
        wub_ref[...] = wuf_ref[s].astype(BF16)
        wdb_ref[...] = wdf_ref[s].astype(BF16)

    @pl.when(active)
    def _():
        x = x_ref[...].astype(BF16)
        gate = _dot(x, wgb_ref[...])
        up = _dot(x, wub_ref[...])
        hid = (_silu(gate) * up).astype(BF16)
        o_ref[...] = _dot(hid, wdb_ref[...])


def _experts(block_exp, first, slot, next_exp, nused, xs_perm, w_gate, w_up, w_down):
    bm = MOE_BM

    def xmap(b, be, fi, sl, ne, nu):
        return (jnp.minimum(b, nu[0] - 1), 0)

    hbm = pl.BlockSpec(memory_space=pl.ANY)
    grid_spec = pltpu.PrefetchScalarGridSpec(
        num_scalar_prefetch=5,
        grid=(MOE_NB,),
        in_specs=[pl.BlockSpec((bm, D_MODEL), xmap), hbm, hbm, hbm],
        out_specs=pl.BlockSpec((bm, D_MODEL), xmap),
        scratch_shapes=[
            pltpu.VMEM((2, D_MODEL, EXPERT_DIM), F32),
            pltpu.VMEM((2, D_MODEL, EXPERT_DIM), F32),
            pltpu.VMEM((2, EXPERT_DIM, D_MODEL), F32),
            pltpu.VMEM((D_MODEL, EXPERT_DIM), BF16),
            pltpu.VMEM((D_MODEL, EXPERT_DIM), BF16),
            pltpu.VMEM((EXPERT_DIM, D_MODEL), BF16),
            pltpu.SemaphoreType.DMA((2,)),
        ],
    )
    return pl.pallas_call(
        _experts_kernel,
        grid_spec=grid_spec,
        out_shape=jax.ShapeDtypeStruct((MOE_ROWS, D_MODEL), F32),
        compiler_params=_cparams(("arbitrary",)),
        name="experts",
    )(block_exp, first, slot, next_exp, nused, xs_perm, w_gate, w_up, w_down)


FINAL_TM = 128
FINAL_CHUNK = 256


def _final_kernel(dest_ref, h2_ref, x1_ref, wk_ref, wg_ref, wu_ref, wd_ref, g2_ref, nf_ref, eo_ref,
                  o_ref, buf_ref, sem):
    tm = FINAL_TM
    i = pl.program_id(0)
    slot = i % 2

    def issue(tile, slot_):
        base = tile * tm
        sbase = slot_ * (TOP_K * tm)

        def body(t, carry):
            for k in range(TOP_K):
                d = dest_ref[k * SEQ + base + t]
                pltpu.make_async_copy(eo_ref.at[pl.ds(d, 1), :],
                                      buf_ref.at[pl.ds(sbase + k * tm + t, 1), :],
                                      sem.at[slot_]).start()
            return carry

        lax.fori_loop(0, tm, body, 0)

    @pl.when(i == 0)
    def _():
        issue(0, 0)

    @pl.when(i + 1 < pl.num_programs(0))
    def _():
        issue(i + 1, 1 - slot)

    h = h2_ref[...].astype(BF16)
    hid = (_silu(_dot(h, wg_ref[...])) * _dot(h, wu_ref[...])).astype(BF16)

    sbase = pl.multiple_of(slot * (TOP_K * tm), TOP_K * tm)
    for k in range(TOP_K):
        pltpu.make_async_copy(eo_ref.at[pl.ds(0, tm), :],
                              buf_ref.at[pl.ds(sbase + k * tm, tm), :], sem.at[slot]).wait()
    wk = wk_ref[...]
    ssq = jnp.zeros((tm, 1), F32)
    for c in range(0, D_MODEL, FINAL_CHUNK):
        cs = slice(c, c + FINAL_CHUNK)
        y = _dot(hid, wd_ref[:, cs])
        for k in range(TOP_K):
            y = y + wk[:, k:k + 1] * buf_ref[pl.ds(sbase + k * tm, tm), cs]
        x2 = x1_ref[:, cs] + g2_ref[:, cs] * y
        o_ref[:, cs] = x2
        ssq = ssq + jnp.sum(x2 * x2, axis=-1, keepdims=True)
    inv = lax.rsqrt(ssq * (1.0 / D_MODEL) + NORM_EPS)
    for c in range(0, D_MODEL, FINAL_CHUNK):
        cs = slice(c, c + FINAL_CHUNK)
        o_ref[:, cs] = o_ref[:, cs] * inv * nf_ref[:, cs]


def _final(dest_flat, h2, x1, wk_t, wsg, wsu, wsd, g2, nf, eo):
    tm = FINAL_TM
    vec = pl.BlockSpec((1, D_MODEL), lambda i, d: (0, 0))
    row = pl.BlockSpec((tm, D_MODEL), lambda i, d: (i, 0))
    grid_spec = pltpu.PrefetchScalarGridSpec(
        num_scalar_prefetch=1,
        grid=(SEQ // tm,),
        in_specs=[row, row,
                  pl.BlockSpec((tm, TOP_K), lambda i, d: (i, 0)),
                  pl.BlockSpec((D_MODEL, EXPERT_DIM), lambda i, d: (0, 0)),
                  pl.BlockSpec((D_MODEL, EXPERT_DIM), lambda i, d: (0, 0)),
                  pl.BlockSpec((EXPERT_DIM, D_MODEL), lambda i, d: (0, 0)),
                  vec, vec,
                  pl.BlockSpec(memory_space=pl.ANY)],
        out_specs=row,
        scratch_shapes=[
            pltpu.VMEM((2 * TOP_K * tm, D_MODEL), F32),
            pltpu.SemaphoreType.DMA((2,)),
        ],
    )
    return pl.pallas_call(
        _final_kernel,
        grid_spec=grid_spec,
        out_shape=jax.ShapeDtypeStruct((SEQ, D_MODEL), F32),
        compiler_params=_cparams(("arbitrary",)),
        name="combine_final",
    )(dest_flat, h2, x1, wk_t, wsg, wsu, wsd, g2, nf, eo)


def kernel(x, c, w_ada, b_ada, norm1_g, w_in, conv_w, conv_b, dt_bias, a_log, d_skip, ssd_norm_g, gla_w_gate, gla_b_gate, gla_norm_g, w_out, norm2_g, w_router, router_bias, w_e_gate, w_e_up, w_e_down, w_s_gate, w_s_up, w_s_down, normf_g):
    layer = 0
    x2d = x.reshape(SEQ, D_MODEL)
    mod = _ada(c.reshape(D_MODEL, 1), w_ada[layer], b_ada[layer].reshape(1, -1))
    sh1, sc1, g1, sh2, sc2, g2 = [mod[:, i * D_MODEL:(i + 1) * D_MODEL] for i in range(6)]

    wi = w_in[layer]
    o_z, o_xbc, o_dt, o_q, o_k, o_v, o_r, o_g, o_ma, o_mb, o_end = (
        0, 2048, 5120, 5152, 6176, 7200, 9248, 11296, 11312, 13360, 15408)
    wb = wi.astype(BF16)
    w_big = jnp.concatenate([
        wb[:, o_z:o_dt], wb[:, o_q:o_k], wb[:, o_v:o_g], wb[:, o_ma:o_end], wb[:, o_k:o_v]], axis=1)
    w_small = jnp.concatenate([
        wi[:, o_dt:o_q], wi[:, o_g:o_ma],
        jnp.zeros((D_MODEL, SMALL_COLS - SSD_HEADS - GLA_GATE_RANK), F32)], axis=1).astype(BF16)

    proj, small = _inproj(x2d, norm1_g[layer].reshape(1, -1), sc1, sh1, w_big, w_small)

    d_exp = jnp.repeat(d_skip[layer], SSD_HEAD_DIM).reshape(1, -1)
    y_a = _ssd(proj, small, conv_w[layer], conv_b[layer].reshape(1, -1),
               dt_bias[layer].reshape(1, -1), a_log[layer].reshape(1, -1), d_exp,
               ssd_norm_g[layer].reshape(1, -1))
    y_b = _gla(proj, small, gla_w_gate[layer], gla_b_gate[layer].reshape(1, -1),
               gla_norm_g[layer].reshape(1, -1))

    x1, h2, logits_t = _outproj(y_a, y_b, x2d, w_out[layer].astype(BF16), g1,
                                norm2_g[layer].reshape(1, -1), sc2, sh2, w_router[layer].T)

    ek, wk, rk, counts = _route(logits_t, router_bias[layer].reshape(-1, 1))

    counts = counts.reshape(-1)
    padded = (counts + MOE_BM - 1) // MOE_BM * MOE_BM
    pends = jnp.cumsum(padded)
    pstarts = pends - padded
    block_start = jnp.arange(MOE_NB, dtype=I32) * MOE_BM
    block_exp = jnp.minimum(
        jnp.sum((pends[None, :] <= block_start[:, None]).astype(I32), axis=1), N_EXPERTS - 1)
    nused = (pends[-1:] // MOE_BM).astype(I32)
    bidx = jnp.arange(MOE_NB, dtype=I32)
    prev_exp = jnp.concatenate([jnp.full((1,), -1, I32), block_exp[:-1]])
    first = jnp.logical_and(bidx < nused[0], block_exp != prev_exp)
    slot = (jnp.cumsum(first.astype(I32)) - 1) & 1
    later_first = jnp.logical_and(first[None, :], bidx[None, :] > bidx[:, None])
    next_first = jnp.min(jnp.where(later_first, bidx[None, :], MOE_NB), axis=1)
    next_exp = jnp.sum(jnp.where(bidx[None, :] == next_first[:, None], block_exp[None, :], 0), axis=1)
    next_exp = jnp.where(next_first < MOE_NB, next_exp, N_EXPERTS).astype(I32)

    dest_flat = _dest(ek, rk, pstarts.reshape(-1, 1).astype(I32)).reshape(-1)
    xs_perm = _dispatch(dest_flat, (pstarts + counts).astype(I32), (padded - counts).astype(I32), h2)
    eo = _experts(block_exp.astype(I32), first.astype(I32), slot.astype(I32), next_exp, nused, xs_perm,
                  w_e_gate[layer], w_e_up[layer], w_e_down[layer])
    out = _final(dest_flat, h2, x1, wk.T, w_s_gate[layer].astype(BF16), w_s_up[layer].astype(BF16),
                 w_s_down[layer].astype(BF16), g2, normf_g.reshape(1, -1), eo)
    return out.reshape(x.shape)
```

```python
import functools

import jax
import jax.numpy as jnp
from jax import lax
from jax.experimental import pallas as pl
from jax.experimental.pallas import tpu as pltpu

F32 = jnp.float32
BF16 = jnp.bfloat16
I32 = jnp.int32

D_MODEL = 2048
SEQ = 8192
NORM_EPS = 1e-6
SSD_HEADS = 32
SSD_HEAD_DIM = 64
SSD_INNER = 2048
SSD_GROUPS = 4
SSD_STATE = 128
SSD_CHUNK = 128
HEADS_PER_GROUP = SSD_HEADS // SSD_GROUPS
GROUP_COLS = HEADS_PER_GROUP * SSD_HEAD_DIM
GLA_HEADS = 4
GLA_KEY_DIM = 1024
GLA_VAL_DIM = 2048
GLA_HEAD_K = 256
GLA_HEAD_V = 512
GLA_GATE_RANK = 16
GLA_GATE_NORM = 16.0
GLA_CHUNK = 64
N_EXPERTS = 64
TOP_K = 8
N_EXPERT_GROUPS = 8
TOPK_GROUPS = 4
EXPERT_DIM = 512
ROUTED_SCALE = 2.5

VMEM_LIMIT_BYTES = 56 * 1024 * 1024

COL_Z, COL_XS, COL_BC, COL_Q, COL_V, COL_R, COL_MA, COL_MB, COL_K = (
    0, 2048, 4096, 5120, 6144, 8192, 10240, 12288, 14336)
PROJ_COLS = 15360
SMALL_COLS = 128

MOE_BM = 256
MOE_NB = SEQ * TOP_K // MOE_BM + N_EXPERTS
MOE_ROWS = MOE_NB * MOE_BM


def _cparams(sem):
    return pltpu.CompilerParams(dimension_semantics=sem, vmem_limit_bytes=VMEM_LIMIT_BYTES)


def _split3(a):
    hi = a.astype(BF16)
    r1 = a - hi.astype(F32)
    mid = r1.astype(BF16)
    lo = (r1 - mid.astype(F32)).astype(BF16)
    return hi, mid, lo


def _dot(a, b):
    return jnp.dot(a, b, preferred_element_type=F32)


def _dot_nt(a, b):
    return lax.dot_general(a, b, (((1,), (1,)), ((), ())), preferred_element_type=F32)


def _dot_tn(a, b):
    return lax.dot_general(a, b, (((0,), (0,)), ((), ())), preferred_element_type=F32)


def _sel_dot(sel_bf16, a):
    hi, mid, lo = _split3(a)
    return _dot(sel_bf16, hi) + _dot(sel_bf16, mid) + _dot(sel_bf16, lo)


def _dot_sel(a, sel_bf16):
    hi, mid, lo = _split3(a)
    return _dot(hi, sel_bf16) + _dot(mid, sel_bf16) + _dot(lo, sel_bf16)


def _dot3(a, b):
    ah = a.astype(BF16)
    al = (a - ah.astype(F32)).astype(BF16)
    bh = b.astype(BF16)
    bl = (b - bh.astype(F32)).astype(BF16)
    return _dot(ah, bh) + _dot(ah, bl) + _dot(al, bh)


def _dot3_nt(a, b):
    ah = a.astype(BF16)
    al = (a - ah.astype(F32)).astype(BF16)
    bh = b.astype(BF16)
    bl = (b - bh.astype(F32)).astype(BF16)
    return _dot_nt(ah, bh) + _dot_nt(ah, bl) + _dot_nt(al, bh)


def _silu(x):
    return x * jax.nn.sigmoid(x)


def _softplus(x):
    return jnp.maximum(x, 0.0) + jnp.log1p(jnp.exp(-jnp.abs(x)))


def _rms(x, g):
    ms = jnp.mean(x * x, axis=-1, keepdims=True)
    return x * lax.rsqrt(ms + NORM_EPS) * g


ADA_TN = 1024


def _ada_kernel(c_ref, w_ref, b_ref, o_ref):
    ca = _silu(c_ref[...])
    o_ref[...] = jnp.sum(w_ref[...] * ca, axis=0, keepdims=True) + b_ref[...]


def _ada(c_col, w_ada, b_ada):
    n = w_ada.shape[1]
    return pl.pallas_call(
        _ada_kernel,
        grid=(n // ADA_TN,),
        in_specs=[
            pl.BlockSpec((D_MODEL, 1), lambda j: (0, 0)),
            pl.BlockSpec((D_MODEL, ADA_TN), lambda j: (0, j)),
            pl.BlockSpec((1, ADA_TN), lambda j: (0, j)),
        ],
        out_specs=pl.BlockSpec((1, ADA_TN), lambda j: (0, j)),
        out_shape=jax.ShapeDtypeStruct((1, n), F32),
        compiler_params=_cparams(("arbitrary",)),
        name="ada_mod",
    )(c_col, w_ada, b_ada)


INPROJ_TM = 1024
INPROJ_TN = 1024


def _inproj_kernel(x_ref, g_ref, sc_ref, sh_ref, wb_ref, ws_ref, o_ref, os_ref, h_ref):
    @pl.when(pl.program_id(1) == 0)
    def _():
        h = _rms(x_ref[...], g_ref[...]) * (1.0 + sc_ref[...]) + sh_ref[...]
        hb = h.astype(BF16)
        h_ref[...] = hb
        os_ref[...] = _dot(hb, ws_ref[...])

    o_ref[...] = _dot(h_ref[...], wb_ref[...]).astype(BF16)


def _inproj(x2d, g, sc, sh, w_big, w_small):
    vec = pl.BlockSpec((1, D_MODEL), lambda i, j: (0, 0))
    return pl.pallas_call(
        _inproj_kernel,
        grid=(SEQ // INPROJ_TM, PROJ_COLS // INPROJ_TN),
        in_specs=[
            pl.BlockSpec((INPROJ_TM, D_MODEL), lambda i, j: (i, 0)),
            vec, vec, vec,
            pl.BlockSpec((D_MODEL, INPROJ_TN), lambda i, j: (0, j)),
            pl.BlockSpec((D_MODEL, SMALL_COLS), lambda i, j: (0, 0)),
        ],
        out_specs=[
            pl.BlockSpec((INPROJ_TM, INPROJ_TN), lambda i, j: (i, j)),
            pl.BlockSpec((INPROJ_TM, SMALL_COLS), lambda i, j: (i, 0)),
        ],
        out_shape=[
            jax.ShapeDtypeStruct((SEQ, PROJ_COLS), BF16),
            jax.ShapeDtypeStruct((SEQ, SMALL_COLS), F32),
        ],
        scratch_shapes=[pltpu.VMEM((INPROJ_TM, D_MODEL), BF16)],
        compiler_params=_cparams(("arbitrary", "arbitrary")),
        name="inproj",
    )(x2d, g, sc, sh, w_big, w_small)


def _causal_conv_silu(cur, prev, w_ref, b_ref):
    row = lax.broadcasted_iota(I32, cur.shape, 0)
    acc = cur * w_ref[3:4, :] + b_ref[...]
    for s in (1, 2, 3):
        shifted = jnp.where(row >= s, pltpu.roll(cur, s, 0), pltpu.roll(prev, s, 0))
        acc = acc + shifted * w_ref[3 - s:4 - s, :]
    return _silu(acc)


def _ssd_kernel(xs_ref, bc_ref, z_ref, ma_ref, sm_ref, cwx_ref, cwb_ref, cbx_ref, cbb_ref,
                dtb_ref, alog_ref, dexp_ref, ng_ref, o_ref, px_ref, pb_ref, st_ref):
    L = SSD_CHUNK

    @pl.when(pl.program_id(0) == 0)
    def _():
        px_ref[...] = jnp.zeros_like(px_ref)
        pb_ref[...] = jnp.zeros_like(pb_ref)
        st_ref[...] = jnp.zeros_like(st_ref)

    xs_raw = xs_ref[...].astype(F32)
    bc_raw = bc_ref[...].astype(F32)
    xs = _causal_conv_silu(xs_raw, px_ref[...], cwx_ref, cbx_ref)
    bc = _causal_conv_silu(bc_raw, pb_ref[...], cwb_ref, cbb_ref)
    px_ref[...] = xs_raw
    pb_ref[...] = bc_raw

    dt = _softplus(sm_ref[:, 0:SSD_HEADS] + dtb_ref[...])
    d_a = dt * (-jnp.exp(alog_ref[...]))
    ri = lax.broadcasted_iota(I32, (L, L), 0)
    ci = lax.broadcasted_iota(I32, (L, L), 1)
    causal = ri >= ci
    tril = jnp.where(causal, 1.0, 0.0).astype(BF16)
    acum = _sel_dot(tril, d_a)
    hi32 = lax.broadcasted_iota(I32, (SSD_HEADS, SSD_HEADS), 0)
    hj32 = lax.broadcasted_iota(I32, (SSD_HEADS, SSD_HEADS), 1)
    eye32 = jnp.where(hi32 == hj32, 1.0, 0.0).astype(BF16)
    a_hi, a_mid, a_lo = _split3(acum)
    acum_t = _dot_nt(eye32, a_hi) + _dot_nt(eye32, a_mid) + _dot_nt(eye32, a_lo)
    last = acum[L - 1:L, :]
    e_a = jnp.exp(acum)
    to_end = jnp.exp(last - acum)
    eh = lax.broadcasted_iota(I32, (SSD_HEADS, SSD_INNER), 0)
    ec = lax.broadcasted_iota(I32, (SSD_HEADS, SSD_INNER), 1)
    expand = jnp.where((ec >> 6) == eh, 1.0, 0.0).astype(BF16)
    dt_x = _dot_sel(dt, expand)
    dte_x = _dot_sel(dt * to_end, expand)
    ea_x = _dot_sel(e_a, expand)
    cd_x = ea_x[L - 1:L, :]

    xdt = xs * dt_x
    lane = lax.broadcasted_iota(I32, xdt.shape, 1)
    left = (lane & 64) == 0
    xdt_l = jnp.where(left, xdt, 0.0).astype(BF16)
    xdt_r = jnp.where(left, 0.0, xdt).astype(BF16)
    xdte = (xs * dte_x).astype(BF16)

    y_diag = []
    y_off = []
    for g in range(SSD_GROUPS):
        bg = bc[:, g * SSD_STATE:(g + 1) * SSD_STATE].astype(BF16)
        cg = bc[:, 512 + g * SSD_STATE:512 + (g + 1) * SSD_STATE].astype(BF16)
        cb = _dot_nt(cg, bg)
        gsl = slice(g * GROUP_COLS, (g + 1) * GROUP_COLS)
        st = st_ref[g]
        y_off.append(_dot(cg, st.astype(BF16)))
        st_ref[g] = st * cd_x[:, gsl] + _dot_tn(bg, xdte[:, gsl])
        for p in range(HEADS_PER_GROUP // 2):
            ms = []
            for h in (g * HEADS_PER_GROUP + 2 * p, g * HEADS_PER_GROUP + 2 * p + 1):
                seg = acum[:, h:h + 1] - acum_t[h:h + 1, :]
                decay = jnp.exp(jnp.where(causal, seg, -jnp.inf))
                ms.append((cb * decay).astype(BF16))
            m_cat = jnp.concatenate(ms, axis=1)
            psl = slice((g * 4 + p) * 128, (g * 4 + p + 1) * 128)
            x_cat = jnp.concatenate([xdt_l[:, psl], xdt_r[:, psl]], axis=0)
            y_diag.append(_dot(m_cat, x_cat))

    y = (jnp.concatenate(y_diag, axis=1) + jnp.concatenate(y_off, axis=1) * ea_x
         + dexp_ref[...] * xs)
    yf = y * _silu(z_ref[...].astype(F32))
    outs = []
    for g in range(SSD_GROUPS):
        seg = yf[:, g * GROUP_COLS:(g + 1) * GROUP_COLS]
        ms = jnp.mean(seg * seg, axis=-1, keepdims=True)
        outs.append(seg * lax.rsqrt(ms + NORM_EPS))
    y_ssd = jnp.concatenate(outs, axis=1) * ng_ref[...]
    o_ref[...] = (jax.nn.sigmoid(ma_ref[...].astype(F32)) * y_ssd).astype(BF16)


def _ssd(proj, small, conv_w, conv_b, dt_bias, a_log, d_exp, norm_g):
    L = SSD_CHUNK
    full = lambda shape: pl.BlockSpec(shape, lambda i: (0, 0))
    return pl.pallas_call(
        _ssd_kernel,
        grid=(SEQ // L,),
        in_specs=[
            pl.BlockSpec((L, 2048), lambda i: (i, COL_XS // 2048)),
            pl.BlockSpec((L, 1024), lambda i: (i, COL_BC // 1024)),
            pl.BlockSpec((L, 2048), lambda i: (i, COL_Z // 2048)),
            pl.BlockSpec((L, 2048), lambda i: (i, COL_MA // 2048)),
            pl.BlockSpec((L, SMALL_COLS), lambda i: (i, 0)),
            pl.BlockSpec((4, 2048), lambda i: (0, 0)),
            pl.BlockSpec((4, 1024), lambda i: (0, 2)),
            pl.BlockSpec((1, 2048), lambda i: (0, 0)),
            pl.BlockSpec((1, 1024), lambda i: (0, 2)),
            full((1, SSD_HEADS)), full((1, SSD_HEADS)),
            full((1, SSD_INNER)), full((1, SSD_INNER)),
        ],
        out_specs=pl.BlockSpec((L, SSD_INNER), lambda i: (i, 0)),
        out_shape=jax.ShapeDtypeStruct((SEQ, SSD_INNER), BF16),
        scratch_shapes=[
            pltpu.VMEM((L, 2048), F32),
            pltpu.VMEM((L, 1024), F32),
            pltpu.VMEM((SSD_GROUPS, SSD_STATE, GROUP_COLS), F32),
        ],
        compiler_params=_cparams(("arbitrary",)),
        name="ssd",
    )(proj, proj, proj, proj, small, conv_w, conv_w, conv_b, conv_b, dt_bias, a_log, d_exp, norm_g)


def _gla_kernel(q_ref, k_ref, v_ref, r_ref, mb_ref, sm_ref, wg_ref, bg_ref, ng_ref, o_ref, s_ref):
    L = GLA_CHUNK

    @pl.when(pl.program_id(0) == 0)
    def _():
        s_ref[...] = jnp.zeros_like(s_ref)

    glr = sm_ref[:, SSD_HEADS:SSD_HEADS + GLA_GATE_RANK]
    pre = _dot3(glr, wg_ref[...]) + bg_ref[...]
    gk = (jnp.minimum(pre, 0.0) - jnp.log1p(jnp.exp(-jnp.abs(pre)))) / GLA_GATE_NORM
    ri = lax.broadcasted_iota(I32, (L, L), 0)
    ci = lax.broadcasted_iota(I32, (L, L), 1)
    causal = ri >= ci
    tril = jnp.where(causal, 1.0, 0.0).astype(BF16)
    bcum = _sel_dot(tril, gk)
    bmid = bcum[L // 2:L // 2 + 1, :]
    blast = bcum[L - 1:L, :]
    q = q_ref[...].astype(F32) * (GLA_HEAD_K ** -0.5)
    k = k_ref[...].astype(F32)
    q_rel = (q * jnp.exp(bcum - bmid)).astype(BF16)
    k_rel = (k * jnp.exp(bmid - bcum)).astype(BF16)
    q_int = (q * jnp.exp(bcum)).astype(BF16)
    k_end = (k * jnp.exp(blast - bcum)).astype(BF16)
    dec = jnp.exp(blast)
    v = v_ref[...]

    outs = []
    for h in range(GLA_HEADS):
        ks = slice(h * GLA_HEAD_K, (h + 1) * GLA_HEAD_K)
        vs = slice(h * GLA_HEAD_V, (h + 1) * GLA_HEAD_V)
        att = jnp.where(causal, _dot_nt(q_rel[:, ks], k_rel[:, ks]), 0.0)
        s_t = s_ref[h]
        o_h = _dot(att.astype(BF16), v[:, vs]) + _dot_nt(q_int[:, ks], s_t.astype(BF16))
        s_ref[h] = s_t * dec[:, ks] + _dot_tn(v[:, vs], k_end[:, ks])
        outs.append(_rms(o_h, ng_ref[...]))
    o = jnp.concatenate(outs, axis=1)
    o = o * _silu(r_ref[...].astype(F32)) * jax.nn.sigmoid(mb_ref[...].astype(F32))
    o_ref[...] = o.astype(BF16)


def _gla(proj, small, w_gate, b_gate, norm_g):
    L = GLA_CHUNK
    full = lambda shape: pl.BlockSpec(shape, lambda i: (0, 0))
    return pl.pallas_call(
        _gla_kernel,
        grid=(SEQ // L,),
        in_specs=[
            pl.BlockSpec((L, 1024), lambda i: (i, COL_Q // 1024)),
            pl.BlockSpec((L, 1024), lambda i: (i, COL_K // 1024)),
            pl.BlockSpec((L, 2048), lambda i: (i, COL_V // 2048)),
            pl.BlockSpec((L, 2048), lambda i: (i, COL_R // 2048)),
            pl.BlockSpec((L, 2048), lambda i: (i, COL_MB // 2048)),
            pl.BlockSpec((L, SMALL_COLS), lambda i: (i, 0)),
            full((GLA_GATE_RANK, GLA_KEY_DIM)), full((1, GLA_KEY_DIM)), full((1, GLA_HEAD_V)),
        ],
        out_specs=pl.BlockSpec((L, GLA_VAL_DIM), lambda i: (i, 0)),
        out_shape=jax.ShapeDtypeStruct((SEQ, GLA_VAL_DIM), BF16),
        scratch_shapes=[pltpu.VMEM((GLA_HEADS, GLA_HEAD_V, GLA_HEAD_K), F32)],
        compiler_params=_cparams(("arbitrary",)),
        name="gla",
    )(proj, proj, proj, proj, proj, small, w_gate, b_gate, norm_g)


OUTPROJ_TM = 256


def _outproj_kernel(a_ref, b_ref, x_ref, w_ref, g1_ref, n2_ref, sc_ref, sh_ref, wr_ref,
                    x1_ref, h2_ref, lg_ref):
    mixed = (a_ref[...].astype(F32) + b_ref[...].astype(F32)).astype(BF16)
    x1 = x_ref[...] + g1_ref[...] * _dot(mixed, w_ref[...])
    x1_ref[...] = x1
    h2 = _rms(x1, n2_ref[...]) * (1.0 + sc_ref[...]) + sh_ref[...]
    h2_ref[...] = h2
    lg_ref[...] = _dot3_nt(wr_ref[...], h2)


def _outproj(ya, yb, x2d, w_out_bf, g1, n2g, sc2, sh2, w_router_t):
    tm = OUTPROJ_TM
    vec = pl.BlockSpec((1, D_MODEL), lambda i: (0, 0))
    row = pl.BlockSpec((tm, D_MODEL), lambda i: (i, 0))
    return pl.pallas_call(
        _outproj_kernel,
        grid=(SEQ // tm,),
        in_specs=[row, row, row,
                  pl.BlockSpec((D_MODEL, D_MODEL), lambda i: (0, 0)),
                  vec, vec, vec, vec,
                  pl.BlockSpec((N_EXPERTS, D_MODEL), lambda i: (0, 0))],
        out_specs=[row, row, pl.BlockSpec((N_EXPERTS, tm), lambda i: (0, i))],
        out_shape=[jax.ShapeDtypeStruct((SEQ, D_MODEL), F32),
                   jax.ShapeDtypeStruct((SEQ, D_MODEL), F32),
                   jax.ShapeDtypeStruct((N_EXPERTS, SEQ), F32)],
        compiler_params=_cparams(("arbitrary",)),
        name="outproj_router",
    )(ya, yb, x2d, w_out_bf, g1, n2g, sc2, sh2, w_router_t)


ROUTE_TN = 512
COMBINE_TT = 128
N_TILES = SEQ // COMBINE_TT
TILE_LANES = 128


def _route_kernel(lg_ref, bias_ref, ek_ref, wk_ref, rk_ref, cnt_ref, cb_ref, ct_ref, carry_ref):
    tn = ROUTE_TN
    gsz = N_EXPERTS // N_EXPERT_GROUPS

    @pl.when(pl.program_id(0) == 0)
    def _():
        carry_ref[...] = jnp.zeros_like(carry_ref)
        cb_ref[...] = jnp.zeros_like(cb_ref)
        ct_ref[...] = jnp.zeros_like(ct_ref)

    s = jax.nn.sigmoid(lg_ref[...])
    choice = s + bias_ref[...]
    r8 = lax.broadcasted_iota(I32, (gsz, tn), 0)
    neg = -jnp.inf

    def top1(cur, rows, nrows):
        m = jnp.max(cur, axis=0, keepdims=True)
        idx = jnp.min(jnp.where(cur == m, rows, nrows), axis=0, keepdims=True)
        return m, idx

    gscores = []
    for g in range(N_EXPERT_GROUPS):
        blk = choice[g * gsz:(g + 1) * gsz, :]
        m1, i1 = top1(blk, r8, gsz)
        m2, _ = top1(jnp.where(r8 == i1, neg, blk), r8, gsz)
        gscores.append(m1 + m2)
    cur = jnp.concatenate(gscores, axis=0)
    gsel = jnp.zeros((N_EXPERT_GROUPS, tn), F32)
    for _ in range(TOPK_GROUPS):
        _, idx = top1(cur, r8, N_EXPERT_GROUPS)
        hit = r8 == idx
        gsel = jnp.where(hit, 1.0, gsel)
        cur = jnp.where(hit, neg, cur)
    emask = jnp.concatenate(
        [jnp.broadcast_to(gsel[g:g + 1, :], (gsz, tn)) for g in range(N_EXPERT_GROUPS)], axis=0)
    rows = lax.broadcasted_iota(I32, (N_EXPERTS, tn), 0)
    cur = jnp.where(emask > 0.5, choice, neg)
    sel = jnp.zeros((N_EXPERTS, tn), F32)
    hits, eks, wks = [], [], []
    for _ in range(TOP_K):
        _, idx = top1(cur, rows, N_EXPERTS)
        hit = rows == idx
        hits.append(hit)
        eks.append(idx)
        wks.append(jnp.sum(jnp.where(hit, s, 0.0), axis=0, keepdims=True))
        cur = jnp.where(hit, neg, cur)
        sel = jnp.where(hit, 1.0, sel)
    wsum = wks[0]
    for w in wks[1:]:
        wsum = wsum + w
    scale = ROUTED_SCALE / (wsum + 1e-20)

    ti = lax.broadcasted_iota(I32, (tn, tn), 0)
    tj = lax.broadcasted_iota(I32, (tn, tn), 1)
    before = jnp.where(ti < tj, 1.0, 0.0).astype(BF16)
    rank = _dot(sel.astype(BF16), before) + carry_ref[...]
    rks = [jnp.sum(jnp.where(hit, rank, 0.0), axis=0, keepdims=True) for hit in hits]

    lane = lax.broadcasted_iota(I32, (N_EXPERTS, TILE_LANES), 1)
    carry = carry_ref[...]
    cb = cb_ref[...]
    ct = ct_ref[...]
    for sub in range(tn // COMBINE_TT):
        cnt = jnp.sum(sel[:, sub * COMBINE_TT:(sub + 1) * COMBINE_TT], axis=1, keepdims=True)
        col = pl.program_id(0) * (tn // COMBINE_TT) + sub
        cb = jnp.where(lane == col, carry, cb)
        ct = jnp.where(lane == col, cnt, ct)
        carry = carry + cnt
    cb_ref[...] = cb
    ct_ref[...] = ct
    carry_ref[...] = carry

    ek_ref[...] = jnp.concatenate(eks, axis=0)
    wk_ref[...] = jnp.concatenate(wks, axis=0) * scale
    rk_ref[...] = jnp.concatenate(rks, axis=0).astype(I32)
    cnt_ref[...] = carry_ref[...].astype(I32)


def _route(logits_t, bias_col):
    tn = ROUTE_TN
    kt = pl.BlockSpec((TOP_K, tn), lambda i: (0, i))
    return pl.pallas_call(
        _route_kernel,
        grid=(SEQ // tn,),
        in_specs=[pl.BlockSpec((N_EXPERTS, tn), lambda i: (0, i)),
                  pl.BlockSpec((N_EXPERTS, 1), lambda i: (0, 0))],
        out_specs=[kt, kt, kt, pl.BlockSpec((N_EXPERTS, 1), lambda i: (0, 0)),
                   pl.BlockSpec((N_EXPERTS, TILE_LANES), lambda i: (0, 0)),
                   pl.BlockSpec((N_EXPERTS, TILE_LANES), lambda i: (0, 0))],
        out_shape=[jax.ShapeDtypeStruct((TOP_K, SEQ), I32),
                   jax.ShapeDtypeStruct((TOP_K, SEQ), F32),
                   jax.ShapeDtypeStruct((TOP_K, SEQ), I32),
                   jax.ShapeDtypeStruct((N_EXPERTS, 1), I32),
                   jax.ShapeDtypeStruct((N_EXPERTS, TILE_LANES), F32),
                   jax.ShapeDtypeStruct((N_EXPERTS, TILE_LANES), F32)],
        scratch_shapes=[pltpu.VMEM((N_EXPERTS, 1), F32)],
        compiler_params=_cparams(("arbitrary",)),
        name="route",
    )(logits_t, bias_col)


def _dest_kernel(ek_ref, rk_ref, ps_ref, shift_ref, o_ref, pos_ref):
    tt = COMBINE_TT
    rows = lax.broadcasted_iota(I32, (N_EXPERTS, tt), 0)
    lane = lax.broadcasted_iota(I32, (N_EXPERTS, TILE_LANES), 1)
    dests, poss = [], []
    for sub in range(ROUTE_TN // tt):
        tile = pl.program_id(0) * (ROUTE_TN // tt) + sub
        shift_col = jnp.sum(jnp.where(lane == tile, shift_ref[...], 0), axis=1, keepdims=True)
        ds, ps = [], []
        for k in range(TOP_K):
            hit = rows == ek_ref[k:k + 1, sub * tt:(sub + 1) * tt]
            d = (jnp.sum(jnp.where(hit, ps_ref[...], 0), axis=0, keepdims=True)
                 + rk_ref[k:k + 1, sub * tt:(sub + 1) * tt])
            ds.append(d)
            ps.append(d + jnp.sum(jnp.where(hit, shift_col, 0), axis=0, keepdims=True))
        dests.append(jnp.concatenate(ds, axis=0))
        poss.append(jnp.concatenate(ps, axis=0))
    o_ref[...] = jnp.concatenate(dests, axis=1)
    pos_ref[...] = jnp.concatenate(poss, axis=1)


def _dest(ek, rk, pstarts_col, shift):
    tn = ROUTE_TN
    kt = pl.BlockSpec((TOP_K, tn), lambda i: (0, i))
    return pl.pallas_call(
        _dest_kernel,
        grid=(SEQ // tn,),
        in_specs=[kt, kt, pl.BlockSpec((N_EXPERTS, 1), lambda i: (0, 0)),
                  pl.BlockSpec((N_EXPERTS, TILE_LANES), lambda i: (0, 0))],
        out_specs=[kt, kt],
        out_shape=[jax.ShapeDtypeStruct((TOP_K, SEQ), I32), jax.ShapeDtypeStruct((TOP_K, SEQ), I32)],
        compiler_params=_cparams(("arbitrary",)),
        name="dest_rows",
    )(ek, rk, pstarts_col, shift)


DISPATCH_TT = 256
SUBLANES = 8
PAD_BITS = tuple(1 << b for b in range(MOE_BM.bit_length() - 2, 2, -1))


def _dispatch_kernel(dest_ref, padoff_ref, padlen_ref, h_ref, out_ref, zeros_ref, sem, zsem):
    tt = DISPATCH_TT
    i = pl.program_id(0)
    base = i * tt

    def issue(t, carry):
        for k in range(TOP_K):
            d = dest_ref[k * SEQ + base + t]
            pltpu.make_async_copy(h_ref.at[pl.ds(t, 1), :], out_ref.at[pl.ds(d, 1), :], sem).start()
        return carry

    lax.fori_loop(0, tt, issue, 0)

    def pad_copy(off, n):
        return pltpu.make_async_copy(zeros_ref.at[pl.ds(0, n), :], out_ref.at[pl.ds(off, n), :], zsem)

    def pad_rows(wait):
        def body(e, carry):
            start = padoff_ref[e]
            head = (-start) & (SUBLANES - 1)
            for j in range(SUBLANES - 1):
                @pl.when(j < head)
                def _():
                    cp = pad_copy(start + j, 1)
                    cp.wait() if wait else cp.start()
            off = start + head
            n = padlen_ref[e] - head
            for bit in PAD_BITS:
                @pl.when((n & bit) != 0)
                def _():
                    cp = pad_copy(pl.multiple_of(off, SUBLANES), bit)
                    cp.wait() if wait else cp.start()
                off = off + (n & bit)
            return carry
        lax.fori_loop(0, N_EXPERTS, body, 0)

    @pl.when(i == 0)
    def _():
        zeros_ref[...] = jnp.zeros_like(zeros_ref)
        pad_rows(wait=False)
        pad_rows(wait=True)

    for _ in range(TOP_K):
        pltpu.make_async_copy(h_ref, out_ref.at[pl.ds(0, tt), :], sem).wait()


def _dispatch(dest_flat, pad_off, pad_len, h2):
    tt = DISPATCH_TT
    grid_spec = pltpu.PrefetchScalarGridSpec(
        num_scalar_prefetch=3,
        grid=(SEQ // tt,),
        in_specs=[pl.BlockSpec((tt, D_MODEL), lambda i, d, po, pn: (i, 0))],
        out_specs=pl.BlockSpec(memory_space=pl.ANY),
        scratch_shapes=[
            pltpu.VMEM((MOE_BM // 2, D_MODEL), F32),
            pltpu.SemaphoreType.DMA,
            pltpu.SemaphoreType.DMA,
        ],
    )
    return pl.pallas_call(
        _dispatch_kernel,
        grid_spec=grid_spec,
        out_shape=jax.ShapeDtypeStruct((MOE_ROWS, D_MODEL), F32),
        compiler_params=_cparams(("arbitrary",)),
        name="dispatch",
    )(dest_flat, pad_off, pad_len, h2)


def _experts_kernel(bexp_ref, first_ref, slot_ref, next_ref, nused_ref,
                    x_ref, wg_hbm, wu_hbm, wd_hbm, o_ref,
                    wgf_ref, wuf_ref, wdf_ref, wgb_ref, wub_ref, wdb_ref, sem):
    b = pl.program_id(0)
    active = b < nused_ref[0]

    def weight_copies(e, s):
        return (pltpu.make_async_copy(wg_hbm.at[e], wgf_ref.at[s], sem.at[s]),
                pltpu.make_async_copy(wu_hbm.at[e], wuf_ref.at[s], sem.at[s]),
                pltpu.make_async_copy(wd_hbm.at[e], wdf_ref.at[s], sem.at[s]))

    @pl.when(jnp.logical_and(active, b == 0))
    def _():
        for cp in weight_copies(bexp_ref[0], 0):
            cp.start()

    @pl.when(jnp.logical_and(active, first_ref[b] == 1))
    def _():
        s = slot_ref[b]
        nxt = next_ref[b]

        @pl.when(nxt < N_EXPERTS)
        def _():
            for cp in weight_copies(nxt, 1 - s):
                cp.start()

        for cp in weight_copies(bexp_ref[b], s):
            cp.wait()
        wgb_ref[...] = wgf_ref[s].astype(BF16)
        wub_ref[...] = wuf_ref[s].astype(BF16)
        wdb_ref[...] = wdf_ref[s].astype(BF16)

    @pl.when(active)
    def _():
        x = x_ref[...].astype(BF16)
        gate = _dot(x, wgb_ref[...])
        up = _dot(x, wub_ref[...])
        hid = (_silu(gate) * up).astype(BF16)
        o_ref[...] = _dot(hid, wdb_ref[...]).astype(BF16)


def _experts(block_exp, first, slot, next_exp, nused, xs_perm, w_gate, w_up, w_down):
    bm = MOE_BM

    def xmap(b, be, fi, sl, ne, nu):
        return (jnp.minimum(b, nu[0] - 1), 0)

    hbm = pl.BlockSpec(memory_space=pl.ANY)
    grid_spec = pltpu.PrefetchScalarGridSpec(
        num_scalar_prefetch=5,
        grid=(MOE_NB,),
        in_specs=[pl.BlockSpec((bm, D_MODEL), xmap), hbm, hbm, hbm],
        out_specs=pl.BlockSpec((bm, D_MODEL), xmap),
        scratch_shapes=[
            pltpu.VMEM((2, D_MODEL, EXPERT_DIM), F32),
            pltpu.VMEM((2, D_MODEL, EXPERT_DIM), F32),
            pltpu.VMEM((2, EXPERT_DIM, D_MODEL), F32),
            pltpu.VMEM((D_MODEL, EXPERT_DIM), BF16),
            pltpu.VMEM((D_MODEL, EXPERT_DIM), BF16),
            pltpu.VMEM((EXPERT_DIM, D_MODEL), BF16),
            pltpu.SemaphoreType.DMA((2,)),
        ],
    )
    return pl.pallas_call(
        _experts_kernel,
        grid_spec=grid_spec,
        out_shape=jax.ShapeDtypeStruct((MOE_ROWS, D_MODEL), BF16),
        compiler_params=_cparams(("arbitrary",)),
        name="experts",
    )(block_exp, first, slot, next_exp, nused, xs_perm, w_gate, w_up, w_down)


SHARED_TM = 512


def _shared_kernel(h2_ref, wg_ref, wu_ref, wd_ref, o_ref):
    h = h2_ref[...].astype(BF16)
    hid = (_silu(_dot(h, wg_ref[...])) * _dot(h, wu_ref[...])).astype(BF16)
    o_ref[...] = _dot(hid, wd_ref[...])


def _shared(h2, wsg, wsu, wsd):
    tm = SHARED_TM
    row = pl.BlockSpec((tm, D_MODEL), lambda i: (i, 0))
    return pl.pallas_call(
        _shared_kernel,
        grid=(SEQ // tm,),
        in_specs=[row,
                  pl.BlockSpec((D_MODEL, EXPERT_DIM), lambda i: (0, 0)),
                  pl.BlockSpec((D_MODEL, EXPERT_DIM), lambda i: (0, 0)),
                  pl.BlockSpec((EXPERT_DIM, D_MODEL), lambda i: (0, 0))],
        out_specs=row,
        out_shape=jax.ShapeDtypeStruct((SEQ, D_MODEL), F32),
        compiler_params=_cparams(("arbitrary",)),
        name="shared_expert",
    )(h2, wsg, wsu, wsd)


COMBINE_ALIGN = 16
COMBINE_CHUNK = 512
COMBINE_ROWS = pl.cdiv(COMBINE_TT * TOP_K + 2 * (COMBINE_ALIGN - 1) * N_EXPERTS,
                       COMBINE_CHUNK) * COMBINE_CHUNK


def _combine_kernel(ws_ref, wl_ref, so_ref, ysh_ref, x1_ref, wk_ref, pos_ref, g2_ref, nf_ref, eo_ref,
                    o_ref, buf_ref, acc_ref, sem):
    tt = COMBINE_TT
    i = pl.program_id(0)
    slot = i % 2

    def for_windows(tile, slot_, wait):
        def body(e, carry):
            idx = tile * N_EXPERTS + e
            n = pl.multiple_of(wl_ref[idx], COMBINE_ALIGN)
            src = pl.multiple_of(ws_ref[idx], COMBINE_ALIGN)
            dst = pl.multiple_of(slot_ * COMBINE_ROWS + so_ref[idx], COMBINE_ALIGN)

            @pl.when(n > 0)
            def _():
                cp = pltpu.make_async_copy(eo_ref.at[pl.ds(src, n), :], buf_ref.at[pl.ds(dst, n), :],
                                           sem.at[slot_])
                cp.wait() if wait else cp.start()
            return carry

        lax.fori_loop(0, N_EXPERTS, body, 0)

    @pl.when(i == 0)
    def _():
        buf_ref[...] = jnp.zeros_like(buf_ref)
        for_windows(0, 0, wait=False)

    @pl.when(i + 1 < pl.num_programs(0))
    def _():
        for_windows(i + 1, 1 - slot, wait=False)

    for_windows(i, slot, wait=True)

    last = i * N_EXPERTS + N_EXPERTS - 1
    staged = so_ref[last] + wl_ref[last]
    sbase = pl.multiple_of(slot * COMBINE_ROWS, COMBINE_ROWS)
    wk = wk_ref[...]
    pos = pos_ref[...]
    acc_ref[...] = ysh_ref[...]
    for c in range(COMBINE_ROWS // COMBINE_CHUNK):
        @pl.when(c * COMBINE_CHUNK < staged)
        def _():
            col = lax.broadcasted_iota(I32, (tt, COMBINE_CHUNK), 1) + c * COMBINE_CHUNK
            w = jnp.zeros((tt, COMBINE_CHUNK), F32)
            for k in range(TOP_K):
                w = w + jnp.where(col == pos[:, k:k + 1], wk[:, k:k + 1], 0.0)
            w_hi = w.astype(BF16)
            w_lo = (w - w_hi.astype(F32)).astype(BF16)
            rows = buf_ref[pl.ds(sbase + c * COMBINE_CHUNK, COMBINE_CHUNK), :]
            acc_ref[...] += _dot(w_hi, rows) + _dot(w_lo, rows)
    x2 = x1_ref[...] + g2_ref[...] * acc_ref[...]
    o_ref[...] = _rms(x2, nf_ref[...])


def _combine(win_start, win_len, stage_off, y_shared, x1, wk_t, pos_t, g2, nf, eo):
    tt = COMBINE_TT
    vec = pl.BlockSpec((1, D_MODEL), lambda i, a, b, c: (0, 0))
    row = pl.BlockSpec((tt, D_MODEL), lambda i, a, b, c: (i, 0))
    slots = pl.BlockSpec((tt, TOP_K), lambda i, a, b, c: (i, 0))
    grid_spec = pltpu.PrefetchScalarGridSpec(
        num_scalar_prefetch=3,
        grid=(N_TILES,),
        in_specs=[row, row, slots, slots, vec, vec, pl.BlockSpec(memory_space=pl.ANY)],
        out_specs=row,
        scratch_shapes=[
            pltpu.VMEM((2 * COMBINE_ROWS, D_MODEL), BF16),
            pltpu.VMEM((tt, D_MODEL), F32),
            pltpu.SemaphoreType.DMA((2,)),
        ],
    )
    return pl.pallas_call(
        _combine_kernel,
        grid_spec=grid_spec,
        out_shape=jax.ShapeDtypeStruct((SEQ, D_MODEL), F32),
        compiler_params=_cparams(("arbitrary",)),
        name="combine_final",
    )(win_start, win_len, stage_off, y_shared, x1, wk_t, pos_t, g2, nf, eo)


def kernel(x, c, w_ada, b_ada, norm1_g, w_in, conv_w, conv_b, dt_bias, a_log, d_skip, ssd_norm_g, gla_w_gate, gla_b_gate, gla_norm_g, w_out, norm2_g, w_router, router_bias, w_e_gate, w_e_up, w_e_down, w_s_gate, w_s_up, w_s_down, normf_g):
    layer = 0
    x2d = x.reshape(SEQ, D_MODEL)
    mod = _ada(c.reshape(D_MODEL, 1), w_ada[layer], b_ada[layer].reshape(1, -1))
    sh1, sc1, g1, sh2, sc2, g2 = [mod[:, i * D_MODEL:(i + 1) * D_MODEL] for i in range(6)]

    wi = w_in[layer]
    o_z, o_xbc, o_dt, o_q, o_k, o_v, o_r, o_g, o_ma, o_mb, o_end = (
        0, 2048, 5120, 5152, 6176, 7200, 9248, 11296, 11312, 13360, 15408)
    wb = wi.astype(BF16)
    w_big = jnp.concatenate([
        wb[:, o_z:o_dt], wb[:, o_q:o_k], wb[:, o_v:o_g], wb[:, o_ma:o_end], wb[:, o_k:o_v]], axis=1)
    w_small = jnp.concatenate([
        wi[:, o_dt:o_q], wi[:, o_g:o_ma],
        jnp.zeros((D_MODEL, SMALL_COLS - SSD_HEADS - GLA_GATE_RANK), F32)], axis=1).astype(BF16)

    proj, small = _inproj(x2d, norm1_g[layer].reshape(1, -1), sc1, sh1, w_big, w_small)

    d_exp = jnp.repeat(d_skip[layer], SSD_HEAD_DIM).reshape(1, -1)
    y_a = _ssd(proj, small, conv_w[layer], conv_b[layer].reshape(1, -1),
               dt_bias[layer].reshape(1, -1), a_log[layer].reshape(1, -1), d_exp,
               ssd_norm_g[layer].reshape(1, -1))
    y_b = _gla(proj, small, gla_w_gate[layer], gla_b_gate[layer].reshape(1, -1),
               gla_norm_g[layer].reshape(1, -1))

    x1, h2, logits_t = _outproj(y_a, y_b, x2d, w_out[layer].astype(BF16), g1,
                                norm2_g[layer].reshape(1, -1), sc2, sh2, w_router[layer].T)

    ek, wk, rk, counts, cbf, ctf = _route(logits_t, router_bias[layer].reshape(-1, 1))

    counts = counts.reshape(-1)
    padded = (counts + MOE_BM - 1) // MOE_BM * MOE_BM
    pends = jnp.cumsum(padded)
    pstarts = pends - padded
    block_start = jnp.arange(MOE_NB, dtype=I32) * MOE_BM
    block_exp = jnp.minimum(
        jnp.sum((pends[None, :] <= block_start[:, None]).astype(I32), axis=1), N_EXPERTS - 1)
    nused = (pends[-1:] // MOE_BM).astype(I32)
    bidx = jnp.arange(MOE_NB, dtype=I32)
    prev_exp = jnp.concatenate([jnp.full((1,), -1, I32), block_exp[:-1]])
    first = jnp.logical_and(bidx < nused[0], block_exp != prev_exp)
    slot = (jnp.cumsum(first.astype(I32)) - 1) & 1
    later_first = jnp.logical_and(first[None, :], bidx[None, :] > bidx[:, None])
    next_first = jnp.min(jnp.where(later_first, bidx[None, :], MOE_NB), axis=1)
    next_exp = jnp.sum(jnp.where(bidx[None, :] == next_first[:, None], block_exp[None, :], 0), axis=1)
    next_exp = jnp.where(next_first < MOE_NB, next_exp, N_EXPERTS).astype(I32)

    cb = cbf[:, :N_TILES].astype(I32)
    ct = ctf[:, :N_TILES].astype(I32)
    run_start = pstarts[:, None].astype(I32) + cb
    win_start = run_start & -COMBINE_ALIGN
    win_end = (run_start + ct + COMBINE_ALIGN - 1) & -COMBINE_ALIGN
    win_len = jnp.where(ct > 0, win_end - win_start, 0)
    stage_off = jnp.cumsum(win_len, axis=0) - win_len
    shift = jnp.pad(stage_off - win_start, ((0, 0), (0, TILE_LANES - N_TILES)))
    tile_major = lambda a: a.T.reshape(-1).astype(I32)

    dest, pos = _dest(ek, rk, pstarts.reshape(-1, 1).astype(I32), shift)
    xs_perm = _dispatch(dest.reshape(-1), (pstarts + counts).astype(I32), (padded - counts).astype(I32), h2)
    eo = _experts(block_exp.astype(I32), first.astype(I32), slot.astype(I32), next_exp, nused, xs_perm,
                  w_e_gate[layer], w_e_up[layer], w_e_down[layer])
    y_shared = _shared(h2, w_s_gate[layer].astype(BF16), w_s_up[layer].astype(BF16),
                       w_s_down[layer].astype(BF16))
    out = _combine(tile_major(win_start), tile_major(win_len), tile_major(stage_off), y_shared, x1,
                   wk.T, pos.T, g2, normf_g.reshape(1, -1), eo)
    return out.reshape(x.shape)
```

```python
import functools

import jax
import jax.numpy as jnp
from jax import lax
from jax.experimental import pallas as pl
from jax.experimental.pallas import tpu as pltpu

F32 = jnp.float32
BF16 = jnp.bfloat16
I32 = jnp.int32

D_MODEL = 2048
SEQ = 8192
NORM_EPS = 1e-6
SSD_HEADS = 32
SSD_HEAD_DIM = 64
SSD_INNER = 2048
SSD_GROUPS = 4
SSD_STATE = 128
SSD_CHUNK = 128
HEADS_PER_GROUP = SSD_HEADS // SSD_GROUPS
GROUP_COLS = HEADS_PER_GROUP * SSD_HEAD_DIM
GLA_HEADS = 4
GLA_KEY_DIM = 1024
GLA_VAL_DIM = 2048
GLA_HEAD_K = 256
GLA_HEAD_V = 512
GLA_GATE_RANK = 16
GLA_GATE_NORM = 16.0
GLA_CHUNK = 64
N_EXPERTS = 64
TOP_K = 8
N_EXPERT_GROUPS = 8
TOPK_GROUPS = 4
EXPERT_DIM = 512
ROUTED_SCALE = 2.5

VMEM_LIMIT_BYTES = 56 * 1024 * 1024

COL_Z, COL_XS, COL_BC, COL_Q, COL_V, COL_R, COL_MA, COL_MB, COL_K = (
    0, 2048, 4096, 5120, 6144, 8192, 10240, 12288, 14336)
PROJ_COLS = 15360
SMALL_COLS = 128

MOE_BM = 256
MOE_NB = SEQ * TOP_K // MOE_BM + N_EXPERTS
MOE_ROWS = MOE_NB * MOE_BM


def _cparams(sem):
    return pltpu.CompilerParams(dimension_semantics=sem, vmem_limit_bytes=VMEM_LIMIT_BYTES)


def _split3(a):
    hi = a.astype(BF16)
    r1 = a - hi.astype(F32)
    mid = r1.astype(BF16)
    lo = (r1 - mid.astype(F32)).astype(BF16)
    return hi, mid, lo


def _dot(a, b):
    return jnp.dot(a, b, preferred_element_type=F32)


def _dot_nt(a, b):
    return lax.dot_general(a, b, (((1,), (1,)), ((), ())), preferred_element_type=F32)


def _dot_tn(a, b):
    return lax.dot_general(a, b, (((0,), (0,)), ((), ())), preferred_element_type=F32)


def _sel_dot(sel_bf16, a):
    hi, mid, lo = _split3(a)
    return _dot(sel_bf16, hi) + _dot(sel_bf16, mid) + _dot(sel_bf16, lo)


def _dot_sel(a, sel_bf16):
    hi, mid, lo = _split3(a)
    return _dot(hi, sel_bf16) + _dot(mid, sel_bf16) + _dot(lo, sel_bf16)


def _dot3(a, b):
    ah = a.astype(BF16)
    al = (a - ah.astype(F32)).astype(BF16)
    bh = b.astype(BF16)
    bl = (b - bh.astype(F32)).astype(BF16)
    return _dot(ah, bh) + _dot(ah, bl) + _dot(al, bh)


def _dot3_nt(a, b):
    ah = a.astype(BF16)
    al = (a - ah.astype(F32)).astype(BF16)
    bh = b.astype(BF16)
    bl = (b - bh.astype(F32)).astype(BF16)
    return _dot_nt(ah, bh) + _dot_nt(ah, bl) + _dot_nt(al, bh)


def _silu(x):
    return x * jax.nn.sigmoid(x)


def _softplus(x):
    return jnp.maximum(x, 0.0) + jnp.log1p(jnp.exp(-jnp.abs(x)))


def _rms(x, g):
    ms = jnp.mean(x * x, axis=-1, keepdims=True)
    return x * lax.rsqrt(ms + NORM_EPS) * g


ADA_TN = 1024


def _ada_kernel(c_ref, w_ref, b_ref, o_ref):
    ca = _silu(c_ref[...])
    o_ref[...] = jnp.sum(w_ref[...] * ca, axis=0, keepdims=True) + b_ref[...]


def _ada(c_col, w_ada, b_ada):
    n = w_ada.shape[1]
    return pl.pallas_call(
        _ada_kernel,
        grid=(n // ADA_TN,),
        in_specs=[
            pl.BlockSpec((D_MODEL, 1), lambda j: (0, 0)),
            pl.BlockSpec((D_MODEL, ADA_TN), lambda j: (0, j)),
            pl.BlockSpec((1, ADA_TN), lambda j: (0, j)),
        ],
        out_specs=pl.BlockSpec((1, ADA_TN), lambda j: (0, j)),
        out_shape=jax.ShapeDtypeStruct((1, n), F32),
        compiler_params=_cparams(("arbitrary",)),
        name="ada_mod",
    )(c_col, w_ada, b_ada)


IN_DIM = 15408
LANES = 128
PACK_TN = 1024
PACK_WIN = PACK_TN + LANES
W_IN_SEGMENTS = ((0, 5120), (5152, 1024), (7200, 4096), (11312, 4096), (6176, 1024))
PACK_STARTS = tuple(s + t * PACK_TN for s, w in W_IN_SEGMENTS for t in range(w // PACK_TN))
PACK_SHIFTS = tuple(sorted({c % LANES for c in PACK_STARTS}))


def _pack_kernel(base_ref, shift_ref, edge_ref, w_hbm, o_ref, win_ref, sem):
    j = pl.program_id(0)
    n = pl.num_programs(0)
    slot = j % 2

    def copies(t, s):
        b = pl.multiple_of(base_ref[t], LANES)
        main = pltpu.make_async_copy(w_hbm.at[:, pl.ds(b, PACK_TN)],
                                     win_ref.at[s, :, pl.ds(0, PACK_TN)], sem.at[s])
        extra = pltpu.make_async_copy(w_hbm.at[:, pl.ds(b + PACK_TN, LANES)],
                                      win_ref.at[s, :, pl.ds(PACK_TN, LANES)], sem.at[s])
        return main, extra, b + PACK_WIN <= IN_DIM

    def start(t, s):
        main, extra, extra_in_bounds = copies(t, s)
        main.start()

        @pl.when(extra_in_bounds)
        def _():
            extra.start()

    @pl.when(j == 0)
    def _():
        start(0, 0)

    @pl.when(j + 1 < n)
    def _():
        start(j + 1, 1 - slot)

    main, extra, extra_in_bounds = copies(j, slot)
    main.wait()

    @pl.when(extra_in_bounds)
    def _():
        extra.wait()

    @pl.when(jnp.logical_not(extra_in_bounds))
    def _():
        win_ref[slot, :, PACK_TN:PACK_WIN] = edge_ref[...]

    for sv in PACK_SHIFTS:
        @pl.when(shift_ref[j] == sv)
        def _():
            o_ref[...] = win_ref[slot][:, sv:sv + PACK_TN].astype(BF16)


def _pack_w_in(w_in2d):
    bases = jnp.asarray([c // LANES * LANES for c in PACK_STARTS], I32)
    shifts = jnp.asarray([c % LANES for c in PACK_STARTS], I32)
    tail = IN_DIM // LANES * LANES
    edge = jnp.pad(w_in2d[:, tail:], ((0, 0), (0, tail + LANES - IN_DIM)))
    grid_spec = pltpu.PrefetchScalarGridSpec(
        num_scalar_prefetch=2,
        grid=(len(PACK_STARTS),),
        in_specs=[pl.BlockSpec((D_MODEL, LANES), lambda j, b, s: (0, 0)),
                  pl.BlockSpec(memory_space=pl.ANY)],
        out_specs=pl.BlockSpec((D_MODEL, PACK_TN), lambda j, b, s: (0, j)),
        scratch_shapes=[pltpu.VMEM((2, D_MODEL, PACK_WIN), F32), pltpu.SemaphoreType.DMA((2,))],
    )
    return pl.pallas_call(
        _pack_kernel,
        grid_spec=grid_spec,
        out_shape=jax.ShapeDtypeStruct((D_MODEL, PROJ_COLS), BF16),
        compiler_params=_cparams(("arbitrary",)),
        name="pack_w_in",
    )(bases, shifts, edge, w_in2d)


INPROJ_TM = 1024
INPROJ_TN = 1024


def _inproj_kernel(x_ref, g_ref, sc_ref, sh_ref, wb_ref, ws_ref, o_ref, os_ref, h_ref):
    @pl.when(pl.program_id(1) == 0)
    def _():
        h = _rms(x_ref[...], g_ref[...]) * (1.0 + sc_ref[...]) + sh_ref[...]
        hb = h.astype(BF16)
        h_ref[...] = hb
        os_ref[...] = _dot(hb, ws_ref[...])

    o_ref[...] = _dot(h_ref[...], wb_ref[...]).astype(BF16)


def _inproj(x2d, g, sc, sh, w_big, w_small):
    vec = pl.BlockSpec((1, D_MODEL), lambda i, j: (0, 0))
    return pl.pallas_call(
        _inproj_kernel,
        grid=(SEQ // INPROJ_TM, PROJ_COLS // INPROJ_TN),
        in_specs=[
            pl.BlockSpec((INPROJ_TM, D_MODEL), lambda i, j: (i, 0)),
            vec, vec, vec,
            pl.BlockSpec((D_MODEL, INPROJ_TN), lambda i, j: (0, j)),
            pl.BlockSpec((D_MODEL, SMALL_COLS), lambda i, j: (0, 0)),
        ],
        out_specs=[
            pl.BlockSpec((INPROJ_TM, INPROJ_TN), lambda i, j: (i, j)),
            pl.BlockSpec((INPROJ_TM, SMALL_COLS), lambda i, j: (i, 0)),
        ],
        out_shape=[
            jax.ShapeDtypeStruct((SEQ, PROJ_COLS), BF16),
            jax.ShapeDtypeStruct((SEQ, SMALL_COLS), F32),
        ],
        scratch_shapes=[pltpu.VMEM((INPROJ_TM, D_MODEL), BF16)],
        compiler_params=_cparams(("arbitrary", "arbitrary")),
        name="inproj",
    )(x2d, g, sc, sh, w_big, w_small)


def _causal_conv_silu(cur, prev, w_ref, b_ref):
    row = lax.broadcasted_iota(I32, cur.shape, 0)
    acc = cur * w_ref[3:4, :] + b_ref[...]
    for s in (1, 2, 3):
        shifted = jnp.where(row >= s, pltpu.roll(cur, s, 0), pltpu.roll(prev, s, 0))
        acc = acc + shifted * w_ref[3 - s:4 - s, :]
    return _silu(acc)


def _ssd_kernel(xs_ref, bc_ref, z_ref, ma_ref, sm_ref, cwx_ref, cwb_ref, cbx_ref, cbb_ref,
                dtb_ref, alog_ref, dexp_ref, ng_ref, o_ref, px_ref, pb_ref, st_ref):
    L = SSD_CHUNK

    @pl.when(pl.program_id(0) == 0)
    def _():
        px_ref[...] = jnp.zeros_like(px_ref)
        pb_ref[...] = jnp.zeros_like(pb_ref)
        st_ref[...] = jnp.zeros_like(st_ref)

    xs_raw = xs_ref[...].astype(F32)
    bc_raw = bc_ref[...].astype(F32)
    xs = _causal_conv_silu(xs_raw, px_ref[...], cwx_ref, cbx_ref)
    bc = _causal_conv_silu(bc_raw, pb_ref[...], cwb_ref, cbb_ref)
    px_ref[...] = xs_raw
    pb_ref[...] = bc_raw

    dt = _softplus(sm_ref[:, 0:SSD_HEADS] + dtb_ref[...])
    d_a = dt * (-jnp.exp(alog_ref[...]))
    ri = lax.broadcasted_iota(I32, (L, L), 0)
    ci = lax.broadcasted_iota(I32, (L, L), 1)
    causal = ri >= ci
    tril = jnp.where(causal, 1.0, 0.0).astype(BF16)
    acum = _sel_dot(tril, d_a)
    hi32 = lax.broadcasted_iota(I32, (SSD_HEADS, SSD_HEADS), 0)
    hj32 = lax.broadcasted_iota(I32, (SSD_HEADS, SSD_HEADS), 1)
    eye32 = jnp.where(hi32 == hj32, 1.0, 0.0).astype(BF16)
    a_hi, a_mid, a_lo = _split3(acum)
    acum_t = _dot_nt(eye32, a_hi) + _dot_nt(eye32, a_mid) + _dot_nt(eye32, a_lo)
    last = acum[L - 1:L, :]
    e_a = jnp.exp(acum)
    to_end = jnp.exp(last - acum)
    eh = lax.broadcasted_iota(I32, (SSD_HEADS, SSD_INNER), 0)
    ec = lax.broadcasted_iota(I32, (SSD_HEADS, SSD_INNER), 1)
    expand = jnp.where((ec >> 6) == eh, 1.0, 0.0).astype(BF16)
    dt_x = _dot_sel(dt, expand)
    dte_x = _dot_sel(dt * to_end, expand)
    ea_x = _dot_sel(e_a, expand)
    cd_x = ea_x[L - 1:L, :]

    xdt = xs * dt_x
    lane = lax.broadcasted_iota(I32, xdt.shape, 1)
    left = (lane & 64) == 0
    xdt_l = jnp.where(left, xdt, 0.0).astype(BF16)
    xdt_r = jnp.where(left, 0.0, xdt).astype(BF16)
    xdte = (xs * dte_x).astype(BF16)

    y_diag = []
    y_off = []
    for g in range(SSD_GROUPS):
        bg = bc[:, g * SSD_STATE:(g + 1) * SSD_STATE].astype(BF16)
        cg = bc[:, 512 + g * SSD_STATE:512 + (g + 1) * SSD_STATE].astype(BF16)
        cb = _dot_nt(cg, bg)
        gsl = slice(g * GROUP_COLS, (g + 1) * GROUP_COLS)
        st = st_ref[g]
        y_off.append(_dot(cg, st.astype(BF16)))
        st_ref[g] = st * cd_x[:, gsl] + _dot_tn(bg, xdte[:, gsl])
        for p in range(HEADS_PER_GROUP // 2):
            ms = []
            for h in (g * HEADS_PER_GROUP + 2 * p, g * HEADS_PER_GROUP + 2 * p + 1):
                seg = acum[:, h:h + 1] - acum_t[h:h + 1, :]
                decay = jnp.exp(jnp.where(causal, seg, -jnp.inf))
                ms.append((cb * decay).astype(BF16))
            m_cat = jnp.concatenate(ms, axis=1)
            psl = slice((g * 4 + p) * 128, (g * 4 + p + 1) * 128)
            x_cat = jnp.concatenate([xdt_l[:, psl], xdt_r[:, psl]], axis=0)
            y_diag.append(_dot(m_cat, x_cat))

    y = (jnp.concatenate(y_diag, axis=1) + jnp.concatenate(y_off, axis=1) * ea_x
         + dexp_ref[...] * xs)
    yf = y * _silu(z_ref[...].astype(F32))
    outs = []
    for g in range(SSD_GROUPS):
        seg = yf[:, g * GROUP_COLS:(g + 1) * GROUP_COLS]
        ms = jnp.mean(seg * seg, axis=-1, keepdims=True)
        outs.append(seg * lax.rsqrt(ms + NORM_EPS))
    y_ssd = jnp.concatenate(outs, axis=1) * ng_ref[...]
    o_ref[...] = (jax.nn.sigmoid(ma_ref[...].astype(F32)) * y_ssd).astype(BF16)


def _ssd(proj, small, conv_w, conv_b, dt_bias, a_log, d_exp, norm_g):
    L = SSD_CHUNK
    full = lambda shape: pl.BlockSpec(shape, lambda i: (0, 0))
    return pl.pallas_call(
        _ssd_kernel,
        grid=(SEQ // L,),
        in_specs=[
            pl.BlockSpec((L, 2048), lambda i: (i, COL_XS // 2048)),
            pl.BlockSpec((L, 1024), lambda i: (i, COL_BC // 1024)),
            pl.BlockSpec((L, 2048), lambda i: (i, COL_Z // 2048)),
            pl.BlockSpec((L, 2048), lambda i: (i, COL_MA // 2048)),
            pl.BlockSpec((L, SMALL_COLS), lambda i: (i, 0)),
            pl.BlockSpec((4, 2048), lambda i: (0, 0)),
            pl.BlockSpec((4, 1024), lambda i: (0, 2)),
            pl.BlockSpec((1, 2048), lambda i: (0, 0)),
            pl.BlockSpec((1, 1024), lambda i: (0, 2)),
            full((1, SSD_HEADS)), full((1, SSD_HEADS)),
            full((1, SSD_INNER)), full((1, SSD_INNER)),
        ],
        out_specs=pl.BlockSpec((L, SSD_INNER), lambda i: (i, 0)),
        out_shape=jax.ShapeDtypeStruct((SEQ, SSD_INNER), BF16),
        scratch_shapes=[
            pltpu.VMEM((L, 2048), F32),
            pltpu.VMEM((L, 1024), F32),
            pltpu.VMEM((SSD_GROUPS, SSD_STATE, GROUP_COLS), F32),
        ],
        compiler_params=_cparams(("arbitrary",)),
        name="ssd",
    )(proj, proj, proj, proj, small, conv_w, conv_w, conv_b, conv_b, dt_bias, a_log, d_exp, norm_g)


def _gla_kernel(q_ref, k_ref, v_ref, r_ref, mb_ref, sm_ref, wg_ref, bg_ref, ng_ref, o_ref, s_ref):
    L = GLA_CHUNK

    @pl.when(pl.program_id(0) == 0)
    def _():
        s_ref[...] = jnp.zeros_like(s_ref)

    glr = sm_ref[:, SSD_HEADS:SSD_HEADS + GLA_GATE_RANK]
    pre = _dot3(glr, wg_ref[...]) + bg_ref[...]
    gk = (jnp.minimum(pre, 0.0) - jnp.log1p(jnp.exp(-jnp.abs(pre)))) / GLA_GATE_NORM
    ri = lax.broadcasted_iota(I32, (L, L), 0)
    ci = lax.broadcasted_iota(I32, (L, L), 1)
    causal = ri >= ci
    tril = jnp.where(causal, 1.0, 0.0).astype(BF16)
    bcum = _sel_dot(tril, gk)
    bmid = bcum[L // 2:L // 2 + 1, :]
    blast = bcum[L - 1:L, :]
    q = q_ref[...].astype(F32) * (GLA_HEAD_K ** -0.5)
    k = k_ref[...].astype(F32)
    q_rel = (q * jnp.exp(bcum - bmid)).astype(BF16)
    k_rel = (k * jnp.exp(bmid - bcum)).astype(BF16)
    q_int = (q * jnp.exp(bcum)).astype(BF16)
    k_end = (k * jnp.exp(blast - bcum)).astype(BF16)
    dec = jnp.exp(blast)
    v = v_ref[...]

    outs = []
    for h in range(GLA_HEADS):
        ks = slice(h * GLA_HEAD_K, (h + 1) * GLA_HEAD_K)
        vs = slice(h * GLA_HEAD_V, (h + 1) * GLA_HEAD_V)
        att = jnp.where(causal, _dot_nt(q_rel[:, ks], k_rel[:, ks]), 0.0)
        s_t = s_ref[h]
        o_h = _dot(att.astype(BF16), v[:, vs]) + _dot_nt(q_int[:, ks], s_t.astype(BF16))
        s_ref[h] = s_t * dec[:, ks] + _dot_tn(v[:, vs], k_end[:, ks])
        outs.append(_rms(o_h, ng_ref[...]))
    o = jnp.concatenate(outs, axis=1)
    o = o * _silu(r_ref[...].astype(F32)) * jax.nn.sigmoid(mb_ref[...].astype(F32))
    o_ref[...] = o.astype(BF16)


def _gla(proj, small, w_gate, b_gate, norm_g):
    L = GLA_CHUNK
    full = lambda shape: pl.BlockSpec(shape, lambda i: (0, 0))
    return pl.pallas_call(
        _gla_kernel,
        grid=(SEQ // L,),
        in_specs=[
            pl.BlockSpec((L, 1024), lambda i: (i, COL_Q // 1024)),
            pl.BlockSpec((L, 1024), lambda i: (i, COL_K // 1024)),
            pl.BlockSpec((L, 2048), lambda i: (i, COL_V // 2048)),
            pl.BlockSpec((L, 2048), lambda i: (i, COL_R // 2048)),
            pl.BlockSpec((L, 2048), lambda i: (i, COL_MB // 2048)),
            pl.BlockSpec((L, SMALL_COLS), lambda i: (i, 0)),
            full((GLA_GATE_RANK, GLA_KEY_DIM)), full((1, GLA_KEY_DIM)), full((1, GLA_HEAD_V)),
        ],
        out_specs=pl.BlockSpec((L, GLA_VAL_DIM), lambda i: (i, 0)),
        out_shape=jax.ShapeDtypeStruct((SEQ, GLA_VAL_DIM), BF16),
        scratch_shapes=[pltpu.VMEM((GLA_HEADS, GLA_HEAD_V, GLA_HEAD_K), F32)],
        compiler_params=_cparams(("arbitrary",)),
        name="gla",
    )(proj, proj, proj, proj, proj, small, w_gate, b_gate, norm_g)


OUTPROJ_TM = 512


def _outproj_kernel(a_ref, b_ref, x_ref, w_ref, g1_ref, n2_ref, sc_ref, sh_ref, wr_ref,
                    x1_ref, h2_ref, lg_ref):
    mixed = (a_ref[...].astype(F32) + b_ref[...].astype(F32)).astype(BF16)
    x1 = x_ref[...] + g1_ref[...] * _dot(mixed, w_ref[...])
    x1_ref[...] = x1
    h2 = _rms(x1, n2_ref[...]) * (1.0 + sc_ref[...]) + sh_ref[...]
    h2_ref[...] = h2
    lg_ref[...] = _dot3_nt(wr_ref[...], h2)


def _outproj(ya, yb, x2d, w_out_bf, g1, n2g, sc2, sh2, w_router_t):
    tm = OUTPROJ_TM
    vec = pl.BlockSpec((1, D_MODEL), lambda i: (0, 0))
    row = pl.BlockSpec((tm, D_MODEL), lambda i: (i, 0))
    return pl.pallas_call(
        _outproj_kernel,
        grid=(SEQ // tm,),
        in_specs=[row, row, row,
                  pl.BlockSpec((D_MODEL, D_MODEL), lambda i: (0, 0), pipeline_mode=pl.Buffered(1)),
                  vec, vec, vec, vec,
                  pl.BlockSpec((N_EXPERTS, D_MODEL), lambda i: (0, 0))],
        out_specs=[row, row, pl.BlockSpec((N_EXPERTS, tm), lambda i: (0, i))],
        out_shape=[jax.ShapeDtypeStruct((SEQ, D_MODEL), F32),
                   jax.ShapeDtypeStruct((SEQ, D_MODEL), F32),
                   jax.ShapeDtypeStruct((N_EXPERTS, SEQ), F32)],
        compiler_params=_cparams(("arbitrary",)),
        name="outproj_router",
    )(ya, yb, x2d, w_out_bf, g1, n2g, sc2, sh2, w_router_t)


ROUTE_TN = 512
COMBINE_TT = 128
N_TILES = SEQ // COMBINE_TT
TILE_LANES = 128


def _route_kernel(lg_ref, bias_ref, ek_ref, wk_ref, rk_ref, cnt_ref, cb_ref, ct_ref, carry_ref):
    tn = ROUTE_TN
    gsz = N_EXPERTS // N_EXPERT_GROUPS

    @pl.when(pl.program_id(0) == 0)
    def _():
        carry_ref[...] = jnp.zeros_like(carry_ref)
        cb_ref[...] = jnp.zeros_like(cb_ref)
        ct_ref[...] = jnp.zeros_like(ct_ref)

    s = jax.nn.sigmoid(lg_ref[...])
    choice = s + bias_ref[...]
    r8 = lax.broadcasted_iota(I32, (gsz, tn), 0)
    neg = -jnp.inf

    def top1(cur, rows, nrows):
        m = jnp.max(cur, axis=0, keepdims=True)
        idx = jnp.min(jnp.where(cur == m, rows, nrows), axis=0, keepdims=True)
        return m, idx

    gscores = []
    for g in range(N_EXPERT_GROUPS):
        blk = choice[g * gsz:(g + 1) * gsz, :]
        m1, i1 = top1(blk, r8, gsz)
        m2, _ = top1(jnp.where(r8 == i1, neg, blk), r8, gsz)
        gscores.append(m1 + m2)
    cur = jnp.concatenate(gscores, axis=0)
    gsel = jnp.zeros((N_EXPERT_GROUPS, tn), F32)
    for _ in range(TOPK_GROUPS):
        _, idx = top1(cur, r8, N_EXPERT_GROUPS)
        hit = r8 == idx
        gsel = jnp.where(hit, 1.0, gsel)
        cur = jnp.where(hit, neg, cur)
    emask = jnp.concatenate(
        [jnp.broadcast_to(gsel[g:g + 1, :], (gsz, tn)) for g in range(N_EXPERT_GROUPS)], axis=0)
    rows = lax.broadcasted_iota(I32, (N_EXPERTS, tn), 0)
    cur = jnp.where(emask > 0.5, choice, neg)
    sel = jnp.zeros((N_EXPERTS, tn), F32)
    hits, eks, wks = [], [], []
    for _ in range(TOP_K):
        _, idx = top1(cur, rows, N_EXPERTS)
        hit = rows == idx
        hits.append(hit)
        eks.append(idx)
        wks.append(jnp.sum(jnp.where(hit, s, 0.0), axis=0, keepdims=True))
        cur = jnp.where(hit, neg, cur)
        sel = jnp.where(hit, 1.0, sel)
    wsum = wks[0]
    for w in wks[1:]:
        wsum = wsum + w
    scale = ROUTED_SCALE / (wsum + 1e-20)

    ti = lax.broadcasted_iota(I32, (tn, tn), 0)
    tj = lax.broadcasted_iota(I32, (tn, tn), 1)
    before = jnp.where(ti < tj, 1.0, 0.0).astype(BF16)
    rank = _dot(sel.astype(BF16), before) + carry_ref[...]
    rks = [jnp.sum(jnp.where(hit, rank, 0.0), axis=0, keepdims=True) for hit in hits]

    lane = lax.broadcasted_iota(I32, (N_EXPERTS, TILE_LANES), 1)
    carry = carry_ref[...]
    cb = cb_ref[...]
    ct = ct_ref[...]
    for sub in range(tn // COMBINE_TT):
        cnt = jnp.sum(sel[:, sub * COMBINE_TT:(sub + 1) * COMBINE_TT], axis=1, keepdims=True)
        col = pl.program_id(0) * (tn // COMBINE_TT) + sub
        cb = jnp.where(lane == col, carry, cb)
        ct = jnp.where(lane == col, cnt, ct)
        carry = carry + cnt
    cb_ref[...] = cb
    ct_ref[...] = ct
    carry_ref[...] = carry

    ek_ref[...] = jnp.concatenate(eks, axis=0)
    wk_ref[...] = jnp.concatenate(wks, axis=0) * scale
    rk_ref[...] = jnp.concatenate(rks, axis=0).astype(I32)
    cnt_ref[...] = carry_ref[...].astype(I32)


def _route(logits_t, bias_col):
    tn = ROUTE_TN
    kt = pl.BlockSpec((TOP_K, tn), lambda i: (0, i))
    return pl.pallas_call(
        _route_kernel,
        grid=(SEQ // tn,),
        in_specs=[pl.BlockSpec((N_EXPERTS, tn), lambda i: (0, i)),
                  pl.BlockSpec((N_EXPERTS, 1), lambda i: (0, 0))],
        out_specs=[kt, kt, kt, pl.BlockSpec((N_EXPERTS, 1), lambda i: (0, 0)),
                   pl.BlockSpec((N_EXPERTS, TILE_LANES), lambda i: (0, 0)),
                   pl.BlockSpec((N_EXPERTS, TILE_LANES), lambda i: (0, 0))],
        out_shape=[jax.ShapeDtypeStruct((TOP_K, SEQ), I32),
                   jax.ShapeDtypeStruct((TOP_K, SEQ), F32),
                   jax.ShapeDtypeStruct((TOP_K, SEQ), I32),
                   jax.ShapeDtypeStruct((N_EXPERTS, 1), I32),
                   jax.ShapeDtypeStruct((N_EXPERTS, TILE_LANES), F32),
                   jax.ShapeDtypeStruct((N_EXPERTS, TILE_LANES), F32)],
        scratch_shapes=[pltpu.VMEM((N_EXPERTS, 1), F32)],
        compiler_params=_cparams(("arbitrary",)),
        name="route",
    )(logits_t, bias_col)


def _dest_kernel(ek_ref, rk_ref, ps_ref, shift_ref, o_ref, pos_ref):
    tt = COMBINE_TT
    rows = lax.broadcasted_iota(I32, (N_EXPERTS, tt), 0)
    lane = lax.broadcasted_iota(I32, (N_EXPERTS, TILE_LANES), 1)
    dests, poss = [], []
    for sub in range(ROUTE_TN // tt):
        tile = pl.program_id(0) * (ROUTE_TN // tt) + sub
        shift_col = jnp.sum(jnp.where(lane == tile, shift_ref[...], 0), axis=1, keepdims=True)
        ds, ps = [], []
        for k in range(TOP_K):
            hit = rows == ek_ref[k:k + 1, sub * tt:(sub + 1) * tt]
            d = (jnp.sum(jnp.where(hit, ps_ref[...], 0), axis=0, keepdims=True)
                 + rk_ref[k:k + 1, sub * tt:(sub + 1) * tt])
            ds.append(d)
            ps.append(d + jnp.sum(jnp.where(hit, shift_col, 0), axis=0, keepdims=True))
        dests.append(jnp.concatenate(ds, axis=0))
        poss.append(jnp.concatenate(ps, axis=0))
    o_ref[...] = jnp.concatenate(dests, axis=1)
    pos_ref[...] = jnp.concatenate(poss, axis=1)


def _dest(ek, rk, pstarts_col, shift):
    tn = ROUTE_TN
    kt = pl.BlockSpec((TOP_K, tn), lambda i: (0, i))
    return pl.pallas_call(
        _dest_kernel,
        grid=(SEQ // tn,),
        in_specs=[kt, kt, pl.BlockSpec((N_EXPERTS, 1), lambda i: (0, 0)),
                  pl.BlockSpec((N_EXPERTS, TILE_LANES), lambda i: (0, 0))],
        out_specs=[kt, kt],
        out_shape=[jax.ShapeDtypeStruct((TOP_K, SEQ), I32), jax.ShapeDtypeStruct((TOP_K, SEQ), I32)],
        compiler_params=_cparams(("arbitrary",)),
        name="dest_rows",
    )(ek, rk, pstarts_col, shift)


DISPATCH_TT = 256
SUBLANES = 8
PAD_BITS = tuple(1 << b for b in range(MOE_BM.bit_length() - 2, 2, -1))


def _dispatch_kernel(dest_ref, padoff_ref, padlen_ref, h_ref, out_ref, zeros_ref, sem, zsem):
    tt = DISPATCH_TT
    i = pl.program_id(0)
    base = i * tt

    def issue(t, carry):
        for k in range(TOP_K):
            d = dest_ref[k * SEQ + base + t]
            pltpu.make_async_copy(h_ref.at[pl.ds(t, 1), :], out_ref.at[pl.ds(d, 1), :], sem).start()
        return carry

    lax.fori_loop(0, tt, issue, 0)

    def pad_copy(off, n):
        return pltpu.make_async_copy(zeros_ref.at[pl.ds(0, n), :], out_ref.at[pl.ds(off, n), :], zsem)

    def pad_rows(wait):
        def body(e, carry):
            start = padoff_ref[e]
            head = (-start) & (SUBLANES - 1)
            for j in range(SUBLANES - 1):
                @pl.when(j < head)
                def _():
                    cp = pad_copy(start + j, 1)
                    cp.wait() if wait else cp.start()
            off = start + head
            n = padlen_ref[e] - head
            for bit in PAD_BITS:
                @pl.when((n & bit) != 0)
                def _():
                    cp = pad_copy(pl.multiple_of(off, SUBLANES), bit)
                    cp.wait() if wait else cp.start()
                off = off + (n & bit)
            return carry
        lax.fori_loop(0, N_EXPERTS, body, 0)

    @pl.when(i == 0)
    def _():
        zeros_ref[...] = jnp.zeros_like(zeros_ref)
        pad_rows(wait=False)
        pad_rows(wait=True)

    for _ in range(TOP_K):
        pltpu.make_async_copy(h_ref, out_ref.at[pl.ds(0, tt), :], sem).wait()


def _dispatch(dest_flat, pad_off, pad_len, h2):
    tt = DISPATCH_TT
    grid_spec = pltpu.PrefetchScalarGridSpec(
        num_scalar_prefetch=3,
        grid=(SEQ // tt,),
        in_specs=[pl.BlockSpec((tt, D_MODEL), lambda i, d, po, pn: (i, 0))],
        out_specs=pl.BlockSpec(memory_space=pl.ANY),
        scratch_shapes=[
            pltpu.VMEM((MOE_BM // 2, D_MODEL), F32),
            pltpu.SemaphoreType.DMA,
            pltpu.SemaphoreType.DMA,
        ],
    )
    return pl.pallas_call(
        _dispatch_kernel,
        grid_spec=grid_spec,
        out_shape=jax.ShapeDtypeStruct((MOE_ROWS, D_MODEL), F32),
        compiler_params=_cparams(("arbitrary",)),
        name="dispatch",
    )(dest_flat, pad_off, pad_len, h2)


def _experts_kernel(bexp_ref, first_ref, slot_ref, next_ref, nused_ref,
                    x_ref, wg_hbm, wu_hbm, wd_hbm, o_ref,
                    wgf_ref, wuf_ref, wdf_ref, wgb_ref, wub_ref, wdb_ref, sem):
    b = pl.program_id(0)
    active = b < nused_ref[0]

    def weight_copies(e, s):
        return (pltpu.make_async_copy(wg_hbm.at[e], wgf_ref.at[s], sem.at[s]),
                pltpu.make_async_copy(wu_hbm.at[e], wuf_ref.at[s], sem.at[s]),
                pltpu.make_async_copy(wd_hbm.at[e], wdf_ref.at[s], sem.at[s]))

    @pl.when(jnp.logical_and(active, b == 0))
    def _():
        for cp in weight_copies(bexp_ref[0], 0):
            cp.start()

    @pl.when(jnp.logical_and(active, first_ref[b] == 1))
    def _():
        s = slot_ref[b]
        nxt = next_ref[b]

        @pl.when(nxt < N_EXPERTS)
        def _():
            for cp in weight_copies(nxt, 1 - s):
                cp.start()

        for cp in weight_copies(bexp_ref[b], s):
            cp.wait()
        wgb_ref[...] = wgf_ref[s].astype(BF16)
        wub_ref[...] = wuf_ref[s].astype(BF16)
        wdb_ref[...] = wdf_ref[s].astype(BF16)

    @pl.when(active)
    def _():
        x = x_ref[...].astype(BF16)
        gate = _dot(x, wgb_ref[...])
        up = _dot(x, wub_ref[...])
        hid = (_silu(gate) * up).astype(BF16)
        o_ref[...] = _dot(hid, wdb_ref[...]).astype(BF16)


def _experts(block_exp, first, slot, next_exp, nused, xs_perm, w_gate, w_up, w_down):
    bm = MOE_BM

    def xmap(b, be, fi, sl, ne, nu):
        return (jnp.minimum(b, nu[0] - 1), 0)

    hbm = pl.BlockSpec(memory_space=pl.ANY)
    grid_spec = pltpu.PrefetchScalarGridSpec(
        num_scalar_prefetch=5,
        grid=(MOE_NB,),
        in_specs=[pl.BlockSpec((bm, D_MODEL), xmap), hbm, hbm, hbm],
        out_specs=pl.BlockSpec((bm, D_MODEL), xmap),
        scratch_shapes=[
            pltpu.VMEM((2, D_MODEL, EXPERT_DIM), F32),
            pltpu.VMEM((2, D_MODEL, EXPERT_DIM), F32),
            pltpu.VMEM((2, EXPERT_DIM, D_MODEL), F32),
            pltpu.VMEM((D_MODEL, EXPERT_DIM), BF16),
            pltpu.VMEM((D_MODEL, EXPERT_DIM), BF16),
            pltpu.VMEM((EXPERT_DIM, D_MODEL), BF16),
            pltpu.SemaphoreType.DMA((2,)),
        ],
    )
    return pl.pallas_call(
        _experts_kernel,
        grid_spec=grid_spec,
        out_shape=jax.ShapeDtypeStruct((MOE_ROWS, D_MODEL), BF16),
        compiler_params=_cparams(("arbitrary",)),
        name="experts",
    )(block_exp, first, slot, next_exp, nused, xs_perm, w_gate, w_up, w_down)


SHARED_TM = 512


def _shared_kernel(h2_ref, wg_ref, wu_ref, wd_ref, o_ref):
    h = h2_ref[...].astype(BF16)
    hid = (_silu(_dot(h, wg_ref[...])) * _dot(h, wu_ref[...])).astype(BF16)
    o_ref[...] = _dot(hid, wd_ref[...])


def _shared(h2, wsg, wsu, wsd):
    tm = SHARED_TM
    row = pl.BlockSpec((tm, D_MODEL), lambda i: (i, 0))
    return pl.pallas_call(
        _shared_kernel,
        grid=(SEQ // tm,),
        in_specs=[row,
                  pl.BlockSpec((D_MODEL, EXPERT_DIM), lambda i: (0, 0)),
                  pl.BlockSpec((D_MODEL, EXPERT_DIM), lambda i: (0, 0)),
                  pl.BlockSpec((EXPERT_DIM, D_MODEL), lambda i: (0, 0))],
        out_specs=row,
        out_shape=jax.ShapeDtypeStruct((SEQ, D_MODEL), F32),
        compiler_params=_cparams(("arbitrary",)),
        name="shared_expert",
    )(h2, wsg, wsu, wsd)


COMBINE_ALIGN = 16
COMBINE_CHUNK = 512
COMBINE_ROWS = pl.cdiv(COMBINE_TT * TOP_K + 2 * (COMBINE_ALIGN - 1) * N_EXPERTS,
                       COMBINE_CHUNK) * COMBINE_CHUNK


def _combine_kernel(ws_ref, wl_ref, so_ref, ysh_ref, x1_ref, wk_ref, pos_ref, g2_ref, nf_ref, eo_ref,
                    o_ref, buf_ref, acc_ref, sem):
    tt = COMBINE_TT
    i = pl.program_id(0)
    slot = i % 2

    def start_windows(tile, slot_):
        def body(e, carry):
            idx = tile * N_EXPERTS + e
            n = pl.multiple_of(wl_ref[idx], COMBINE_ALIGN)
            src = pl.multiple_of(ws_ref[idx], COMBINE_ALIGN)
            dst = pl.multiple_of(slot_ * COMBINE_ROWS + so_ref[idx], COMBINE_ALIGN)

            @pl.when(n > 0)
            def _():
                pltpu.make_async_copy(eo_ref.at[pl.ds(src, n), :], buf_ref.at[pl.ds(dst, n), :],
                                      sem.at[slot_]).start()
            return carry

        lax.fori_loop(0, N_EXPERTS, body, 0)

    @pl.when(i == 0)
    def _():
        buf_ref[...] = jnp.zeros_like(buf_ref)
        start_windows(0, 0)

    @pl.when(i + 1 < pl.num_programs(0))
    def _():
        start_windows(i + 1, 1 - slot)

    last = i * N_EXPERTS + N_EXPERTS - 1
    staged = pl.multiple_of(so_ref[last] + wl_ref[last], COMBINE_ALIGN)
    sbase = pl.multiple_of(slot * COMBINE_ROWS, COMBINE_ROWS)
    pltpu.make_async_copy(eo_ref.at[pl.ds(0, staged), :], buf_ref.at[pl.ds(sbase, staged), :],
                          sem.at[slot]).wait()
    wk = wk_ref[...]
    pos = pos_ref[...]
    acc_ref[...] = ysh_ref[...]
    for c in range(COMBINE_ROWS // COMBINE_CHUNK):
        @pl.when(c * COMBINE_CHUNK < staged)
        def _():
            col = lax.broadcasted_iota(I32, (tt, COMBINE_CHUNK), 1) + c * COMBINE_CHUNK
            w = jnp.zeros((tt, COMBINE_CHUNK), F32)
            for k in range(TOP_K):
                w = w + jnp.where(col == pos[:, k:k + 1], wk[:, k:k + 1], 0.0)
            w_hi = w.astype(BF16)
            w_lo = (w - w_hi.astype(F32)).astype(BF16)
            rows = buf_ref[pl.ds(sbase + c * COMBINE_CHUNK, COMBINE_CHUNK), :]
            both = _dot(jnp.concatenate([w_hi, w_lo], axis=0), rows)
            acc_ref[...] += both[:tt] + both[tt:]
    x2 = x1_ref[...] + g2_ref[...] * acc_ref[...]
    o_ref[...] = _rms(x2, nf_ref[...])


def _combine(win_start, win_len, stage_off, y_shared, x1, wk_t, pos_t, g2, nf, eo):
    tt = COMBINE_TT
    vec = pl.BlockSpec((1, D_MODEL), lambda i, a, b, c: (0, 0))
    row = pl.BlockSpec((tt, D_MODEL), lambda i, a, b, c: (i, 0))
    slots = pl.BlockSpec((tt, TOP_K), lambda i, a, b, c: (i, 0))
    grid_spec = pltpu.PrefetchScalarGridSpec(
        num_scalar_prefetch=3,
        grid=(N_TILES,),
        in_specs=[row, row, slots, slots, vec, vec, pl.BlockSpec(memory_space=pl.ANY)],
        out_specs=row,
        scratch_shapes=[
            pltpu.VMEM((2 * COMBINE_ROWS, D_MODEL), BF16),
            pltpu.VMEM((tt, D_MODEL), F32),
            pltpu.SemaphoreType.DMA((2,)),
        ],
    )
    return pl.pallas_call(
        _combine_kernel,
        grid_spec=grid_spec,
        out_shape=jax.ShapeDtypeStruct((SEQ, D_MODEL), F32),
        compiler_params=_cparams(("arbitrary",)),
        name="combine_final",
    )(win_start, win_len, stage_off, y_shared, x1, wk_t, pos_t, g2, nf, eo)


def kernel(x, c, w_ada, b_ada, norm1_g, w_in, conv_w, conv_b, dt_bias, a_log, d_skip, ssd_norm_g, gla_w_gate, gla_b_gate, gla_norm_g, w_out, norm2_g, w_router, router_bias, w_e_gate, w_e_up, w_e_down, w_s_gate, w_s_up, w_s_down, normf_g):
    layer = 0
    x2d = x.reshape(SEQ, D_MODEL)
    mod = _ada(c.reshape(D_MODEL, 1), w_ada[layer], b_ada[layer].reshape(1, -1))
    sh1, sc1, g1, sh2, sc2, g2 = [mod[:, i * D_MODEL:(i + 1) * D_MODEL] for i in range(6)]

    wi = w_in[layer]
    o_z, o_xbc, o_dt, o_q, o_k, o_v, o_r, o_g, o_ma, o_mb, o_end = (
        0, 2048, 5120, 5152, 6176, 7200, 9248, 11296, 11312, 13360, 15408)
    w_big = _pack_w_in(wi)
    w_small = jnp.concatenate([
        wi[:, o_dt:o_q], wi[:, o_g:o_ma],
        jnp.zeros((D_MODEL, SMALL_COLS - SSD_HEADS - GLA_GATE_RANK), F32)], axis=1).astype(BF16)

    proj, small = _inproj(x2d, norm1_g[layer].reshape(1, -1), sc1, sh1, w_big, w_small)

    d_exp = jnp.repeat(d_skip[layer], SSD_HEAD_DIM).reshape(1, -1)
    y_a = _ssd(proj, small, conv_w[layer], conv_b[layer].reshape(1, -1),
               dt_bias[layer].reshape(1, -1), a_log[layer].reshape(1, -1), d_exp,
               ssd_norm_g[layer].reshape(1, -1))
    y_b = _gla(proj, small, gla_w_gate[layer], gla_b_gate[layer].reshape(1, -1),
               gla_norm_g[layer].reshape(1, -1))

    x1, h2, logits_t = _outproj(y_a, y_b, x2d, w_out[layer].astype(BF16), g1,
                                norm2_g[layer].reshape(1, -1), sc2, sh2, w_router[layer].T)

    ek, wk, rk, counts, cbf, ctf = _route(logits_t, router_bias[layer].reshape(-1, 1))

    counts = counts.reshape(-1)
    padded = (counts + MOE_BM - 1) // MOE_BM * MOE_BM
    pends = jnp.cumsum(padded)
    pstarts = pends - padded
    block_start = jnp.arange(MOE_NB, dtype=I32) * MOE_BM
    block_exp = jnp.minimum(
        jnp.sum((pends[None, :] <= block_start[:, None]).astype(I32), axis=1), N_EXPERTS - 1)
    nused = (pends[-1:] // MOE_BM).astype(I32)
    bidx = jnp.arange(MOE_NB, dtype=I32)
    prev_exp = jnp.concatenate([jnp.full((1,), -1, I32), block_exp[:-1]])
    first = jnp.logical_and(bidx < nused[0], block_exp != prev_exp)
    slot = (jnp.cumsum(first.astype(I32)) - 1) & 1
    later_first = jnp.logical_and(first[None, :], bidx[None, :] > bidx[:, None])
    next_first = jnp.min(jnp.where(later_first, bidx[None, :], MOE_NB), axis=1)
    next_exp = jnp.sum(jnp.where(bidx[None, :] == next_first[:, None], block_exp[None, :], 0), axis=1)
    next_exp = jnp.where(next_first < MOE_NB, next_exp, N_EXPERTS).astype(I32)

    cb = cbf[:, :N_TILES].astype(I32)
    ct = ctf[:, :N_TILES].astype(I32)
    run_start = pstarts[:, None].astype(I32) + cb
    win_start = run_start & -COMBINE_ALIGN
    win_end = (run_start + ct + COMBINE_ALIGN - 1) & -COMBINE_ALIGN
    win_len = jnp.where(ct > 0, win_end - win_start, 0)
    stage_off = jnp.cumsum(win_len, axis=0) - win_len
    shift = jnp.pad(stage_off - win_start, ((0, 0), (0, TILE_LANES - N_TILES)))
    tile_major = lambda a: a.T.reshape(-1).astype(I32)

    dest, pos = _dest(ek, rk, pstarts.reshape(-1, 1).astype(I32), shift)
    xs_perm = _dispatch(dest.reshape(-1), (pstarts + counts).astype(I32), (padded - counts).astype(I32), h2)
    eo = _experts(block_exp.astype(I32), first.astype(I32), slot.astype(I32), next_exp, nused, xs_perm,
                  w_e_gate[layer], w_e_up[layer], w_e_down[layer])
    y_shared = _shared(h2, w_s_gate[layer].astype(BF16), w_s_up[layer].astype(BF16),
                       w_s_down[layer].astype(BF16))
    out = _combine(tile_major(win_start), tile_major(win_len), tile_major(stage_off), y_shared, x1,
                   wk.T, pos.T, g2, normf_g.reshape(1, -1), eo)
    return out.reshape(x.shape)
```

```python
import functools

import jax
import jax.numpy as jnp
from jax import lax
from jax.experimental import pallas as pl
from jax.experimental.pallas import tpu as pltpu

F32 = jnp.float32
BF16 = jnp.bfloat16
I32 = jnp.int32

D_MODEL = 2048
SEQ = 8192
NORM_EPS = 1e-6
SSD_HEADS = 32
SSD_HEAD_DIM = 64
SSD_INNER = 2048
SSD_GROUPS = 4
SSD_STATE = 128
SSD_CHUNK = 128
HEADS_PER_GROUP = SSD_HEADS // SSD_GROUPS
GROUP_COLS = HEADS_PER_GROUP * SSD_HEAD_DIM
GLA_HEADS = 4
GLA_KEY_DIM = 1024
GLA_VAL_DIM = 2048
GLA_HEAD_K = 256
GLA_HEAD_V = 512
GLA_GATE_RANK = 16
GLA_GATE_NORM = 16.0
GLA_CHUNK = 64
GLA_STEP = 128
N_EXPERTS = 64
TOP_K = 8
N_EXPERT_GROUPS = 8
TOPK_GROUPS = 4
EXPERT_DIM = 512
ROUTED_SCALE = 2.5

VMEM_LIMIT_BYTES = 56 * 1024 * 1024
SUBLANES = 8

COL_Z, COL_XS, COL_BC, COL_Q, COL_V, COL_R, COL_MA, COL_MB, COL_K = (
    0, 2048, 4096, 5120, 6144, 8192, 10240, 12288, 14336)
PROJ_COLS = 15360
SMALL_COLS = 128

MOE_BM = 256
MOE_NB = SEQ * TOP_K // MOE_BM + N_EXPERTS
MOE_ROWS = MOE_NB * MOE_BM


def _cparams(sem):
    return pltpu.CompilerParams(dimension_semantics=sem, vmem_limit_bytes=VMEM_LIMIT_BYTES)


def _split3(a):
    hi = a.astype(BF16)
    r1 = a - hi.astype(F32)
    mid = r1.astype(BF16)
    lo = (r1 - mid.astype(F32)).astype(BF16)
    return hi, mid, lo


def _dot(a, b):
    return jnp.dot(a, b, preferred_element_type=F32)


def _dot_nt(a, b):
    return lax.dot_general(a, b, (((1,), (1,)), ((), ())), preferred_element_type=F32)


def _dot_tn(a, b):
    return lax.dot_general(a, b, (((0,), (0,)), ((), ())), preferred_element_type=F32)


def _sel_dot(sel_bf16, a):
    hi, mid, lo = _split3(a)
    return _dot(sel_bf16, hi) + _dot(sel_bf16, mid) + _dot(sel_bf16, lo)


def _dot_sel(a, sel_bf16):
    hi, mid, lo = _split3(a)
    return _dot(hi, sel_bf16) + _dot(mid, sel_bf16) + _dot(lo, sel_bf16)


def _dot3(a, b):
    ah = a.astype(BF16)
    al = (a - ah.astype(F32)).astype(BF16)
    bh = b.astype(BF16)
    bl = (b - bh.astype(F32)).astype(BF16)
    return _dot(ah, bh) + _dot(ah, bl) + _dot(al, bh)


def _dot3_nt(a, b):
    ah = a.astype(BF16)
    al = (a - ah.astype(F32)).astype(BF16)
    bh = b.astype(BF16)
    bl = (b - bh.astype(F32)).astype(BF16)
    return _dot_nt(ah, bh) + _dot_nt(ah, bl) + _dot_nt(al, bh)


def _silu(x):
    return x * jax.nn.sigmoid(x)


def _softplus(x):
    return jnp.maximum(x, 0.0) + jnp.log1p(jnp.exp(-jnp.abs(x)))


def _rms(x, g):
    ms = jnp.mean(x * x, axis=-1, keepdims=True)
    return x * lax.rsqrt(ms + NORM_EPS) * g


ADA_TN = 1024


def _ada_kernel(c_ref, w_ref, b_ref, o_ref):
    ca = _silu(c_ref[...])
    o_ref[...] = jnp.sum(w_ref[...] * ca, axis=0, keepdims=True) + b_ref[...]


def _ada(c_col, w_ada, b_ada):
    n = w_ada.shape[1]
    return pl.pallas_call(
        _ada_kernel,
        grid=(n // ADA_TN,),
        in_specs=[
            pl.BlockSpec((D_MODEL, 1), lambda j: (0, 0)),
            pl.BlockSpec((D_MODEL, ADA_TN), lambda j: (0, j)),
            pl.BlockSpec((1, ADA_TN), lambda j: (0, j)),
        ],
        out_specs=pl.BlockSpec((1, ADA_TN), lambda j: (0, j)),
        out_shape=jax.ShapeDtypeStruct((1, n), F32),
        compiler_params=_cparams(("arbitrary",)),
        name="ada_mod",
    )(c_col, w_ada, b_ada)


IN_DIM = 15408
LANES = 128
PACK_TN = 1024
PACK_WIN = PACK_TN + LANES
W_IN_SEGMENTS = ((0, 5120), (5152, 1024), (7200, 4096), (11312, 4096), (6176, 1024))
PACK_STARTS = tuple(s + t * PACK_TN for s, w in W_IN_SEGMENTS for t in range(w // PACK_TN))
PACK_SHIFTS = tuple(sorted({c % LANES for c in PACK_STARTS}))


def _pack_kernel(base_ref, shift_ref, edge_ref, w_hbm, o_ref, win_ref, sem):
    j = pl.program_id(0)
    n = pl.num_programs(0)
    slot = j % 2

    def copies(t, s):
        b = pl.multiple_of(base_ref[t], LANES)
        main = pltpu.make_async_copy(w_hbm.at[:, pl.ds(b, PACK_TN)],
                                     win_ref.at[s, :, pl.ds(0, PACK_TN)], sem.at[s])
        extra = pltpu.make_async_copy(w_hbm.at[:, pl.ds(b + PACK_TN, LANES)],
                                      win_ref.at[s, :, pl.ds(PACK_TN, LANES)], sem.at[s])
        return main, extra, b + PACK_WIN <= IN_DIM

    def start(t, s):
        main, extra, extra_in_bounds = copies(t, s)
        main.start()

        @pl.when(extra_in_bounds)
        def _():
            extra.start()

    @pl.when(j == 0)
    def _():
        start(0, 0)

    @pl.when(j + 1 < n)
    def _():
        start(j + 1, 1 - slot)

    main, extra, extra_in_bounds = copies(j, slot)
    main.wait()

    @pl.when(extra_in_bounds)
    def _():
        extra.wait()

    @pl.when(jnp.logical_not(extra_in_bounds))
    def _():
        win_ref[slot, :, PACK_TN:PACK_WIN] = edge_ref[...]

    for sv in PACK_SHIFTS:
        @pl.when(shift_ref[j] == sv)
        def _():
            o_ref[...] = win_ref[slot][:, sv:sv + PACK_TN].astype(BF16)


def _pack_w_in(w_in2d):
    bases = jnp.asarray([c // LANES * LANES for c in PACK_STARTS], I32)
    shifts = jnp.asarray([c % LANES for c in PACK_STARTS], I32)
    tail = IN_DIM // LANES * LANES
    edge = jnp.pad(w_in2d[:, tail:], ((0, 0), (0, tail + LANES - IN_DIM)))
    grid_spec = pltpu.PrefetchScalarGridSpec(
        num_scalar_prefetch=2,
        grid=(len(PACK_STARTS),),
        in_specs=[pl.BlockSpec((D_MODEL, LANES), lambda j, b, s: (0, 0)),
                  pl.BlockSpec(memory_space=pl.ANY)],
        out_specs=pl.BlockSpec((D_MODEL, PACK_TN), lambda j, b, s: (0, j)),
        scratch_shapes=[pltpu.VMEM((2, D_MODEL, PACK_WIN), F32), pltpu.SemaphoreType.DMA((2,))],
    )
    return pl.pallas_call(
        _pack_kernel,
        grid_spec=grid_spec,
        out_shape=jax.ShapeDtypeStruct((D_MODEL, PROJ_COLS), BF16),
        compiler_params=_cparams(("arbitrary",)),
        name="pack_w_in",
    )(bases, shifts, edge, w_in2d)


INPROJ_TM = 1024
INPROJ_TN = 1024


def _inproj_kernel(x_ref, g_ref, sc_ref, sh_ref, wb_ref, ws_ref, o_ref, os_ref, h_ref):
    @pl.when(pl.program_id(1) == 0)
    def _():
        h = _rms(x_ref[...], g_ref[...]) * (1.0 + sc_ref[...]) + sh_ref[...]
        hb = h.astype(BF16)
        h_ref[...] = hb
        os_ref[...] = _dot(hb, ws_ref[...].astype(BF16))

    o_ref[...] = _dot(h_ref[...], wb_ref[...]).astype(BF16)


def _inproj(x2d, g, sc, sh, w_big, w_small):
    vec = pl.BlockSpec((1, D_MODEL), lambda i, j: (0, 0))
    return pl.pallas_call(
        _inproj_kernel,
        grid=(SEQ // INPROJ_TM, PROJ_COLS // INPROJ_TN),
        in_specs=[
            pl.BlockSpec((INPROJ_TM, D_MODEL), lambda i, j: (i, 0)),
            vec, vec, vec,
            pl.BlockSpec((D_MODEL, INPROJ_TN), lambda i, j: (0, j)),
            pl.BlockSpec((D_MODEL, SMALL_COLS), lambda i, j: (0, 0)),
        ],
        out_specs=[
            pl.BlockSpec((INPROJ_TM, INPROJ_TN), lambda i, j: (i, j)),
            pl.BlockSpec((INPROJ_TM, SMALL_COLS), lambda i, j: (i, 0)),
        ],
        out_shape=[
            jax.ShapeDtypeStruct((SEQ, PROJ_COLS), BF16),
            jax.ShapeDtypeStruct((SEQ, SMALL_COLS), F32),
        ],
        scratch_shapes=[pltpu.VMEM((INPROJ_TM, D_MODEL), BF16)],
        compiler_params=_cparams(("arbitrary", "arbitrary")),
        name="inproj",
    )(x2d, g, sc, sh, w_big, w_small)


def _causal_conv_silu(cur, prev, w_ref, b_ref):
    row = lax.broadcasted_iota(I32, cur.shape, 0)
    acc = cur * w_ref[3:4, :] + b_ref[...]
    for s in (1, 2, 3):
        shifted = jnp.where(row >= s, pltpu.roll(cur, s, 0), pltpu.roll(prev, s, 0))
        acc = acc + shifted * w_ref[3 - s:4 - s, :]
    return _silu(acc)


def _ssd_kernel(xs_ref, bc_ref, z_ref, ma_ref, sm_ref, cwx_ref, cwb_ref, cbx_ref, cbb_ref,
                dtb_ref, alog_ref, dexp_ref, ng_ref, o_ref, px_ref, pb_ref, st_ref):
    L = SSD_CHUNK

    @pl.when(pl.program_id(0) == 0)
    def _():
        px_ref[...] = jnp.zeros_like(px_ref)
        pb_ref[...] = jnp.zeros_like(pb_ref)
        st_ref[...] = jnp.zeros_like(st_ref)

    xs_raw = xs_ref[...].astype(F32)
    bc_raw = bc_ref[...].astype(F32)
    xs = _causal_conv_silu(xs_raw, px_ref[...], cwx_ref, cbx_ref)
    bc = _causal_conv_silu(bc_raw, pb_ref[...], cwb_ref, cbb_ref)
    px_ref[...] = xs_raw
    pb_ref[...] = bc_raw

    dt = _softplus(sm_ref[:, 0:SSD_HEADS] + dtb_ref[...])
    d_a = dt * (-jnp.exp(alog_ref[...]))
    ri = lax.broadcasted_iota(I32, (L, L), 0)
    ci = lax.broadcasted_iota(I32, (L, L), 1)
    causal = ri >= ci
    tril = jnp.where(causal, 1.0, 0.0).astype(BF16)
    acum = _sel_dot(tril, d_a)
    hi32 = lax.broadcasted_iota(I32, (SSD_HEADS, SSD_HEADS), 0)
    hj32 = lax.broadcasted_iota(I32, (SSD_HEADS, SSD_HEADS), 1)
    eye32 = jnp.where(hi32 == hj32, 1.0, 0.0).astype(BF16)
    a_hi, a_mid, a_lo = _split3(acum)
    acum_t = _dot_nt(eye32, a_hi) + _dot_nt(eye32, a_mid) + _dot_nt(eye32, a_lo)
    last = acum[L - 1:L, :]
    e_a = jnp.exp(acum)
    to_end = jnp.exp(last - acum)
    eh = lax.broadcasted_iota(I32, (SSD_HEADS, SSD_INNER), 0)
    ec = lax.broadcasted_iota(I32, (SSD_HEADS, SSD_INNER), 1)
    expand = jnp.where((ec >> 6) == eh, 1.0, 0.0).astype(BF16)
    dt_x = _dot_sel(dt, expand)
    dte_x = _dot_sel(dt * to_end, expand)
    ea_x = _dot_sel(e_a, expand)
    cd_x = ea_x[L - 1:L, :]

    xdt = xs * dt_x
    lane = lax.broadcasted_iota(I32, xdt.shape, 1)
    left = (lane & 64) == 0
    xdt_l = jnp.where(left, xdt, 0.0).astype(BF16)
    xdt_r = jnp.where(left, 0.0, xdt).astype(BF16)
    xdte = (xs * dte_x).astype(BF16)

    y_diag = []
    y_off = []
    for g in range(SSD_GROUPS):
        bg = bc[:, g * SSD_STATE:(g + 1) * SSD_STATE].astype(BF16)
        cg = bc[:, 512 + g * SSD_STATE:512 + (g + 1) * SSD_STATE].astype(BF16)
        cb = _dot_nt(cg, bg)
        gsl = slice(g * GROUP_COLS, (g + 1) * GROUP_COLS)
        st = st_ref[g]
        y_off.append(_dot(cg, st.astype(BF16)))
        st_ref[g] = st * cd_x[:, gsl] + _dot_tn(bg, xdte[:, gsl])
        for p in range(HEADS_PER_GROUP // 2):
            ms = []
            for h in (g * HEADS_PER_GROUP + 2 * p, g * HEADS_PER_GROUP + 2 * p + 1):
                seg = acum[:, h:h + 1] - acum_t[h:h + 1, :]
                decay = jnp.exp(jnp.where(causal, seg, -jnp.inf))
                ms.append((cb * decay).astype(BF16))
            m_cat = jnp.concatenate(ms, axis=1)
            psl = slice((g * 4 + p) * 128, (g * 4 + p + 1) * 128)
            x_cat = jnp.concatenate([xdt_l[:, psl], xdt_r[:, psl]], axis=0)
            y_diag.append(_dot(m_cat, x_cat))

    y = (jnp.concatenate(y_diag, axis=1) + jnp.concatenate(y_off, axis=1) * ea_x
         + dexp_ref[...] * xs)
    yf = y * _silu(z_ref[...].astype(F32))
    outs = []
    for g in range(SSD_GROUPS):
        seg = yf[:, g * GROUP_COLS:(g + 1) * GROUP_COLS]
        ms = jnp.mean(seg * seg, axis=-1, keepdims=True)
        outs.append(seg * lax.rsqrt(ms + NORM_EPS))
    y_ssd = jnp.concatenate(outs, axis=1) * ng_ref[...]
    o_ref[...] = (jax.nn.sigmoid(ma_ref[...].astype(F32)) * y_ssd).astype(BF16)


def _ssd(proj, small, conv_w, conv_b, dt_bias, a_log, d_exp, norm_g):
    L = SSD_CHUNK
    full = lambda shape: pl.BlockSpec(shape, lambda i: (0, 0))
    return pl.pallas_call(
        _ssd_kernel,
        grid=(SEQ // L,),
        in_specs=[
            pl.BlockSpec((L, 2048), lambda i: (i, COL_XS // 2048)),
            pl.BlockSpec((L, 1024), lambda i: (i, COL_BC // 1024)),
            pl.BlockSpec((L, 2048), lambda i: (i, COL_Z // 2048)),
            pl.BlockSpec((L, 2048), lambda i: (i, COL_MA // 2048)),
            pl.BlockSpec((L, SMALL_COLS), lambda i: (i, 0)),
            pl.BlockSpec((4, 2048), lambda i: (0, 0)),
            pl.BlockSpec((4, 1024), lambda i: (0, 2)),
            pl.BlockSpec((1, 2048), lambda i: (0, 0)),
            pl.BlockSpec((1, 1024), lambda i: (0, 2)),
            full((1, SSD_HEADS)), full((1, SSD_HEADS)),
            full((1, SSD_INNER)), full((1, SSD_INNER)),
        ],
        out_specs=pl.BlockSpec((L, SSD_INNER), lambda i: (i, 0)),
        out_shape=jax.ShapeDtypeStruct((SEQ, SSD_INNER), BF16),
        scratch_shapes=[
            pltpu.VMEM((L, 2048), F32),
            pltpu.VMEM((L, 1024), F32),
            pltpu.VMEM((SSD_GROUPS, SSD_STATE, GROUP_COLS), F32),
        ],
        compiler_params=_cparams(("arbitrary",)),
        name="ssd",
    )(proj, proj, proj, proj, small, conv_w, conv_w, conv_b, conv_b, dt_bias, a_log, d_exp, norm_g)


def _gla_kernel(q_ref, k_ref, v_ref, r_ref, mb_ref, sm_ref, wg_ref, bg_ref, ng_ref, o_ref, s_ref):
    L = GLA_CHUNK

    @pl.when(pl.program_id(0) == 0)
    def _():
        s_ref[...] = jnp.zeros_like(s_ref)

    ri = lax.broadcasted_iota(I32, (L, L), 0)
    ci = lax.broadcasted_iota(I32, (L, L), 1)
    causal = ri >= ci
    tril = jnp.where(causal, 1.0, 0.0).astype(BF16)

    for c in range(GLA_STEP // L):
        rs = slice(c * L, (c + 1) * L)
        glr = sm_ref[rs, SSD_HEADS:SSD_HEADS + GLA_GATE_RANK]
        pre = _dot3(glr, wg_ref[...]) + bg_ref[...]
        gk = (jnp.minimum(pre, 0.0) - jnp.log1p(jnp.exp(-jnp.abs(pre)))) / GLA_GATE_NORM
        bcum = _sel_dot(tril, gk)
        bmid = bcum[L // 2:L // 2 + 1, :]
        blast = bcum[L - 1:L, :]
        q = q_ref[rs, :].astype(F32) * (GLA_HEAD_K ** -0.5)
        k = k_ref[rs, :].astype(F32)
        q_rel = (q * jnp.exp(bcum - bmid)).astype(BF16)
        k_rel = (k * jnp.exp(bmid - bcum)).astype(BF16)
        q_int = (q * jnp.exp(bcum)).astype(BF16)
        k_end = (k * jnp.exp(blast - bcum)).astype(BF16)
        dec = jnp.exp(blast)
        v = v_ref[rs, :]

        outs = []
        for h in range(GLA_HEADS):
            ks = slice(h * GLA_HEAD_K, (h + 1) * GLA_HEAD_K)
            vs = slice(h * GLA_HEAD_V, (h + 1) * GLA_HEAD_V)
            att = jnp.where(causal, _dot_nt(q_rel[:, ks], k_rel[:, ks]), 0.0)
            s_t = s_ref[h]
            o_h = _dot(att.astype(BF16), v[:, vs]) + _dot_nt(q_int[:, ks], s_t.astype(BF16))
            s_ref[h] = s_t * dec[:, ks] + _dot_tn(v[:, vs], k_end[:, ks])
            outs.append(_rms(o_h, ng_ref[...]))
        o = jnp.concatenate(outs, axis=1)
        o = o * _silu(r_ref[rs, :].astype(F32)) * jax.nn.sigmoid(mb_ref[rs, :].astype(F32))
        o_ref[rs, :] = o.astype(BF16)


def _gla(proj, small, w_gate, b_gate, norm_g):
    L = GLA_STEP
    full = lambda shape: pl.BlockSpec(shape, lambda i: (0, 0))
    return pl.pallas_call(
        _gla_kernel,
        grid=(SEQ // L,),
        in_specs=[
            pl.BlockSpec((L, 1024), lambda i: (i, COL_Q // 1024)),
            pl.BlockSpec((L, 1024), lambda i: (i, COL_K // 1024)),
            pl.BlockSpec((L, 2048), lambda i: (i, COL_V // 2048)),
            pl.BlockSpec((L, 2048), lambda i: (i, COL_R // 2048)),
            pl.BlockSpec((L, 2048), lambda i: (i, COL_MB // 2048)),
            pl.BlockSpec((L, SMALL_COLS), lambda i: (i, 0)),
            full((GLA_GATE_RANK, GLA_KEY_DIM)), full((1, GLA_KEY_DIM)), full((1, GLA_HEAD_V)),
        ],
        out_specs=pl.BlockSpec((L, GLA_VAL_DIM), lambda i: (i, 0)),
        out_shape=jax.ShapeDtypeStruct((SEQ, GLA_VAL_DIM), BF16),
        scratch_shapes=[pltpu.VMEM((GLA_HEADS, GLA_HEAD_V, GLA_HEAD_K), F32)],
        compiler_params=_cparams(("arbitrary",)),
        name="gla",
    )(proj, proj, proj, proj, proj, small, w_gate, b_gate, norm_g)


OUTPROJ_TM = 512


def _outproj_kernel(a_ref, b_ref, x_ref, w_ref, g1_ref, n2_ref, sc_ref, sh_ref, wr_ref,
                    x1_ref, h2_ref, lg_ref):
    mixed = (a_ref[...].astype(F32) + b_ref[...].astype(F32)).astype(BF16)
    x1 = x_ref[...] + g1_ref[...] * _dot(mixed, w_ref[...])
    x1_ref[...] = x1
    h2 = _rms(x1, n2_ref[...]) * (1.0 + sc_ref[...]) + sh_ref[...]
    h2_ref[...] = h2
    lg_ref[...] = _dot3_nt(wr_ref[...], h2)


def _outproj(ya, yb, x2d, w_out_bf, g1, n2g, sc2, sh2, w_router_t):
    tm = OUTPROJ_TM
    vec = pl.BlockSpec((1, D_MODEL), lambda i: (0, 0))
    row = pl.BlockSpec((tm, D_MODEL), lambda i: (i, 0))
    return pl.pallas_call(
        _outproj_kernel,
        grid=(SEQ // tm,),
        in_specs=[row, row, row,
                  pl.BlockSpec((D_MODEL, D_MODEL), lambda i: (0, 0), pipeline_mode=pl.Buffered(1)),
                  vec, vec, vec, vec,
                  pl.BlockSpec((N_EXPERTS, D_MODEL), lambda i: (0, 0))],
        out_specs=[row, row, pl.BlockSpec((N_EXPERTS, tm), lambda i: (0, i))],
        out_shape=[jax.ShapeDtypeStruct((SEQ, D_MODEL), F32),
                   jax.ShapeDtypeStruct((SEQ, D_MODEL), F32),
                   jax.ShapeDtypeStruct((N_EXPERTS, SEQ), F32)],
        compiler_params=_cparams(("arbitrary",)),
        name="outproj_router",
    )(ya, yb, x2d, w_out_bf, g1, n2g, sc2, sh2, w_router_t)


ROUTE_TN = 512
COMBINE_TT = 128
N_TILES = SEQ // COMBINE_TT
TILE_LANES = 128


def _route_kernel(lg_ref, bias_ref, ek_ref, wk_ref, rk_ref, cnt_ref, cb_ref, ct_ref, carry_ref):
    tn = ROUTE_TN
    gsz = N_EXPERTS // N_EXPERT_GROUPS

    @pl.when(pl.program_id(0) == 0)
    def _():
        carry_ref[...] = jnp.zeros_like(carry_ref)
        cb_ref[...] = jnp.zeros_like(cb_ref)
        ct_ref[...] = jnp.zeros_like(ct_ref)

    s = jax.nn.sigmoid(lg_ref[...])
    choice = s + bias_ref[...]
    r8 = lax.broadcasted_iota(I32, (gsz, tn), 0)
    neg = -jnp.inf

    def top1(cur, rows, nrows):
        m = jnp.max(cur, axis=0, keepdims=True)
        idx = jnp.min(jnp.where(cur == m, rows, nrows), axis=0, keepdims=True)
        return m, idx

    gscores = []
    for g in range(N_EXPERT_GROUPS):
        blk = choice[g * gsz:(g + 1) * gsz, :]
        m1, i1 = top1(blk, r8, gsz)
        m2, _ = top1(jnp.where(r8 == i1, neg, blk), r8, gsz)
        gscores.append(m1 + m2)
    cur = jnp.concatenate(gscores, axis=0)
    gsel = jnp.zeros((N_EXPERT_GROUPS, tn), F32)
    for _ in range(TOPK_GROUPS):
        _, idx = top1(cur, r8, N_EXPERT_GROUPS)
        hit = r8 == idx
        gsel = jnp.where(hit, 1.0, gsel)
        cur = jnp.where(hit, neg, cur)
    emask = jnp.concatenate(
        [jnp.broadcast_to(gsel[g:g + 1, :], (gsz, tn)) for g in range(N_EXPERT_GROUPS)], axis=0)
    rows = lax.broadcasted_iota(I32, (N_EXPERTS, tn), 0)
    cur = jnp.where(emask > 0.5, choice, neg)
    sel = jnp.zeros((N_EXPERTS, tn), F32)
    hits, eks, wks = [], [], []
    for _ in range(TOP_K):
        _, idx = top1(cur, rows, N_EXPERTS)
        hit = rows == idx
        hits.append(hit)
        eks.append(idx)
        wks.append(jnp.sum(jnp.where(hit, s, 0.0), axis=0, keepdims=True))
        cur = jnp.where(hit, neg, cur)
        sel = jnp.where(hit, 1.0, sel)
    wsum = wks[0]
    for w in wks[1:]:
        wsum = wsum + w
    scale = ROUTED_SCALE / (wsum + 1e-20)

    ti = lax.broadcasted_iota(I32, (tn, tn), 0)
    tj = lax.broadcasted_iota(I32, (tn, tn), 1)
    before = jnp.where(ti < tj, 1.0, 0.0).astype(BF16)
    rank = _dot(sel.astype(BF16), before) + carry_ref[...]
    rks = [jnp.sum(jnp.where(hit, rank, 0.0), axis=0, keepdims=True) for hit in hits]

    lane = lax.broadcasted_iota(I32, (N_EXPERTS, TILE_LANES), 1)
    carry = carry_ref[...]
    cb = cb_ref[...]
    ct = ct_ref[...]
    for sub in range(tn // COMBINE_TT):
        cnt = jnp.sum(sel[:, sub * COMBINE_TT:(sub + 1) * COMBINE_TT], axis=1, keepdims=True)
        col = pl.program_id(0) * (tn // COMBINE_TT) + sub
        cb = jnp.where(lane == col, carry, cb)
        ct = jnp.where(lane == col, cnt, ct)
        carry = carry + cnt
    cb_ref[...] = cb
    ct_ref[...] = ct
    carry_ref[...] = carry

    ek_ref[...] = jnp.concatenate(eks, axis=0)
    wk_ref[...] = jnp.concatenate(wks, axis=0) * scale
    rk_ref[...] = jnp.concatenate(rks, axis=0).astype(I32)
    cnt_ref[...] = carry_ref[...].astype(I32)


def _route(logits_t, bias_col):
    tn = ROUTE_TN
    kt = pl.BlockSpec((TOP_K, tn), lambda i: (0, i))
    return pl.pallas_call(
        _route_kernel,
        grid=(SEQ // tn,),
        in_specs=[pl.BlockSpec((N_EXPERTS, tn), lambda i: (0, i)),
                  pl.BlockSpec((N_EXPERTS, 1), lambda i: (0, 0))],
        out_specs=[kt, kt, kt, pl.BlockSpec((N_EXPERTS, 1), lambda i: (0, 0)),
                   pl.BlockSpec((N_EXPERTS, TILE_LANES), lambda i: (0, 0)),
                   pl.BlockSpec((N_EXPERTS, TILE_LANES), lambda i: (0, 0))],
        out_shape=[jax.ShapeDtypeStruct((TOP_K, SEQ), I32),
                   jax.ShapeDtypeStruct((TOP_K, SEQ), F32),
                   jax.ShapeDtypeStruct((TOP_K, SEQ), I32),
                   jax.ShapeDtypeStruct((N_EXPERTS, 1), I32),
                   jax.ShapeDtypeStruct((N_EXPERTS, TILE_LANES), F32),
                   jax.ShapeDtypeStruct((N_EXPERTS, TILE_LANES), F32)],
        scratch_shapes=[pltpu.VMEM((N_EXPERTS, 1), F32)],
        compiler_params=_cparams(("arbitrary",)),
        name="route",
    )(logits_t, bias_col)


def _dest_kernel(ek_ref, rk_ref, ps_ref, shift_ref, o_ref, pos_ref):
    tt = COMBINE_TT
    rows = lax.broadcasted_iota(I32, (N_EXPERTS, tt), 0)
    lane = lax.broadcasted_iota(I32, (N_EXPERTS, TILE_LANES), 1)
    dests, poss = [], []
    for sub in range(ROUTE_TN // tt):
        tile = pl.program_id(0) * (ROUTE_TN // tt) + sub
        shift_col = jnp.sum(jnp.where(lane == tile, shift_ref[...], 0), axis=1, keepdims=True)
        ds, ps = [], []
        for k in range(TOP_K):
            hit = rows == ek_ref[k:k + 1, sub * tt:(sub + 1) * tt]
            d = (jnp.sum(jnp.where(hit, ps_ref[...], 0), axis=0, keepdims=True)
                 + rk_ref[k:k + 1, sub * tt:(sub + 1) * tt])
            ds.append(d)
            ps.append(d + jnp.sum(jnp.where(hit, shift_col, 0), axis=0, keepdims=True))
        dests.append(jnp.concatenate(ds, axis=0))
        poss.append(jnp.concatenate(ps, axis=0))
    o_ref[...] = jnp.concatenate(dests, axis=1)
    pos_ref[...] = jnp.concatenate(poss, axis=1)


def _dest(ek, rk, pstarts_col, shift):
    tn = ROUTE_TN
    kt = pl.BlockSpec((TOP_K, tn), lambda i: (0, i))
    return pl.pallas_call(
        _dest_kernel,
        grid=(SEQ // tn,),
        in_specs=[kt, kt, pl.BlockSpec((N_EXPERTS, 1), lambda i: (0, 0)),
                  pl.BlockSpec((N_EXPERTS, TILE_LANES), lambda i: (0, 0))],
        out_specs=[kt, kt],
        out_shape=[jax.ShapeDtypeStruct((TOP_K, SEQ), I32), jax.ShapeDtypeStruct((TOP_K, SEQ), I32)],
        compiler_params=_cparams(("arbitrary",)),
        name="dest_rows",
    )(ek, rk, pstarts_col, shift)


DISPATCH_TT = 256
PAD_BITS = tuple(1 << b for b in range(MOE_BM.bit_length() - 2, 2, -1))


def _dispatch_kernel(dest_ref, padoff_ref, padlen_ref, h_ref, out_ref, zeros_ref, sem, zsem):
    tt = DISPATCH_TT
    i = pl.program_id(0)
    base = i * tt

    def issue(t, carry):
        for k in range(TOP_K):
            d = dest_ref[k * SEQ + base + t]
            pltpu.make_async_copy(h_ref.at[pl.ds(t, 1), :], out_ref.at[pl.ds(d, 1), :], sem).start()
        return carry

    lax.fori_loop(0, tt, issue, 0)

    def pad_copy(off, n):
        return pltpu.make_async_copy(zeros_ref.at[pl.ds(0, n), :], out_ref.at[pl.ds(off, n), :], zsem)

    def pad_rows(wait):
        def body(e, carry):
            start = padoff_ref[e]
            head = (-start) & (SUBLANES - 1)
            for j in range(SUBLANES - 1):
                @pl.when(j < head)
                def _():
                    cp = pad_copy(start + j, 1)
                    cp.wait() if wait else cp.start()
            off = start + head
            n = padlen_ref[e] - head
            for bit in PAD_BITS:
                @pl.when((n & bit) != 0)
                def _():
                    cp = pad_copy(pl.multiple_of(off, SUBLANES), bit)
                    cp.wait() if wait else cp.start()
                off = off + (n & bit)
            return carry
        lax.fori_loop(0, N_EXPERTS, body, 0)

    @pl.when(i == 0)
    def _():
        zeros_ref[...] = jnp.zeros_like(zeros_ref)
        pad_rows(wait=False)
        pad_rows(wait=True)

    for _ in range(TOP_K):
        pltpu.make_async_copy(h_ref, out_ref.at[pl.ds(0, tt), :], sem).wait()


def _dispatch(dest_flat, pad_off, pad_len, h2):
    tt = DISPATCH_TT
    grid_spec = pltpu.PrefetchScalarGridSpec(
        num_scalar_prefetch=3,
        grid=(SEQ // tt,),
        in_specs=[pl.BlockSpec((tt, D_MODEL), lambda i, d, po, pn: (i, 0))],
        out_specs=pl.BlockSpec(memory_space=pl.ANY),
        scratch_shapes=[
            pltpu.VMEM((MOE_BM // 2, D_MODEL), F32),
            pltpu.SemaphoreType.DMA,
            pltpu.SemaphoreType.DMA,
        ],
    )
    return pl.pallas_call(
        _dispatch_kernel,
        grid_spec=grid_spec,
        out_shape=jax.ShapeDtypeStruct((MOE_ROWS, D_MODEL), F32),
        compiler_params=_cparams(("arbitrary",)),
        name="dispatch",
    )(dest_flat, pad_off, pad_len, h2)


def _experts_kernel(bexp_ref, first_ref, slot_ref, next_ref, nused_ref,
                    x_ref, wg_hbm, wu_hbm, wd_hbm, o_ref,
                    wgf_ref, wuf_ref, wdf_ref, wgb_ref, wub_ref, wdb_ref, sem):
    b = pl.program_id(0)
    active = b < nused_ref[0]

    def weight_copies(e, s):
        return (pltpu.make_async_copy(wg_hbm.at[e], wgf_ref.at[s], sem.at[s]),
                pltpu.make_async_copy(wu_hbm.at[e], wuf_ref.at[s], sem.at[s]),
                pltpu.make_async_copy(wd_hbm.at[e], wdf_ref.at[s], sem.at[s]))

    @pl.when(jnp.logical_and(active, b == 0))
    def _():
        for cp in weight_copies(bexp_ref[0], 0):
            cp.start()

    @pl.when(jnp.logical_and(active, first_ref[b] == 1))
    def _():
        s = slot_ref[b]
        nxt = next_ref[b]

        @pl.when(nxt < N_EXPERTS)
        def _():
            for cp in weight_copies(nxt, 1 - s):
                cp.start()

        for cp in weight_copies(bexp_ref[b], s):
            cp.wait()
        wgb_ref[...] = wgf_ref[s].astype(BF16)
        wub_ref[...] = wuf_ref[s].astype(BF16)
        wdb_ref[...] = wdf_ref[s].astype(BF16)

    @pl.when(active)
    def _():
        x = x_ref[...].astype(BF16)
        gate = _dot(x, wgb_ref[...])
        up = _dot(x, wub_ref[...])
        hid = (_silu(gate) * up).astype(BF16)
        o_ref[...] = _dot(hid, wdb_ref[...]).astype(BF16)


def _experts(block_exp, first, slot, next_exp, nused, xs_perm, w_gate, w_up, w_down):
    bm = MOE_BM

    def xmap(b, be, fi, sl, ne, nu):
        return (jnp.minimum(b, nu[0] - 1), 0)

    hbm = pl.BlockSpec(memory_space=pl.ANY)
    grid_spec = pltpu.PrefetchScalarGridSpec(
        num_scalar_prefetch=5,
        grid=(MOE_NB,),
        in_specs=[pl.BlockSpec((bm, D_MODEL), xmap), hbm, hbm, hbm],
        out_specs=pl.BlockSpec((bm, D_MODEL), xmap),
        scratch_shapes=[
            pltpu.VMEM((2, D_MODEL, EXPERT_DIM), F32),
            pltpu.VMEM((2, D_MODEL, EXPERT_DIM), F32),
            pltpu.VMEM((2, EXPERT_DIM, D_MODEL), F32),
            pltpu.VMEM((D_MODEL, EXPERT_DIM), BF16),
            pltpu.VMEM((D_MODEL, EXPERT_DIM), BF16),
            pltpu.VMEM((EXPERT_DIM, D_MODEL), BF16),
            pltpu.SemaphoreType.DMA((2,)),
        ],
    )
    return pl.pallas_call(
        _experts_kernel,
        grid_spec=grid_spec,
        out_shape=jax.ShapeDtypeStruct((MOE_ROWS, D_MODEL), BF16),
        compiler_params=_cparams(("arbitrary",)),
        name="experts",
    )(block_exp, first, slot, next_exp, nused, xs_perm, w_gate, w_up, w_down)


SHARED_TM = 512


def _shared_kernel(h2_ref, wg_ref, wu_ref, wd_ref, o_ref):
    h = h2_ref[...].astype(BF16)
    hid = (_silu(_dot(h, wg_ref[...])) * _dot(h, wu_ref[...])).astype(BF16)
    o_ref[...] = _dot(hid, wd_ref[...])


def _shared(h2, wsg, wsu, wsd):
    tm = SHARED_TM
    row = pl.BlockSpec((tm, D_MODEL), lambda i: (i, 0))
    return pl.pallas_call(
        _shared_kernel,
        grid=(SEQ // tm,),
        in_specs=[row,
                  pl.BlockSpec((D_MODEL, EXPERT_DIM), lambda i: (0, 0)),
                  pl.BlockSpec((D_MODEL, EXPERT_DIM), lambda i: (0, 0)),
                  pl.BlockSpec((EXPERT_DIM, D_MODEL), lambda i: (0, 0))],
        out_specs=row,
        out_shape=jax.ShapeDtypeStruct((SEQ, D_MODEL), F32),
        compiler_params=_cparams(("arbitrary",)),
        name="shared_expert",
    )(h2, wsg, wsu, wsd)


COMBINE_ALIGN = 16
COMBINE_CHUNK = 512
COMBINE_ROWS = pl.cdiv(COMBINE_TT * TOP_K + 2 * (COMBINE_ALIGN - 1) * N_EXPERTS,
                       COMBINE_CHUNK) * COMBINE_CHUNK
COMBINE_MIN_CHUNKS = pl.cdiv(COMBINE_TT * TOP_K + (COMBINE_ALIGN - 1) * N_EXPERTS, COMBINE_CHUNK)


def _combine_kernel(ws_ref, wl_ref, so_ref, ysh_ref, x1_ref, wk_ref, pos_ref, g2_ref, nf_ref, eo_ref,
                    o_ref, buf_ref, acc_ref, sem):
    tt = COMBINE_TT
    i = pl.program_id(0)
    slot = i % 2

    def start_windows(tile, slot_):
        def body(e, carry):
            idx = tile * N_EXPERTS + e
            n = pl.multiple_of(wl_ref[idx], COMBINE_ALIGN)
            src = pl.multiple_of(ws_ref[idx], COMBINE_ALIGN)
            dst = pl.multiple_of(slot_ * COMBINE_ROWS + so_ref[idx], COMBINE_ALIGN)

            @pl.when(n > 0)
            def _():
                pltpu.make_async_copy(eo_ref.at[pl.ds(src, n), :], buf_ref.at[pl.ds(dst, n), :],
                                      sem.at[slot_]).start()
            return carry

        lax.fori_loop(0, N_EXPERTS, body, 0)

    @pl.when(i == 0)
    def _():
        buf_ref[...] = jnp.zeros_like(buf_ref)
        start_windows(0, 0)

    @pl.when(i + 1 < pl.num_programs(0))
    def _():
        start_windows(i + 1, 1 - slot)

    last = i * N_EXPERTS + N_EXPERTS - 1
    staged = pl.multiple_of(so_ref[last] + wl_ref[last], COMBINE_ALIGN)
    sbase = pl.multiple_of(slot * COMBINE_ROWS, COMBINE_ROWS)
    pltpu.make_async_copy(eo_ref.at[pl.ds(0, staged), :], buf_ref.at[pl.ds(sbase, staged), :],
                          sem.at[slot]).wait()
    wk = wk_ref[...]
    pos = pos_ref[...]
    def routed(c):
        col = lax.broadcasted_iota(I32, (tt, COMBINE_CHUNK), 1) + c * COMBINE_CHUNK
        w = jnp.zeros((tt, COMBINE_CHUNK), F32)
        for k in range(TOP_K):
            w = w + jnp.where(col == pos[:, k:k + 1], wk[:, k:k + 1], 0.0)
        w_hi = w.astype(BF16)
        w_lo = (w - w_hi.astype(F32)).astype(BF16)
        rows = buf_ref[pl.ds(sbase + c * COMBINE_CHUNK, COMBINE_CHUNK), :]
        both = _dot(jnp.concatenate([w_hi, w_lo], axis=0), rows)
        return both[:tt] + both[tt:]

    acc = ysh_ref[...]
    for c in range(COMBINE_MIN_CHUNKS):
        acc = acc + routed(c)
    acc_ref[...] = acc
    for c in range(COMBINE_MIN_CHUNKS, COMBINE_ROWS // COMBINE_CHUNK):
        @pl.when(c * COMBINE_CHUNK < staged)
        def _():
            acc_ref[...] += routed(c)
    x2 = x1_ref[...] + g2_ref[...] * acc_ref[...]
    o_ref[...] = _rms(x2, nf_ref[...])


def _combine(win_start, win_len, stage_off, y_shared, x1, wk_t, pos_t, g2, nf, eo):
    tt = COMBINE_TT
    vec = pl.BlockSpec((1, D_MODEL), lambda i, a, b, c: (0, 0))
    row = pl.BlockSpec((tt, D_MODEL), lambda i, a, b, c: (i, 0))
    slots = pl.BlockSpec((tt, TOP_K), lambda i, a, b, c: (i, 0))
    grid_spec = pltpu.PrefetchScalarGridSpec(
        num_scalar_prefetch=3,
        grid=(N_TILES,),
        in_specs=[row, row, slots, slots, vec, vec, pl.BlockSpec(memory_space=pl.ANY)],
        out_specs=row,
        scratch_shapes=[
            pltpu.VMEM((2 * COMBINE_ROWS, D_MODEL), BF16),
            pltpu.VMEM((tt, D_MODEL), F32),
            pltpu.SemaphoreType.DMA((2,)),
        ],
    )
    return pl.pallas_call(
        _combine_kernel,
        grid_spec=grid_spec,
        out_shape=jax.ShapeDtypeStruct((SEQ, D_MODEL), F32),
        compiler_params=_cparams(("arbitrary",)),
        name="combine_final",
    )(win_start, win_len, stage_off, y_shared, x1, wk_t, pos_t, g2, nf, eo)


def kernel(x, c, w_ada, b_ada, norm1_g, w_in, conv_w, conv_b, dt_bias, a_log, d_skip, ssd_norm_g, gla_w_gate, gla_b_gate, gla_norm_g, w_out, norm2_g, w_router, router_bias, w_e_gate, w_e_up, w_e_down, w_s_gate, w_s_up, w_s_down, normf_g):
    layer = 0
    x2d = x.reshape(SEQ, D_MODEL)
    mod = _ada(c.reshape(D_MODEL, 1), w_ada[layer], b_ada[layer].reshape(1, -1))
    sh1, sc1, g1, sh2, sc2, g2 = [mod[:, i * D_MODEL:(i + 1) * D_MODEL] for i in range(6)]

    wi = w_in[layer]
    o_dt, o_q, o_g, o_ma = 5120, 5152, 11296, 11312
    w_big = _pack_w_in(wi)
    w_small = jnp.concatenate([
        wi[:, o_dt:o_q], wi[:, o_g:o_ma],
        jnp.zeros((D_MODEL, SMALL_COLS - SSD_HEADS - GLA_GATE_RANK), F32)], axis=1)

    proj, small = _inproj(x2d, norm1_g[layer].reshape(1, -1), sc1, sh1, w_big, w_small)

    d_exp = jnp.repeat(d_skip[layer], SSD_HEAD_DIM).reshape(1, -1)
    y_a = _ssd(proj, small, conv_w[layer], conv_b[layer].reshape(1, -1),
               dt_bias[layer].reshape(1, -1), a_log[layer].reshape(1, -1), d_exp,
               ssd_norm_g[layer].reshape(1, -1))
    y_b = _gla(proj, small, gla_w_gate[layer], gla_b_gate[layer].reshape(1, -1),
               gla_norm_g[layer].reshape(1, -1))

    x1, h2, logits_t = _outproj(y_a, y_b, x2d, w_out[layer].astype(BF16), g1,
                                norm2_g[layer].reshape(1, -1), sc2, sh2, w_router[layer].T)

    ek, wk, rk, counts, cbf, ctf = _route(logits_t, router_bias[layer].reshape(-1, 1))

    counts = counts.reshape(-1)
    padded = (counts + MOE_BM - 1) // MOE_BM * MOE_BM
    pends = jnp.cumsum(padded)
    pstarts = pends - padded
    block_start = jnp.arange(MOE_NB, dtype=I32) * MOE_BM
    block_exp = jnp.minimum(
        jnp.sum((pends[None, :] <= block_start[:, None]).astype(I32), axis=1), N_EXPERTS - 1)
    nused = (pends[-1:] // MOE_BM).astype(I32)
    bidx = jnp.arange(MOE_NB, dtype=I32)
    prev_exp = jnp.concatenate([jnp.full((1,), -1, I32), block_exp[:-1]])
    first = jnp.logical_and(bidx < nused[0], block_exp != prev_exp)
    slot = (jnp.cumsum(first.astype(I32)) - 1) & 1
    later_first = jnp.logical_and(first[None, :], bidx[None, :] > bidx[:, None])
    next_first = jnp.min(jnp.where(later_first, bidx[None, :], MOE_NB), axis=1)
    next_exp = jnp.sum(jnp.where(bidx[None, :] == next_first[:, None], block_exp[None, :], 0), axis=1)
    next_exp = jnp.where(next_first < MOE_NB, next_exp, N_EXPERTS).astype(I32)

    cb = cbf[:, :N_TILES].astype(I32)
    ct = ctf[:, :N_TILES].astype(I32)
    run_start = pstarts[:, None].astype(I32) + cb
    win_start = run_start & -COMBINE_ALIGN
    win_end = (run_start + ct + COMBINE_ALIGN - 1) & -COMBINE_ALIGN
    win_len = jnp.where(ct > 0, win_end - win_start, 0)
    stage_off = jnp.cumsum(win_len, axis=0) - win_len
    shift = jnp.pad(stage_off - win_start, ((0, 0), (0, TILE_LANES - N_TILES)))
    tile_major = lambda a: a.T.reshape(-1).astype(I32)

    dest, pos = _dest(ek, rk, pstarts.reshape(-1, 1).astype(I32), shift)
    xs_perm = _dispatch(dest.reshape(-1), (pstarts + counts).astype(I32), (padded - counts).astype(I32), h2)
    eo = _experts(block_exp.astype(I32), first.astype(I32), slot.astype(I32), next_exp, nused, xs_perm,
                  w_e_gate[layer], w_e_up[layer], w_e_down[layer])
    y_shared = _shared(h2, w_s_gate[layer].astype(BF16), w_s_up[layer].astype(BF16),
                       w_s_down[layer].astype(BF16))
    out = _combine(tile_major(win_start), tile_major(win_len), tile_major(stage_off), y_shared, x1,
                   wk.T, pos.T, g2, normf_g.reshape(1, -1), eo)
    return out.reshape(x.shape)
```

```python
import functools

import jax
import jax.numpy as jnp
from jax import lax
from jax.experimental import pallas as pl
from jax.experimental.pallas import tpu as pltpu

F32 = jnp.float32
BF16 = jnp.bfloat16
I32 = jnp.int32

D_MODEL = 2048
SEQ = 8192
NORM_EPS = 1e-6
SSD_HEADS = 32
SSD_HEAD_DIM = 64
SSD_INNER = 2048
SSD_GROUPS = 4
SSD_STATE = 128
SSD_CHUNK = 128
HEADS_PER_GROUP = SSD_HEADS // SSD_GROUPS
GROUP_COLS = HEADS_PER_GROUP * SSD_HEAD_DIM
GLA_HEADS = 4
GLA_KEY_DIM = 1024
GLA_VAL_DIM = 2048
GLA_HEAD_K = 256
GLA_HEAD_V = 512
GLA_GATE_RANK = 16
GLA_GATE_NORM = 16.0
GLA_CHUNK = 64
GLA_STEP = 128
N_EXPERTS = 64
TOP_K = 8
N_EXPERT_GROUPS = 8
TOPK_GROUPS = 4
EXPERT_DIM = 512
ROUTED_SCALE = 2.5

VMEM_LIMIT_BYTES = 56 * 1024 * 1024
SUBLANES = 8

COL_Z, COL_XS, COL_BC, COL_Q, COL_V, COL_R, COL_MA, COL_MB, COL_K = (
    0, 2048, 4096, 5120, 6144, 8192, 10240, 12288, 14336)
PROJ_COLS = 15360
SMALL_COLS = 128

MOE_BM = 256
MOE_NB = SEQ * TOP_K // MOE_BM + N_EXPERTS
MOE_ROWS = MOE_NB * MOE_BM


def _cparams(sem):
    return pltpu.CompilerParams(dimension_semantics=sem, vmem_limit_bytes=VMEM_LIMIT_BYTES)


def _split3(a):
    hi = a.astype(BF16)
    r1 = a - hi.astype(F32)
    mid = r1.astype(BF16)
    lo = (r1 - mid.astype(F32)).astype(BF16)
    return hi, mid, lo


def _dot(a, b):
    return jnp.dot(a, b, preferred_element_type=F32)


def _dot_nt(a, b):
    return lax.dot_general(a, b, (((1,), (1,)), ((), ())), preferred_element_type=F32)


def _dot_tn(a, b):
    return lax.dot_general(a, b, (((0,), (0,)), ((), ())), preferred_element_type=F32)


def _sel_dot(sel_bf16, a):
    hi, mid, lo = _split3(a)
    return _dot(sel_bf16, hi) + _dot(sel_bf16, mid) + _dot(sel_bf16, lo)


def _dot_sel(a, sel_bf16):
    hi, mid, lo = _split3(a)
    return _dot(hi, sel_bf16) + _dot(mid, sel_bf16) + _dot(lo, sel_bf16)


def _dot3(a, b):
    ah = a.astype(BF16)
    al = (a - ah.astype(F32)).astype(BF16)
    bh = b.astype(BF16)
    bl = (b - bh.astype(F32)).astype(BF16)
    return _dot(ah, bh) + _dot(ah, bl) + _dot(al, bh)


def _dot3_nt(a, b):
    ah = a.astype(BF16)
    al = (a - ah.astype(F32)).astype(BF16)
    bh = b.astype(BF16)
    bl = (b - bh.astype(F32)).astype(BF16)
    return _dot_nt(ah, bh) + _dot_nt(ah, bl) + _dot_nt(al, bh)


def _silu(x):
    return x * jax.nn.sigmoid(x)


def _softplus(x):
    return jnp.maximum(x, 0.0) + jnp.log1p(jnp.exp(-jnp.abs(x)))


def _rms(x, g):
    ms = jnp.mean(x * x, axis=-1, keepdims=True)
    return x * lax.rsqrt(ms + NORM_EPS) * g


ADA_TN = 1024


def _ada_kernel(c_ref, w_ref, b_ref, o_ref):
    ca = _silu(c_ref[...])
    o_ref[...] = jnp.sum(w_ref[...] * ca, axis=0, keepdims=True) + b_ref[...]


def _ada(c_col, w_ada, b_ada):
    n = w_ada.shape[1]
    return pl.pallas_call(
        _ada_kernel,
        grid=(n // ADA_TN,),
        in_specs=[
            pl.BlockSpec((D_MODEL, 1), lambda j: (0, 0)),
            pl.BlockSpec((D_MODEL, ADA_TN), lambda j: (0, j)),
            pl.BlockSpec((1, ADA_TN), lambda j: (0, j)),
        ],
        out_specs=pl.BlockSpec((1, ADA_TN), lambda j: (0, j)),
        out_shape=jax.ShapeDtypeStruct((1, n), F32),
        compiler_params=_cparams(("arbitrary",)),
        name="ada_mod",
    )(c_col, w_ada, b_ada)


IN_DIM = 15408
PACK_TN = 1024
W_IN_SEGMENTS = ((0, 5120), (5152, 1024), (7200, 4096), (11312, 4096), (6176, 1024))
W_IN_DT_COL = 5120
W_IN_GLR_COL = 11296
PACK_STARTS = tuple(s + t * PACK_TN for s, w in W_IN_SEGMENTS for t in range(w // PACK_TN))


def _pack_kernel(start_ref, dt_ref, glr_ref, wt_hbm, o_ref, small_ref, win_ref, sem):
    j = pl.program_id(0)
    slot = j % 2

    @pl.when(j == 0)
    def _():
        small_ref[...] = jnp.zeros_like(small_ref)
        small_ref[0:SSD_HEADS, :] = dt_ref[...]
        small_ref[SSD_HEADS:SSD_HEADS + GLA_GATE_RANK, :] = glr_ref[...]

    def window_copy(t, s):
        row = pl.multiple_of(start_ref[t], 2 * SUBLANES)
        return pltpu.make_async_copy(wt_hbm.at[pl.ds(row, PACK_TN), :], win_ref.at[s], sem.at[s])

    @pl.when(j == 0)
    def _():
        window_copy(0, 0).start()

    @pl.when(j + 1 < pl.num_programs(0))
    def _():
        window_copy(j + 1, 1 - slot).start()

    window_copy(j, slot).wait()
    o_ref[...] = win_ref[slot].T.astype(BF16)


def _pack_w_in(w_in_t):
    starts = jnp.asarray(PACK_STARTS, I32)
    grid_spec = pltpu.PrefetchScalarGridSpec(
        num_scalar_prefetch=1,
        grid=(len(PACK_STARTS),),
        in_specs=[pl.BlockSpec((SSD_HEADS, D_MODEL), lambda j, s: (W_IN_DT_COL // SSD_HEADS, 0)),
                  pl.BlockSpec((GLA_GATE_RANK, D_MODEL), lambda j, s: (W_IN_GLR_COL // GLA_GATE_RANK, 0)),
                  pl.BlockSpec(memory_space=pl.ANY)],
        out_specs=[pl.BlockSpec((D_MODEL, PACK_TN), lambda j, s: (0, j)),
                   pl.BlockSpec((SMALL_COLS, D_MODEL), lambda j, s: (0, 0))],
        scratch_shapes=[pltpu.VMEM((2, PACK_TN, D_MODEL), F32), pltpu.SemaphoreType.DMA((2,))],
    )
    return pl.pallas_call(
        _pack_kernel,
        grid_spec=grid_spec,
        out_shape=[jax.ShapeDtypeStruct((D_MODEL, PROJ_COLS), BF16),
                   jax.ShapeDtypeStruct((SMALL_COLS, D_MODEL), F32)],
        compiler_params=_cparams(("arbitrary",)),
        name="pack_w_in",
    )(starts, w_in_t, w_in_t, w_in_t)


INPROJ_TM = 1024
INPROJ_TN = 1024


def _inproj_kernel(x_ref, g_ref, sc_ref, sh_ref, wb_ref, ws_ref, o_ref, os_ref, h_ref):
    @pl.when(pl.program_id(1) == 0)
    def _():
        h = _rms(x_ref[...], g_ref[...]) * (1.0 + sc_ref[...]) + sh_ref[...]
        hb = h.astype(BF16)
        h_ref[...] = hb
        os_ref[...] = _dot_nt(hb, ws_ref[...].astype(BF16))

    o_ref[...] = _dot(h_ref[...], wb_ref[...]).astype(BF16)


def _inproj(x2d, g, sc, sh, w_big, w_small):
    vec = pl.BlockSpec((1, D_MODEL), lambda i, j: (0, 0))
    return pl.pallas_call(
        _inproj_kernel,
        grid=(SEQ // INPROJ_TM, PROJ_COLS // INPROJ_TN),
        in_specs=[
            pl.BlockSpec((INPROJ_TM, D_MODEL), lambda i, j: (i, 0)),
            vec, vec, vec,
            pl.BlockSpec((D_MODEL, INPROJ_TN), lambda i, j: (0, j)),
            pl.BlockSpec((SMALL_COLS, D_MODEL), lambda i, j: (0, 0)),
        ],
        out_specs=[
            pl.BlockSpec((INPROJ_TM, INPROJ_TN), lambda i, j: (i, j)),
            pl.BlockSpec((INPROJ_TM, SMALL_COLS), lambda i, j: (i, 0)),
        ],
        out_shape=[
            jax.ShapeDtypeStruct((SEQ, PROJ_COLS), BF16),
            jax.ShapeDtypeStruct((SEQ, SMALL_COLS), F32),
        ],
        scratch_shapes=[pltpu.VMEM((INPROJ_TM, D_MODEL), BF16)],
        compiler_params=_cparams(("arbitrary", "arbitrary")),
        name="inproj",
    )(x2d, g, sc, sh, w_big, w_small)


def _causal_conv_silu(cur, prev, w_ref, b_ref):
    row = lax.broadcasted_iota(I32, cur.shape, 0)
    acc = cur * w_ref[3:4, :] + b_ref[...]
    for s in (1, 2, 3):
        shifted = jnp.where(row >= s, pltpu.roll(cur, s, 0), pltpu.roll(prev, s, 0))
        acc = acc + shifted * w_ref[3 - s:4 - s, :]
    return _silu(acc)


def _ssd_kernel(xs_ref, bc_ref, z_ref, ma_ref, sm_ref, cwx_ref, cwb_ref, cbx_ref, cbb_ref,
                dtb_ref, alog_ref, dexp_ref, ng_ref, o_ref, px_ref, pb_ref, st_ref):
    L = SSD_CHUNK

    @pl.when(pl.program_id(0) == 0)
    def _():
        px_ref[...] = jnp.zeros_like(px_ref)
        pb_ref[...] = jnp.zeros_like(pb_ref)
        st_ref[...] = jnp.zeros_like(st_ref)

    xs_raw = xs_ref[...].astype(F32)
    bc_raw = bc_ref[...].astype(F32)
    xs = _causal_conv_silu(xs_raw, px_ref[...], cwx_ref, cbx_ref)
    bc = _causal_conv_silu(bc_raw, pb_ref[...], cwb_ref, cbb_ref)
    px_ref[...] = xs_raw
    pb_ref[...] = bc_raw

    dt = _softplus(sm_ref[:, 0:SSD_HEADS] + dtb_ref[...])
    d_a = dt * (-jnp.exp(alog_ref[...]))
    ri = lax.broadcasted_iota(I32, (L, L), 0)
    ci = lax.broadcasted_iota(I32, (L, L), 1)
    causal = ri >= ci
    tril = jnp.where(causal, 1.0, 0.0).astype(BF16)
    acum = _sel_dot(tril, d_a)
    hi32 = lax.broadcasted_iota(I32, (SSD_HEADS, SSD_HEADS), 0)
    hj32 = lax.broadcasted_iota(I32, (SSD_HEADS, SSD_HEADS), 1)
    eye32 = jnp.where(hi32 == hj32, 1.0, 0.0).astype(BF16)
    a_hi, a_mid, a_lo = _split3(acum)
    acum_t = _dot_nt(eye32, a_hi) + _dot_nt(eye32, a_mid) + _dot_nt(eye32, a_lo)
    last = acum[L - 1:L, :]
    e_a = jnp.exp(acum)
    to_end = jnp.exp(last - acum)
    eh = lax.broadcasted_iota(I32, (SSD_HEADS, SSD_INNER), 0)
    ec = lax.broadcasted_iota(I32, (SSD_HEADS, SSD_INNER), 1)
    expand = jnp.where((ec >> 6) == eh, 1.0, 0.0).astype(BF16)
    dt_x = _dot_sel(dt, expand)
    dte_x = _dot_sel(dt * to_end, expand)
    ea_x = _dot_sel(e_a, expand)
    cd_x = ea_x[L - 1:L, :]

    xdt = xs * dt_x
    lane = lax.broadcasted_iota(I32, xdt.shape, 1)
    left = (lane & 64) == 0
    xdt_l = jnp.where(left, xdt, 0.0).astype(BF16)
    xdt_r = jnp.where(left, 0.0, xdt).astype(BF16)
    xdte = (xs * dte_x).astype(BF16)

    y_diag = []
    y_off = []
    for g in range(SSD_GROUPS):
        bg = bc[:, g * SSD_STATE:(g + 1) * SSD_STATE].astype(BF16)
        cg = bc[:, 512 + g * SSD_STATE:512 + (g + 1) * SSD_STATE].astype(BF16)
        cb = _dot_nt(cg, bg)
        gsl = slice(g * GROUP_COLS, (g + 1) * GROUP_COLS)
        st = st_ref[g]
        y_off.append(_dot(cg, st.astype(BF16)))
        st_ref[g] = st * cd_x[:, gsl] + _dot_tn(bg, xdte[:, gsl])
        for p in range(HEADS_PER_GROUP // 2):
            ms = []
            for h in (g * HEADS_PER_GROUP + 2 * p, g * HEADS_PER_GROUP + 2 * p + 1):
                seg = acum[:, h:h + 1] - acum_t[h:h + 1, :]
                decay = jnp.exp(jnp.where(causal, seg, -jnp.inf))
                ms.append((cb * decay).astype(BF16))
            m_cat = jnp.concatenate(ms, axis=1)
            psl = slice((g * 4 + p) * 128, (g * 4 + p + 1) * 128)
            x_cat = jnp.concatenate([xdt_l[:, psl], xdt_r[:, psl]], axis=0)
            y_diag.append(_dot(m_cat, x_cat))

    y = (jnp.concatenate(y_diag, axis=1) + jnp.concatenate(y_off, axis=1) * ea_x
         + dexp_ref[...] * xs)
    yf = y * _silu(z_ref[...].astype(F32))
    outs = []
    for g in range(SSD_GROUPS):
        seg = yf[:, g * GROUP_COLS:(g + 1) * GROUP_COLS]
        ms = jnp.mean(seg * seg, axis=-1, keepdims=True)
        outs.append(seg * lax.rsqrt(ms + NORM_EPS))
    y_ssd = jnp.concatenate(outs, axis=1) * ng_ref[...]
    o_ref[...] = (jax.nn.sigmoid(ma_ref[...].astype(F32)) * y_ssd).astype(BF16)


def _ssd(proj, small, conv_w, conv_b, dt_bias, a_log, d_exp, norm_g):
    L = SSD_CHUNK
    full = lambda shape: pl.BlockSpec(shape, lambda i: (0, 0))
    return pl.pallas_call(
        _ssd_kernel,
        grid=(SEQ // L,),
        in_specs=[
            pl.BlockSpec((L, 2048), lambda i: (i, COL_XS // 2048)),
            pl.BlockSpec((L, 1024), lambda i: (i, COL_BC // 1024)),
            pl.BlockSpec((L, 2048), lambda i: (i, COL_Z // 2048)),
            pl.BlockSpec((L, 2048), lambda i: (i, COL_MA // 2048)),
            pl.BlockSpec((L, SMALL_COLS), lambda i: (i, 0)),
            pl.BlockSpec((4, 2048), lambda i: (0, 0)),
            pl.BlockSpec((4, 1024), lambda i: (0, 2)),
            pl.BlockSpec((1, 2048), lambda i: (0, 0)),
            pl.BlockSpec((1, 1024), lambda i: (0, 2)),
            full((1, SSD_HEADS)), full((1, SSD_HEADS)),
            full((1, SSD_INNER)), full((1, SSD_INNER)),
        ],
        out_specs=pl.BlockSpec((L, SSD_INNER), lambda i: (i, 0)),
        out_shape=jax.ShapeDtypeStruct((SEQ, SSD_INNER), BF16),
        scratch_shapes=[
            pltpu.VMEM((L, 2048), F32),
            pltpu.VMEM((L, 1024), F32),
            pltpu.VMEM((SSD_GROUPS, SSD_STATE, GROUP_COLS), F32),
        ],
        compiler_params=_cparams(("arbitrary",)),
        name="ssd",
    )(proj, proj, proj, proj, small, conv_w, conv_w, conv_b, conv_b, dt_bias, a_log, d_exp, norm_g)


def _gla_kernel(q_ref, k_ref, v_ref, r_ref, mb_ref, sm_ref, wg_ref, bg_ref, ng_ref, o_ref, s_ref):
    L = GLA_CHUNK

    @pl.when(pl.program_id(0) == 0)
    def _():
        s_ref[...] = jnp.zeros_like(s_ref)

    ri = lax.broadcasted_iota(I32, (L, L), 0)
    ci = lax.broadcasted_iota(I32, (L, L), 1)
    causal = ri >= ci
    tril = jnp.where(causal, 1.0, 0.0).astype(BF16)

    for c in range(GLA_STEP // L):
        rs = slice(c * L, (c + 1) * L)
        glr = sm_ref[rs, SSD_HEADS:SSD_HEADS + GLA_GATE_RANK]
        pre = _dot3(glr, wg_ref[...]) + bg_ref[...]
        gk = (jnp.minimum(pre, 0.0) - jnp.log1p(jnp.exp(-jnp.abs(pre)))) / GLA_GATE_NORM
        bcum = _sel_dot(tril, gk)
        bmid = bcum[L // 2:L // 2 + 1, :]
        blast = bcum[L - 1:L, :]
        q = q_ref[rs, :].astype(F32) * (GLA_HEAD_K ** -0.5)
        k = k_ref[rs, :].astype(F32)
        q_rel = (q * jnp.exp(bcum - bmid)).astype(BF16)
        k_rel = (k * jnp.exp(bmid - bcum)).astype(BF16)
        q_int = (q * jnp.exp(bcum)).astype(BF16)
        k_end = (k * jnp.exp(blast - bcum)).astype(BF16)
        dec = jnp.exp(blast)
        v = v_ref[rs, :]

        outs = []
        for h in range(GLA_HEADS):
            ks = slice(h * GLA_HEAD_K, (h + 1) * GLA_HEAD_K)
            vs = slice(h * GLA_HEAD_V, (h + 1) * GLA_HEAD_V)
            att = jnp.where(causal, _dot_nt(q_rel[:, ks], k_rel[:, ks]), 0.0)
            s_t = s_ref[h]
            o_h = _dot(att.astype(BF16), v[:, vs]) + _dot_nt(q_int[:, ks], s_t.astype(BF16))
            s_ref[h] = s_t * dec[:, ks] + _dot_tn(v[:, vs], k_end[:, ks])
            outs.append(_rms(o_h, ng_ref[...]))
        o = jnp.concatenate(outs, axis=1)
        o = o * _silu(r_ref[rs, :].astype(F32)) * jax.nn.sigmoid(mb_ref[rs, :].astype(F32))
        o_ref[rs, :] = o.astype(BF16)


def _gla(proj, small, w_gate, b_gate, norm_g):
    L = GLA_STEP
    full = lambda shape: pl.BlockSpec(shape, lambda i: (0, 0))
    return pl.pallas_call(
        _gla_kernel,
        grid=(SEQ // L,),
        in_specs=[
            pl.BlockSpec((L, 1024), lambda i: (i, COL_Q // 1024)),
            pl.BlockSpec((L, 1024), lambda i: (i, COL_K // 1024)),
            pl.BlockSpec((L, 2048), lambda i: (i, COL_V // 2048)),
            pl.BlockSpec((L, 2048), lambda i: (i, COL_R // 2048)),
            pl.BlockSpec((L, 2048), lambda i: (i, COL_MB // 2048)),
            pl.BlockSpec((L, SMALL_COLS), lambda i: (i, 0)),
            full((GLA_GATE_RANK, GLA_KEY_DIM)), full((1, GLA_KEY_DIM)), full((1, GLA_HEAD_V)),
        ],
        out_specs=pl.BlockSpec((L, GLA_VAL_DIM), lambda i: (i, 0)),
        out_shape=jax.ShapeDtypeStruct((SEQ, GLA_VAL_DIM), BF16),
        scratch_shapes=[pltpu.VMEM((GLA_HEADS, GLA_HEAD_V, GLA_HEAD_K), F32)],
        compiler_params=_cparams(("arbitrary",)),
        name="gla",
    )(proj, proj, proj, proj, proj, small, w_gate, b_gate, norm_g)


OUTPROJ_TM = 512


def _outproj_kernel(a_ref, b_ref, x_ref, w_ref, g1_ref, n2_ref, sc_ref, sh_ref, wr_ref,
                    x1_ref, h2_ref, lg_ref):
    mixed = (a_ref[...].astype(F32) + b_ref[...].astype(F32)).astype(BF16)
    x1 = x_ref[...] + g1_ref[...] * _dot(mixed, w_ref[...])
    x1_ref[...] = x1
    h2 = _rms(x1, n2_ref[...]) * (1.0 + sc_ref[...]) + sh_ref[...]
    h2_ref[...] = h2
    lg_ref[...] = _dot3_nt(wr_ref[...], h2)


def _outproj(ya, yb, x2d, w_out_bf, g1, n2g, sc2, sh2, w_router_t):
    tm = OUTPROJ_TM
    vec = pl.BlockSpec((1, D_MODEL), lambda i: (0, 0))
    row = pl.BlockSpec((tm, D_MODEL), lambda i: (i, 0))
    return pl.pallas_call(
        _outproj_kernel,
        grid=(SEQ // tm,),
        in_specs=[row, row, row,
                  pl.BlockSpec((D_MODEL, D_MODEL), lambda i: (0, 0), pipeline_mode=pl.Buffered(1)),
                  vec, vec, vec, vec,
                  pl.BlockSpec((N_EXPERTS, D_MODEL), lambda i: (0, 0))],
        out_specs=[row, row, pl.BlockSpec((N_EXPERTS, tm), lambda i: (0, i))],
        out_shape=[jax.ShapeDtypeStruct((SEQ, D_MODEL), F32),
                   jax.ShapeDtypeStruct((SEQ, D_MODEL), F32),
                   jax.ShapeDtypeStruct((N_EXPERTS, SEQ), F32)],
        compiler_params=_cparams(("arbitrary",)),
        name="outproj_router",
    )(ya, yb, x2d, w_out_bf, g1, n2g, sc2, sh2, w_router_t)


ROUTE_TN = 512
COMBINE_TT = 128
N_TILES = SEQ // COMBINE_TT
TILE_LANES = 128


def _route_kernel(lg_ref, bias_ref, ek_ref, wk_ref, rk_ref, cnt_ref, cb_ref, ct_ref, carry_ref):
    tn = ROUTE_TN
    gsz = N_EXPERTS // N_EXPERT_GROUPS

    @pl.when(pl.program_id(0) == 0)
    def _():
        carry_ref[...] = jnp.zeros_like(carry_ref)
        cb_ref[...] = jnp.zeros_like(cb_ref)
        ct_ref[...] = jnp.zeros_like(ct_ref)

    s = jax.nn.sigmoid(lg_ref[...])
    choice = s + bias_ref[...]
    r8 = lax.broadcasted_iota(I32, (gsz, tn), 0)
    neg = -jnp.inf

    def top1(cur, rows, nrows):
        m = jnp.max(cur, axis=0, keepdims=True)
        idx = jnp.min(jnp.where(cur == m, rows, nrows), axis=0, keepdims=True)
        return m, idx

    gscores = []
    for g in range(N_EXPERT_GROUPS):
        blk = choice[g * gsz:(g + 1) * gsz, :]
        m1, i1 = top1(blk, r8, gsz)
        m2, _ = top1(jnp.where(r8 == i1, neg, blk), r8, gsz)
        gscores.append(m1 + m2)
    cur = jnp.concatenate(gscores, axis=0)
    gsel = jnp.zeros((N_EXPERT_GROUPS, tn), F32)
    for _ in range(TOPK_GROUPS):
        _, idx = top1(cur, r8, N_EXPERT_GROUPS)
        hit = r8 == idx
        gsel = jnp.where(hit, 1.0, gsel)
        cur = jnp.where(hit, neg, cur)
    emask = jnp.concatenate(
        [jnp.broadcast_to(gsel[g:g + 1, :], (gsz, tn)) for g in range(N_EXPERT_GROUPS)], axis=0)
    rows = lax.broadcasted_iota(I32, (N_EXPERTS, tn), 0)
    cur = jnp.where(emask > 0.5, choice, neg)
    sel = jnp.zeros((N_EXPERTS, tn), F32)
    hits, eks, wks = [], [], []
    for _ in range(TOP_K):
        _, idx = top1(cur, rows, N_EXPERTS)
        hit = rows == idx
        hits.append(hit)
        eks.append(idx)
        wks.append(jnp.sum(jnp.where(hit, s, 0.0), axis=0, keepdims=True))
        cur = jnp.where(hit, neg, cur)
        sel = jnp.where(hit, 1.0, sel)
    wsum = wks[0]
    for w in wks[1:]:
        wsum = wsum + w
    scale = ROUTED_SCALE / (wsum + 1e-20)

    ti = lax.broadcasted_iota(I32, (tn, tn), 0)
    tj = lax.broadcasted_iota(I32, (tn, tn), 1)
    before = jnp.where(ti < tj, 1.0, 0.0).astype(BF16)
    rank = _dot(sel.astype(BF16), before) + carry_ref[...]
    rks = [jnp.sum(jnp.where(hit, rank, 0.0), axis=0, keepdims=True) for hit in hits]

    lane = lax.broadcasted_iota(I32, (N_EXPERTS, TILE_LANES), 1)
    carry = carry_ref[...]
    cb = cb_ref[...]
    ct = ct_ref[...]
    for sub in range(tn // COMBINE_TT):
        cnt = jnp.sum(sel[:, sub * COMBINE_TT:(sub + 1) * COMBINE_TT], axis=1, keepdims=True)
        col = pl.program_id(0) * (tn // COMBINE_TT) + sub
        cb = jnp.where(lane == col, carry, cb)
        ct = jnp.where(lane == col, cnt, ct)
        carry = carry + cnt
    cb_ref[...] = cb
    ct_ref[...] = ct
    carry_ref[...] = carry

    ek_ref[...] = jnp.concatenate(eks, axis=0)
    wk_ref[...] = jnp.concatenate(wks, axis=0) * scale
    rk_ref[...] = jnp.concatenate(rks, axis=0).astype(I32)
    cnt_ref[...] = carry_ref[...].astype(I32)


def _route(logits_t, bias_col):
    tn = ROUTE_TN
    kt = pl.BlockSpec((TOP_K, tn), lambda i: (0, i))
    return pl.pallas_call(
        _route_kernel,
        grid=(SEQ // tn,),
        in_specs=[pl.BlockSpec((N_EXPERTS, tn), lambda i: (0, i)),
                  pl.BlockSpec((N_EXPERTS, 1), lambda i: (0, 0))],
        out_specs=[kt, kt, kt, pl.BlockSpec((N_EXPERTS, 1), lambda i: (0, 0)),
                   pl.BlockSpec((N_EXPERTS, TILE_LANES), lambda i: (0, 0)),
                   pl.BlockSpec((N_EXPERTS, TILE_LANES), lambda i: (0, 0))],
        out_shape=[jax.ShapeDtypeStruct((TOP_K, SEQ), I32),
                   jax.ShapeDtypeStruct((TOP_K, SEQ), F32),
                   jax.ShapeDtypeStruct((TOP_K, SEQ), I32),
                   jax.ShapeDtypeStruct((N_EXPERTS, 1), I32),
                   jax.ShapeDtypeStruct((N_EXPERTS, TILE_LANES), F32),
                   jax.ShapeDtypeStruct((N_EXPERTS, TILE_LANES), F32)],
        scratch_shapes=[pltpu.VMEM((N_EXPERTS, 1), F32)],
        compiler_params=_cparams(("arbitrary",)),
        name="route",
    )(logits_t, bias_col)


def _dest_kernel(ek_ref, rk_ref, ps_ref, shift_ref, o_ref, pos_ref):
    tt = COMBINE_TT
    rows = lax.broadcasted_iota(I32, (N_EXPERTS, tt), 0)
    lane = lax.broadcasted_iota(I32, (N_EXPERTS, TILE_LANES), 1)
    dests, poss = [], []
    for sub in range(ROUTE_TN // tt):
        tile = pl.program_id(0) * (ROUTE_TN // tt) + sub
        shift_col = jnp.sum(jnp.where(lane == tile, shift_ref[...], 0), axis=1, keepdims=True)
        ds, ps = [], []
        for k in range(TOP_K):
            hit = rows == ek_ref[k:k + 1, sub * tt:(sub + 1) * tt]
            d = (jnp.sum(jnp.where(hit, ps_ref[...], 0), axis=0, keepdims=True)
                 + rk_ref[k:k + 1, sub * tt:(sub + 1) * tt])
            ds.append(d)
            ps.append(d + jnp.sum(jnp.where(hit, shift_col, 0), axis=0, keepdims=True))
        dests.append(jnp.concatenate(ds, axis=0))
        poss.append(jnp.concatenate(ps, axis=0))
    o_ref[...] = jnp.concatenate(dests, axis=1)
    pos_ref[...] = jnp.concatenate(poss, axis=1)


def _dest(ek, rk, pstarts_col, shift):
    tn = ROUTE_TN
    kt = pl.BlockSpec((TOP_K, tn), lambda i: (0, i))
    return pl.pallas_call(
        _dest_kernel,
        grid=(SEQ // tn,),
        in_specs=[kt, kt, pl.BlockSpec((N_EXPERTS, 1), lambda i: (0, 0)),
                  pl.BlockSpec((N_EXPERTS, TILE_LANES), lambda i: (0, 0))],
        out_specs=[kt, kt],
        out_shape=[jax.ShapeDtypeStruct((TOP_K, SEQ), I32), jax.ShapeDtypeStruct((TOP_K, SEQ), I32)],
        compiler_params=_cparams(("arbitrary",)),
        name="dest_rows",
    )(ek, rk, pstarts_col, shift)


DISPATCH_TT = 256
PAD_BITS = tuple(1 << b for b in range(MOE_BM.bit_length() - 2, 2, -1))


def _dispatch_kernel(dest_ref, padoff_ref, padlen_ref, h_ref, out_ref, zeros_ref, sem, zsem):
    tt = DISPATCH_TT
    i = pl.program_id(0)
    base = i * tt

    def issue(t, carry):
        for k in range(TOP_K):
            d = dest_ref[k * SEQ + base + t]
            pltpu.make_async_copy(h_ref.at[pl.ds(t, 1), :], out_ref.at[pl.ds(d, 1), :], sem).start()
        return carry

    lax.fori_loop(0, tt, issue, 0)

    def pad_copy(off, n):
        return pltpu.make_async_copy(zeros_ref.at[pl.ds(0, n), :], out_ref.at[pl.ds(off, n), :], zsem)

    def pad_rows(wait):
        def body(e, carry):
            start = padoff_ref[e]
            head = (-start) & (SUBLANES - 1)
            for j in range(SUBLANES - 1):
                @pl.when(j < head)
                def _():
                    cp = pad_copy(start + j, 1)
                    cp.wait() if wait else cp.start()
            off = start + head
            n = padlen_ref[e] - head
            for bit in PAD_BITS:
                @pl.when((n & bit) != 0)
                def _():
                    cp = pad_copy(pl.multiple_of(off, SUBLANES), bit)
                    cp.wait() if wait else cp.start()
                off = off + (n & bit)
            return carry
        lax.fori_loop(0, N_EXPERTS, body, 0)

    @pl.when(i == 0)
    def _():
        zeros_ref[...] = jnp.zeros_like(zeros_ref)
        pad_rows(wait=False)
        pad_rows(wait=True)

    for _ in range(TOP_K):
        pltpu.make_async_copy(h_ref, out_ref.at[pl.ds(0, tt), :], sem).wait()


def _dispatch(dest_flat, pad_off, pad_len, h2):
    tt = DISPATCH_TT
    grid_spec = pltpu.PrefetchScalarGridSpec(
        num_scalar_prefetch=3,
        grid=(SEQ // tt,),
        in_specs=[pl.BlockSpec((tt, D_MODEL), lambda i, d, po, pn: (i, 0))],
        out_specs=pl.BlockSpec(memory_space=pl.ANY),
        scratch_shapes=[
            pltpu.VMEM((MOE_BM // 2, D_MODEL), F32),
            pltpu.SemaphoreType.DMA,
            pltpu.SemaphoreType.DMA,
        ],
    )
    return pl.pallas_call(
        _dispatch_kernel,
        grid_spec=grid_spec,
        out_shape=jax.ShapeDtypeStruct((MOE_ROWS, D_MODEL), F32),
        compiler_params=_cparams(("arbitrary",)),
        name="dispatch",
    )(dest_flat, pad_off, pad_len, h2)


def _experts_kernel(bexp_ref, first_ref, slot_ref, next_ref, nused_ref,
                    x_ref, wg_hbm, wu_hbm, wd_hbm, o_ref,
                    wgf_ref, wuf_ref, wdf_ref, wgb_ref, wub_ref, wdb_ref, sem):
    b = pl.program_id(0)
    active = b < nused_ref[0]

    def weight_copies(e, s):
        return (pltpu.make_async_copy(wg_hbm.at[e], wgf_ref.at[s], sem.at[s]),
                pltpu.make_async_copy(wu_hbm.at[e], wuf_ref.at[s], sem.at[s]),
                pltpu.make_async_copy(wd_hbm.at[e], wdf_ref.at[s], sem.at[s]))

    @pl.when(jnp.logical_and(active, b == 0))
    def _():
        for cp in weight_copies(bexp_ref[0], 0):
            cp.start()

    @pl.when(jnp.logical_and(active, first_ref[b] == 1))
    def _():
        s = slot_ref[b]
        nxt = next_ref[b]

        @pl.when(nxt < N_EXPERTS)
        def _():
            for cp in weight_copies(nxt, 1 - s):
                cp.start()

        for cp in weight_copies(bexp_ref[b], s):
            cp.wait()
        wgb_ref[...] = wgf_ref[s].astype(BF16)
        wub_ref[...] = wuf_ref[s].astype(BF16)
        wdb_ref[...] = wdf_ref[s].astype(BF16)

    @pl.when(active)
    def _():
        x = x_ref[...].astype(BF16)
        gate = _dot(x, wgb_ref[...])
        up = _dot(x, wub_ref[...])
        hid = (_silu(gate) * up).astype(BF16)
        o_ref[...] = _dot(hid, wdb_ref[...]).astype(BF16)


def _experts(block_exp, first, slot, next_exp, nused, xs_perm, w_gate, w_up, w_down):
    bm = MOE_BM

    def xmap(b, be, fi, sl, ne, nu):
        return (jnp.minimum(b, nu[0] - 1), 0)

    hbm = pl.BlockSpec(memory_space=pl.ANY)
    grid_spec = pltpu.PrefetchScalarGridSpec(
        num_scalar_prefetch=5,
        grid=(MOE_NB,),
        in_specs=[pl.BlockSpec((bm, D_MODEL), xmap), hbm, hbm, hbm],
        out_specs=pl.BlockSpec((bm, D_MODEL), xmap),
        scratch_shapes=[
            pltpu.VMEM((2, D_MODEL, EXPERT_DIM), F32),
            pltpu.VMEM((2, D_MODEL, EXPERT_DIM), F32),
            pltpu.VMEM((2, EXPERT_DIM, D_MODEL), F32),
            pltpu.VMEM((D_MODEL, EXPERT_DIM), BF16),
            pltpu.VMEM((D_MODEL, EXPERT_DIM), BF16),
            pltpu.VMEM((EXPERT_DIM, D_MODEL), BF16),
            pltpu.SemaphoreType.DMA((2,)),
        ],
    )
    return pl.pallas_call(
        _experts_kernel,
        grid_spec=grid_spec,
        out_shape=jax.ShapeDtypeStruct((MOE_ROWS, D_MODEL), BF16),
        compiler_params=_cparams(("arbitrary",)),
        name="experts",
    )(block_exp, first, slot, next_exp, nused, xs_perm, w_gate, w_up, w_down)


SHARED_TM = 512


def _shared_kernel(h2_ref, wg_ref, wu_ref, wd_ref, o_ref):
    h = h2_ref[...].astype(BF16)
    hid = (_silu(_dot(h, wg_ref[...])) * _dot(h, wu_ref[...])).astype(BF16)
    o_ref[...] = _dot(hid, wd_ref[...])


def _shared(h2, wsg, wsu, wsd):
    tm = SHARED_TM
    row = pl.BlockSpec((tm, D_MODEL), lambda i: (i, 0))
    return pl.pallas_call(
        _shared_kernel,
        grid=(SEQ // tm,),
        in_specs=[row,
                  pl.BlockSpec((D_MODEL, EXPERT_DIM), lambda i: (0, 0)),
                  pl.BlockSpec((D_MODEL, EXPERT_DIM), lambda i: (0, 0)),
                  pl.BlockSpec((EXPERT_DIM, D_MODEL), lambda i: (0, 0))],
        out_specs=row,
        out_shape=jax.ShapeDtypeStruct((SEQ, D_MODEL), F32),
        compiler_params=_cparams(("arbitrary",)),
        name="shared_expert",
    )(h2, wsg, wsu, wsd)


COMBINE_ALIGN = 16
COMBINE_CHUNK = 512
COMBINE_ROWS = pl.cdiv(COMBINE_TT * TOP_K + 2 * (COMBINE_ALIGN - 1) * N_EXPERTS,
                       COMBINE_CHUNK) * COMBINE_CHUNK
COMBINE_MIN_CHUNKS = pl.cdiv(COMBINE_TT * TOP_K + (COMBINE_ALIGN - 1) * N_EXPERTS, COMBINE_CHUNK)


def _combine_kernel(ws_ref, wl_ref, so_ref, ysh_ref, x1_ref, wk_ref, pos_ref, g2_ref, nf_ref, eo_ref,
                    o_ref, buf_ref, acc_ref, sem):
    tt = COMBINE_TT
    i = pl.program_id(0)
    slot = i % 2

    def start_windows(tile, slot_):
        def body(e, carry):
            idx = tile * N_EXPERTS + e
            n = pl.multiple_of(wl_ref[idx], COMBINE_ALIGN)
            src = pl.multiple_of(ws_ref[idx], COMBINE_ALIGN)
            dst = pl.multiple_of(slot_ * COMBINE_ROWS + so_ref[idx], COMBINE_ALIGN)

            @pl.when(n > 0)
            def _():
                pltpu.make_async_copy(eo_ref.at[pl.ds(src, n), :], buf_ref.at[pl.ds(dst, n), :],
                                      sem.at[slot_]).start()
            return carry

        lax.fori_loop(0, N_EXPERTS, body, 0)

    @pl.when(i == 0)
    def _():
        buf_ref[...] = jnp.zeros_like(buf_ref)
        start_windows(0, 0)

    @pl.when(i + 1 < pl.num_programs(0))
    def _():
        start_windows(i + 1, 1 - slot)

    last = i * N_EXPERTS + N_EXPERTS - 1
    staged = pl.multiple_of(so_ref[last] + wl_ref[last], COMBINE_ALIGN)
    sbase = pl.multiple_of(slot * COMBINE_ROWS, COMBINE_ROWS)
    pltpu.make_async_copy(eo_ref.at[pl.ds(0, staged), :], buf_ref.at[pl.ds(sbase, staged), :],
                          sem.at[slot]).wait()
    wk = wk_ref[...]
    pos = pos_ref[...]
    def routed(c):
        col = lax.broadcasted_iota(I32, (tt, COMBINE_CHUNK), 1) + c * COMBINE_CHUNK
        w = jnp.zeros((tt, COMBINE_CHUNK), F32)
        for k in range(TOP_K):
            w = w + jnp.where(col == pos[:, k:k + 1], wk[:, k:k + 1], 0.0)
        w_hi = w.astype(BF16)
        w_lo = (w - w_hi.astype(F32)).astype(BF16)
        rows = buf_ref[pl.ds(sbase + c * COMBINE_CHUNK, COMBINE_CHUNK), :]
        both = _dot(jnp.concatenate([w_hi, w_lo], axis=0), rows)
        return both[:tt] + both[tt:]

    acc = ysh_ref[...]
    for c in range(COMBINE_MIN_CHUNKS):
        acc = acc + routed(c)
    acc_ref[...] = acc
    for c in range(COMBINE_MIN_CHUNKS, COMBINE_ROWS // COMBINE_CHUNK):
        @pl.when(c * COMBINE_CHUNK < staged)
        def _():
            acc_ref[...] += routed(c)
    x2 = x1_ref[...] + g2_ref[...] * acc_ref[...]
    o_ref[...] = _rms(x2, nf_ref[...])


def _combine(win_start, win_len, stage_off, y_shared, x1, wk_t, pos_t, g2, nf, eo):
    tt = COMBINE_TT
    vec = pl.BlockSpec((1, D_MODEL), lambda i, a, b, c: (0, 0))
    row = pl.BlockSpec((tt, D_MODEL), lambda i, a, b, c: (i, 0))
    slots = pl.BlockSpec((tt, TOP_K), lambda i, a, b, c: (i, 0))
    grid_spec = pltpu.PrefetchScalarGridSpec(
        num_scalar_prefetch=3,
        grid=(N_TILES,),
        in_specs=[row, row, slots, slots, vec, vec, pl.BlockSpec(memory_space=pl.ANY)],
        out_specs=row,
        scratch_shapes=[
            pltpu.VMEM((2 * COMBINE_ROWS, D_MODEL), BF16),
            pltpu.VMEM((tt, D_MODEL), F32),
            pltpu.SemaphoreType.DMA((2,)),
        ],
    )
    return pl.pallas_call(
        _combine_kernel,
        grid_spec=grid_spec,
        out_shape=jax.ShapeDtypeStruct((SEQ, D_MODEL), F32),
        compiler_params=_cparams(("arbitrary",)),
        name="combine_final",
    )(win_start, win_len, stage_off, y_shared, x1, wk_t, pos_t, g2, nf, eo)


def kernel(x, c, w_ada, b_ada, norm1_g, w_in, conv_w, conv_b, dt_bias, a_log, d_skip, ssd_norm_g, gla_w_gate, gla_b_gate, gla_norm_g, w_out, norm2_g, w_router, router_bias, w_e_gate, w_e_up, w_e_down, w_s_gate, w_s_up, w_s_down, normf_g):
    layer = 0
    x2d = x.reshape(SEQ, D_MODEL)
    mod = _ada(c.reshape(D_MODEL, 1), w_ada[layer], b_ada[layer].reshape(1, -1))
    sh1, sc1, g1, sh2, sc2, g2 = [mod[:, i * D_MODEL:(i + 1) * D_MODEL] for i in range(6)]

    w_big, w_small = _pack_w_in(jnp.swapaxes(w_in[layer], 0, 1))

    proj, small = _inproj(x2d, norm1_g[layer].reshape(1, -1), sc1, sh1, w_big, w_small)

    d_exp = jnp.repeat(d_skip[layer], SSD_HEAD_DIM).reshape(1, -1)
    y_a = _ssd(proj, small, conv_w[layer], conv_b[layer].reshape(1, -1),
               dt_bias[layer].reshape(1, -1), a_log[layer].reshape(1, -1), d_exp,
               ssd_norm_g[layer].reshape(1, -1))
    y_b = _gla(proj, small, gla_w_gate[layer], gla_b_gate[layer].reshape(1, -1),
               gla_norm_g[layer].reshape(1, -1))

    x1, h2, logits_t = _outproj(y_a, y_b, x2d, w_out[layer].astype(BF16), g1,
                                norm2_g[layer].reshape(1, -1), sc2, sh2, w_router[layer].T)

    ek, wk, rk, counts, cbf, ctf = _route(logits_t, router_bias[layer].reshape(-1, 1))

    counts = counts.reshape(-1)
    padded = (counts + MOE_BM - 1) // MOE_BM * MOE_BM
    pends = jnp.cumsum(padded)
    pstarts = pends - padded
    block_start = jnp.arange(MOE_NB, dtype=I32) * MOE_BM
    block_exp = jnp.minimum(
        jnp.sum((pends[None, :] <= block_start[:, None]).astype(I32), axis=1), N_EXPERTS - 1)
    nused = (pends[-1:] // MOE_BM).astype(I32)
    bidx = jnp.arange(MOE_NB, dtype=I32)
    prev_exp = jnp.concatenate([jnp.full((1,), -1, I32), block_exp[:-1]])
    first = jnp.logical_and(bidx < nused[0], block_exp != prev_exp)
    slot = (jnp.cumsum(first.astype(I32)) - 1) & 1
    later_first = jnp.logical_and(first[None, :], bidx[None, :] > bidx[:, None])
    next_first = jnp.min(jnp.where(later_first, bidx[None, :], MOE_NB), axis=1)
    next_exp = jnp.sum(jnp.where(bidx[None, :] == next_first[:, None], block_exp[None, :], 0), axis=1)
    next_exp = jnp.where(next_first < MOE_NB, next_exp, N_EXPERTS).astype(I32)

    cb = cbf[:, :N_TILES].astype(I32)
    ct = ctf[:, :N_TILES].astype(I32)
    run_start = pstarts[:, None].astype(I32) + cb
    win_start = run_start & -COMBINE_ALIGN
    win_end = (run_start + ct + COMBINE_ALIGN - 1) & -COMBINE_ALIGN
    win_len = jnp.where(ct > 0, win_end - win_start, 0)
    stage_off = jnp.cumsum(win_len, axis=0) - win_len
    shift = jnp.pad(stage_off - win_start, ((0, 0), (0, TILE_LANES - N_TILES)))
    tile_major = lambda a: a.T.reshape(-1).astype(I32)

    dest, pos = _dest(ek, rk, pstarts.reshape(-1, 1).astype(I32), shift)
    xs_perm = _dispatch(dest.reshape(-1), (pstarts + counts).astype(I32), (padded - counts).astype(I32), h2)
    eo = _experts(block_exp.astype(I32), first.astype(I32), slot.astype(I32), next_exp, nused, xs_perm,
                  w_e_gate[layer], w_e_up[layer], w_e_down[layer])
    y_shared = _shared(h2, w_s_gate[layer].astype(BF16), w_s_up[layer].astype(BF16),
                       w_s_down[layer].astype(BF16))
    out = _combine(tile_major(win_start), tile_major(win_len), tile_major(stage_off), y_shared, x1,
                   wk.T, pos.T, g2, normf_g.reshape(1, -1), eo)
    return out.reshape(x.shape)
```

```python
import functools

import jax
import jax.numpy as jnp
from jax import lax
from jax.experimental import pallas as pl
from jax.experimental.pallas import tpu as pltpu

F32 = jnp.float32
BF16 = jnp.bfloat16
I32 = jnp.int32

D_MODEL = 2048
SEQ = 8192
NORM_EPS = 1e-6
SSD_HEADS = 32
SSD_HEAD_DIM = 64
SSD_INNER = 2048
SSD_GROUPS = 4
SSD_STATE = 128
SSD_CHUNK = 128
HEADS_PER_GROUP = SSD_HEADS // SSD_GROUPS
GROUP_COLS = HEADS_PER_GROUP * SSD_HEAD_DIM
GLA_HEADS = 4
GLA_KEY_DIM = 1024
GLA_VAL_DIM = 2048
GLA_HEAD_K = 256
GLA_HEAD_V = 512
GLA_GATE_RANK = 16
GLA_GATE_NORM = 16.0
GLA_CHUNK = 64
GLA_STEP = 128
N_EXPERTS = 64
TOP_K = 8
N_EXPERT_GROUPS = 8
TOPK_GROUPS = 4
EXPERT_DIM = 512
ROUTED_SCALE = 2.5

VMEM_LIMIT_BYTES = 56 * 1024 * 1024
SUBLANES = 8

COL_Z, COL_XS, COL_BC, COL_Q, COL_V, COL_R, COL_MA, COL_MB, COL_K = (
    0, 2048, 4096, 5120, 6144, 8192, 10240, 12288, 14336)
PROJ_COLS = 15360
SMALL_COLS = 128

MOE_BM = 256
MOE_NB = SEQ * TOP_K // MOE_BM + N_EXPERTS
MOE_ROWS = MOE_NB * MOE_BM


def _cparams(sem):
    return pltpu.CompilerParams(dimension_semantics=sem, vmem_limit_bytes=VMEM_LIMIT_BYTES)


def _split3(a):
    hi = a.astype(BF16)
    r1 = a - hi.astype(F32)
    mid = r1.astype(BF16)
    lo = (r1 - mid.astype(F32)).astype(BF16)
    return hi, mid, lo


def _dot(a, b):
    return jnp.dot(a, b, preferred_element_type=F32)


def _dot_nt(a, b):
    return lax.dot_general(a, b, (((1,), (1,)), ((), ())), preferred_element_type=F32)


def _dot_tn(a, b):
    return lax.dot_general(a, b, (((0,), (0,)), ((), ())), preferred_element_type=F32)


def _sel_dot(sel_bf16, a):
    hi, mid, lo = _split3(a)
    return _dot(sel_bf16, hi) + _dot(sel_bf16, mid) + _dot(sel_bf16, lo)


def _dot_sel(a, sel_bf16):
    hi, mid, lo = _split3(a)
    return _dot(hi, sel_bf16) + _dot(mid, sel_bf16) + _dot(lo, sel_bf16)


def _dot3(a, b):
    ah = a.astype(BF16)
    al = (a - ah.astype(F32)).astype(BF16)
    bh = b.astype(BF16)
    bl = (b - bh.astype(F32)).astype(BF16)
    return _dot(ah, bh) + _dot(ah, bl) + _dot(al, bh)


def _dot3_nt(a, b):
    ah = a.astype(BF16)
    al = (a - ah.astype(F32)).astype(BF16)
    bh = b.astype(BF16)
    bl = (b - bh.astype(F32)).astype(BF16)
    return _dot_nt(ah, bh) + _dot_nt(ah, bl) + _dot_nt(al, bh)


def _silu(x):
    return x * jax.nn.sigmoid(x)


def _softplus(x):
    return jnp.maximum(x, 0.0) + jnp.log1p(jnp.exp(-jnp.abs(x)))


def _rms(x, g):
    ms = jnp.mean(x * x, axis=-1, keepdims=True)
    return x * lax.rsqrt(ms + NORM_EPS) * g


ADA_TN = 1024


def _ada_kernel(c_ref, w_ref, b_ref, o_ref):
    ca = _silu(c_ref[...])
    o_ref[...] = jnp.sum(w_ref[...] * ca, axis=0, keepdims=True) + b_ref[...]


def _ada(c_col, w_ada, b_ada):
    n = w_ada.shape[1]
    return pl.pallas_call(
        _ada_kernel,
        grid=(n // ADA_TN,),
        in_specs=[
            pl.BlockSpec((D_MODEL, 1), lambda j: (0, 0)),
            pl.BlockSpec((D_MODEL, ADA_TN), lambda j: (0, j)),
            pl.BlockSpec((1, ADA_TN), lambda j: (0, j)),
        ],
        out_specs=pl.BlockSpec((1, ADA_TN), lambda j: (0, j)),
        out_shape=jax.ShapeDtypeStruct((1, n), F32),
        compiler_params=_cparams(("arbitrary",)),
        name="ada_mod",
    )(c_col, w_ada, b_ada)


IN_DIM = 15408
PACK_TN = 1024
W_IN_SEGMENTS = ((0, 5120), (5152, 1024), (7200, 4096), (11312, 4096), (6176, 1024))
W_IN_DT_COL = 5120
W_IN_GLR_COL = 11296
PACK_STARTS = tuple(s + t * PACK_TN for s, w in W_IN_SEGMENTS for t in range(w // PACK_TN))


def _pack_kernel(start_ref, dt_ref, glr_ref, wt_hbm, o_ref, small_ref, win_ref, sem):
    j = pl.program_id(0)
    slot = j % 2

    @pl.when(j == 0)
    def _():
        small_ref[...] = jnp.zeros_like(small_ref)
        small_ref[0:SSD_HEADS, :] = dt_ref[...]
        small_ref[SSD_HEADS:SSD_HEADS + GLA_GATE_RANK, :] = glr_ref[...]

    def window_copy(t, s):
        row = pl.multiple_of(start_ref[t], 2 * SUBLANES)
        return pltpu.make_async_copy(wt_hbm.at[pl.ds(row, PACK_TN), :], win_ref.at[s], sem.at[s])

    @pl.when(j == 0)
    def _():
        window_copy(0, 0).start()

    @pl.when(j + 1 < pl.num_programs(0))
    def _():
        window_copy(j + 1, 1 - slot).start()

    window_copy(j, slot).wait()
    o_ref[...] = win_ref[slot].T.astype(BF16)


def _pack_w_in(w_in_t):
    starts = jnp.asarray(PACK_STARTS, I32)
    grid_spec = pltpu.PrefetchScalarGridSpec(
        num_scalar_prefetch=1,
        grid=(len(PACK_STARTS),),
        in_specs=[pl.BlockSpec((SSD_HEADS, D_MODEL), lambda j, s: (W_IN_DT_COL // SSD_HEADS, 0)),
                  pl.BlockSpec((GLA_GATE_RANK, D_MODEL), lambda j, s: (W_IN_GLR_COL // GLA_GATE_RANK, 0)),
                  pl.BlockSpec(memory_space=pl.ANY)],
        out_specs=[pl.BlockSpec((D_MODEL, PACK_TN), lambda j, s: (0, j)),
                   pl.BlockSpec((SMALL_COLS, D_MODEL), lambda j, s: (0, 0))],
        scratch_shapes=[pltpu.VMEM((2, PACK_TN, D_MODEL), F32), pltpu.SemaphoreType.DMA((2,))],
    )
    return pl.pallas_call(
        _pack_kernel,
        grid_spec=grid_spec,
        out_shape=[jax.ShapeDtypeStruct((D_MODEL, PROJ_COLS), BF16),
                   jax.ShapeDtypeStruct((SMALL_COLS, D_MODEL), F32)],
        compiler_params=_cparams(("arbitrary",)),
        name="pack_w_in",
    )(starts, w_in_t, w_in_t, w_in_t)


INPROJ_TM = 1024
INPROJ_TN = 1024


def _inproj_kernel(x_ref, g_ref, sc_ref, sh_ref, wb_ref, ws_ref, o_ref, os_ref, h_ref):
    @pl.when(pl.program_id(1) == 0)
    def _():
        h = _rms(x_ref[...], g_ref[...]) * (1.0 + sc_ref[...]) + sh_ref[...]
        hb = h.astype(BF16)
        h_ref[...] = hb
        os_ref[...] = _dot_nt(hb, ws_ref[...].astype(BF16))

    o_ref[...] = _dot(h_ref[...], wb_ref[...]).astype(BF16)


def _inproj(x2d, g, sc, sh, w_big, w_small):
    vec = pl.BlockSpec((1, D_MODEL), lambda i, j: (0, 0))
    return pl.pallas_call(
        _inproj_kernel,
        grid=(SEQ // INPROJ_TM, PROJ_COLS // INPROJ_TN),
        in_specs=[
            pl.BlockSpec((INPROJ_TM, D_MODEL), lambda i, j: (i, 0)),
            vec, vec, vec,
            pl.BlockSpec((D_MODEL, INPROJ_TN), lambda i, j: (0, j)),
            pl.BlockSpec((SMALL_COLS, D_MODEL), lambda i, j: (0, 0)),
        ],
        out_specs=[
            pl.BlockSpec((INPROJ_TM, INPROJ_TN), lambda i, j: (i, j)),
            pl.BlockSpec((INPROJ_TM, SMALL_COLS), lambda i, j: (i, 0)),
        ],
        out_shape=[
            jax.ShapeDtypeStruct((SEQ, PROJ_COLS), BF16),
            jax.ShapeDtypeStruct((SEQ, SMALL_COLS), F32),
        ],
        scratch_shapes=[pltpu.VMEM((INPROJ_TM, D_MODEL), BF16)],
        compiler_params=_cparams(("arbitrary", "arbitrary")),
        name="inproj",
    )(x2d, g, sc, sh, w_big, w_small)


def _causal_conv_silu(cur, prev, w_ref, b_ref):
    row = lax.broadcasted_iota(I32, cur.shape, 0)
    acc = cur * w_ref[3:4, :] + b_ref[...]
    for s in (1, 2, 3):
        shifted = jnp.where(row >= s, pltpu.roll(cur, s, 0), pltpu.roll(prev, s, 0))
        acc = acc + shifted * w_ref[3 - s:4 - s, :]
    return _silu(acc)


def _ssd_chunk(xs_ref, bc_ref, z_ref, ma_ref, sm_ref, cwx_ref, cwb_ref, cbx_ref, cbb_ref,
               dtb_ref, alog_ref, dexp_ref, ng_ref, px_ref, pb_ref, st_ref):
    L = SSD_CHUNK
    xs_raw = xs_ref[...].astype(F32)
    bc_raw = bc_ref[...].astype(F32)
    xs = _causal_conv_silu(xs_raw, px_ref[...], cwx_ref, cbx_ref)
    bc = _causal_conv_silu(bc_raw, pb_ref[...], cwb_ref, cbb_ref)
    px_ref[...] = xs_raw
    pb_ref[...] = bc_raw

    dt = _softplus(sm_ref[:, 0:SSD_HEADS] + dtb_ref[...])
    d_a = dt * (-jnp.exp(alog_ref[...]))
    ri = lax.broadcasted_iota(I32, (L, L), 0)
    ci = lax.broadcasted_iota(I32, (L, L), 1)
    causal = ri >= ci
    tril = jnp.where(causal, 1.0, 0.0).astype(BF16)
    acum = _sel_dot(tril, d_a)
    hi32 = lax.broadcasted_iota(I32, (SSD_HEADS, SSD_HEADS), 0)
    hj32 = lax.broadcasted_iota(I32, (SSD_HEADS, SSD_HEADS), 1)
    eye32 = jnp.where(hi32 == hj32, 1.0, 0.0).astype(BF16)
    a_hi, a_mid, a_lo = _split3(acum)
    acum_t = _dot_nt(eye32, a_hi) + _dot_nt(eye32, a_mid) + _dot_nt(eye32, a_lo)
    last = acum[L - 1:L, :]
    e_a = jnp.exp(acum)
    to_end = jnp.exp(last - acum)
    eh = lax.broadcasted_iota(I32, (SSD_HEADS, SSD_INNER), 0)
    ec = lax.broadcasted_iota(I32, (SSD_HEADS, SSD_INNER), 1)
    expand = jnp.where((ec >> 6) == eh, 1.0, 0.0).astype(BF16)
    dt_x = _dot_sel(dt, expand)
    dte_x = _dot_sel(dt * to_end, expand)
    ea_x = _dot_sel(e_a, expand)
    cd_x = ea_x[L - 1:L, :]

    xdt = xs * dt_x
    lane = lax.broadcasted_iota(I32, xdt.shape, 1)
    left = (lane & 64) == 0
    xdt_l = jnp.where(left, xdt, 0.0).astype(BF16)
    xdt_r = jnp.where(left, 0.0, xdt).astype(BF16)
    xdte = (xs * dte_x).astype(BF16)

    y_diag = []
    y_off = []
    for g in range(SSD_GROUPS):
        bg = bc[:, g * SSD_STATE:(g + 1) * SSD_STATE].astype(BF16)
        cg = bc[:, 512 + g * SSD_STATE:512 + (g + 1) * SSD_STATE].astype(BF16)
        cb = _dot_nt(cg, bg)
        gsl = slice(g * GROUP_COLS, (g + 1) * GROUP_COLS)
        st = st_ref[g]
        y_off.append(_dot(cg, st.astype(BF16)))
        st_ref[g] = st * cd_x[:, gsl] + _dot_tn(bg, xdte[:, gsl])
        for p in range(HEADS_PER_GROUP // 2):
            ms = []
            for h in (g * HEADS_PER_GROUP + 2 * p, g * HEADS_PER_GROUP + 2 * p + 1):
                seg = acum[:, h:h + 1] - acum_t[h:h + 1, :]
                decay = jnp.exp(jnp.where(causal, seg, -jnp.inf))
                ms.append((cb * decay).astype(BF16))
            m_cat = jnp.concatenate(ms, axis=1)
            psl = slice((g * 4 + p) * 128, (g * 4 + p + 1) * 128)
            x_cat = jnp.concatenate([xdt_l[:, psl], xdt_r[:, psl]], axis=0)
            y_diag.append(_dot(m_cat, x_cat))

    y = (jnp.concatenate(y_diag, axis=1) + jnp.concatenate(y_off, axis=1) * ea_x
         + dexp_ref[...] * xs)
    yf = y * _silu(z_ref[...].astype(F32))
    outs = []
    for g in range(SSD_GROUPS):
        seg = yf[:, g * GROUP_COLS:(g + 1) * GROUP_COLS]
        ms = jnp.mean(seg * seg, axis=-1, keepdims=True)
        outs.append(seg * lax.rsqrt(ms + NORM_EPS))
    y_ssd = jnp.concatenate(outs, axis=1) * ng_ref[...]
    return jax.nn.sigmoid(ma_ref[...].astype(F32)) * y_ssd


def _gla_chunks(q_ref, k_ref, v_ref, r_ref, mb_ref, sm_ref, wg_ref, bg_ref, ng_ref, s_ref, y_ssd, o_ref):
    L = GLA_CHUNK
    ri = lax.broadcasted_iota(I32, (L, L), 0)
    ci = lax.broadcasted_iota(I32, (L, L), 1)
    causal = ri >= ci
    tril = jnp.where(causal, 1.0, 0.0).astype(BF16)

    for c in range(GLA_STEP // L):
        rs = slice(c * L, (c + 1) * L)
        glr = sm_ref[rs, SSD_HEADS:SSD_HEADS + GLA_GATE_RANK]
        pre = _dot3(glr, wg_ref[...]) + bg_ref[...]
        gk = (jnp.minimum(pre, 0.0) - jnp.log1p(jnp.exp(-jnp.abs(pre)))) / GLA_GATE_NORM
        bcum = _sel_dot(tril, gk)
        bmid = bcum[L // 2:L // 2 + 1, :]
        blast = bcum[L - 1:L, :]
        q = q_ref[rs, :].astype(F32) * (GLA_HEAD_K ** -0.5)
        k = k_ref[rs, :].astype(F32)
        q_rel = (q * jnp.exp(bcum - bmid)).astype(BF16)
        k_rel = (k * jnp.exp(bmid - bcum)).astype(BF16)
        q_int = (q * jnp.exp(bcum)).astype(BF16)
        k_end = (k * jnp.exp(blast - bcum)).astype(BF16)
        dec = jnp.exp(blast)
        v = v_ref[rs, :]

        outs = []
        for h in range(GLA_HEADS):
            ks = slice(h * GLA_HEAD_K, (h + 1) * GLA_HEAD_K)
            vs = slice(h * GLA_HEAD_V, (h + 1) * GLA_HEAD_V)
            att = jnp.where(causal, _dot_nt(q_rel[:, ks], k_rel[:, ks]), 0.0)
            s_t = s_ref[h]
            o_h = _dot(att.astype(BF16), v[:, vs]) + _dot_nt(q_int[:, ks], s_t.astype(BF16))
            s_ref[h] = s_t * dec[:, ks] + _dot_tn(v[:, vs], k_end[:, ks])
            outs.append(_rms(o_h, ng_ref[...]))
        o = jnp.concatenate(outs, axis=1)
        o = o * _silu(r_ref[rs, :].astype(F32)) * jax.nn.sigmoid(mb_ref[rs, :].astype(F32))
        o_ref[rs, :] = (y_ssd[rs, :] + o).astype(BF16)


N_SSD_IN = 13
N_GLA_IN = 9


def _mixer_kernel(*refs):
    ssd_in = refs[:N_SSD_IN]
    gla_in = refs[N_SSD_IN:N_SSD_IN + N_GLA_IN]
    o_ref, px_ref, pb_ref, st_ref, s_ref = refs[N_SSD_IN + N_GLA_IN:]

    @pl.when(pl.program_id(0) == 0)
    def _():
        px_ref[...] = jnp.zeros_like(px_ref)
        pb_ref[...] = jnp.zeros_like(pb_ref)
        st_ref[...] = jnp.zeros_like(st_ref)
        s_ref[...] = jnp.zeros_like(s_ref)

    y_ssd = _ssd_chunk(*ssd_in, px_ref, pb_ref, st_ref)
    _gla_chunks(*gla_in, s_ref, y_ssd, o_ref)


def _mixer(proj, small, conv_w, conv_b, dt_bias, a_log, d_exp, ssd_norm_g, w_gate, b_gate, gla_norm_g):
    L = SSD_CHUNK
    full = lambda shape: pl.BlockSpec(shape, lambda i: (0, 0))
    cols = lambda width, col: pl.BlockSpec((L, width), lambda i: (i, col // width))
    ssd_specs = [
        cols(2048, COL_XS), cols(1024, COL_BC), cols(2048, COL_Z), cols(2048, COL_MA),
        pl.BlockSpec((L, SMALL_COLS), lambda i: (i, 0)),
        pl.BlockSpec((4, 2048), lambda i: (0, 0)),
        pl.BlockSpec((4, 1024), lambda i: (0, 2)),
        pl.BlockSpec((1, 2048), lambda i: (0, 0)),
        pl.BlockSpec((1, 1024), lambda i: (0, 2)),
        full((1, SSD_HEADS)), full((1, SSD_HEADS)),
        full((1, SSD_INNER)), full((1, SSD_INNER)),
    ]
    gla_specs = [
        cols(1024, COL_Q), cols(1024, COL_K), cols(2048, COL_V), cols(2048, COL_R), cols(2048, COL_MB),
        pl.BlockSpec((L, SMALL_COLS), lambda i: (i, 0)),
        full((GLA_GATE_RANK, GLA_KEY_DIM)), full((1, GLA_KEY_DIM)), full((1, GLA_HEAD_V)),
    ]
    assert len(ssd_specs) == N_SSD_IN and len(gla_specs) == N_GLA_IN and GLA_STEP == L
    return pl.pallas_call(
        _mixer_kernel,
        grid=(SEQ // L,),
        in_specs=ssd_specs + gla_specs,
        out_specs=pl.BlockSpec((L, D_MODEL), lambda i: (i, 0)),
        out_shape=jax.ShapeDtypeStruct((SEQ, D_MODEL), BF16),
        scratch_shapes=[
            pltpu.VMEM((L, 2048), F32),
            pltpu.VMEM((L, 1024), F32),
            pltpu.VMEM((SSD_GROUPS, SSD_STATE, GROUP_COLS), F32),
            pltpu.VMEM((GLA_HEADS, GLA_HEAD_V, GLA_HEAD_K), F32),
        ],
        compiler_params=_cparams(("arbitrary",)),
        name="mixer",
    )(proj, proj, proj, proj, small, conv_w, conv_w, conv_b, conv_b, dt_bias, a_log, d_exp, ssd_norm_g,
      proj, proj, proj, proj, proj, small, w_gate, b_gate, gla_norm_g)


OUTPROJ_TM = 512


def _outproj_kernel(m_ref, x_ref, w_ref, g1_ref, n2_ref, sc_ref, sh_ref, wr_ref,
                    x1_ref, h2_ref, lg_ref):
    x1 = x_ref[...] + g1_ref[...] * _dot(m_ref[...], w_ref[...])
    x1_ref[...] = x1
    h2 = _rms(x1, n2_ref[...]) * (1.0 + sc_ref[...]) + sh_ref[...]
    h2_ref[...] = h2
    lg_ref[...] = _dot3_nt(wr_ref[...], h2)


def _outproj(mixed, x2d, w_out_bf, g1, n2g, sc2, sh2, w_router_t):
    tm = OUTPROJ_TM
    vec = pl.BlockSpec((1, D_MODEL), lambda i: (0, 0))
    row = pl.BlockSpec((tm, D_MODEL), lambda i: (i, 0))
    return pl.pallas_call(
        _outproj_kernel,
        grid=(SEQ // tm,),
        in_specs=[row, row,
                  pl.BlockSpec((D_MODEL, D_MODEL), lambda i: (0, 0), pipeline_mode=pl.Buffered(1)),
                  vec, vec, vec, vec,
                  pl.BlockSpec((N_EXPERTS, D_MODEL), lambda i: (0, 0))],
        out_specs=[row, row, pl.BlockSpec((N_EXPERTS, tm), lambda i: (0, i))],
        out_shape=[jax.ShapeDtypeStruct((SEQ, D_MODEL), F32),
                   jax.ShapeDtypeStruct((SEQ, D_MODEL), F32),
                   jax.ShapeDtypeStruct((N_EXPERTS, SEQ), F32)],
        compiler_params=_cparams(("arbitrary",)),
        name="outproj_router",
    )(mixed, x2d, w_out_bf, g1, n2g, sc2, sh2, w_router_t)


ROUTE_TN = 512
COMBINE_TT = 128
N_TILES = SEQ // COMBINE_TT
TILE_LANES = 128


def _route_kernel(lg_ref, bias_ref, ek_ref, wk_ref, rk_ref, cnt_ref, cb_ref, ct_ref, carry_ref):
    tn = ROUTE_TN
    gsz = N_EXPERTS // N_EXPERT_GROUPS

    @pl.when(pl.program_id(0) == 0)
    def _():
        carry_ref[...] = jnp.zeros_like(carry_ref)
        cb_ref[...] = jnp.zeros_like(cb_ref)
        ct_ref[...] = jnp.zeros_like(ct_ref)

    s = jax.nn.sigmoid(lg_ref[...])
    choice = s + bias_ref[...]
    r8 = lax.broadcasted_iota(I32, (gsz, tn), 0)
    neg = -jnp.inf

    def top1(cur, rows, nrows):
        m = jnp.max(cur, axis=0, keepdims=True)
        idx = jnp.min(jnp.where(cur == m, rows, nrows), axis=0, keepdims=True)
        return m, idx

    gscores = []
    for g in range(N_EXPERT_GROUPS):
        blk = choice[g * gsz:(g + 1) * gsz, :]
        m1, i1 = top1(blk, r8, gsz)
        m2, _ = top1(jnp.where(r8 == i1, neg, blk), r8, gsz)
        gscores.append(m1 + m2)
    cur = jnp.concatenate(gscores, axis=0)
    gsel = jnp.zeros((N_EXPERT_GROUPS, tn), F32)
    for _ in range(TOPK_GROUPS):
        _, idx = top1(cur, r8, N_EXPERT_GROUPS)
        hit = r8 == idx
        gsel = jnp.where(hit, 1.0, gsel)
        cur = jnp.where(hit, neg, cur)
    emask = jnp.concatenate(
        [jnp.broadcast_to(gsel[g:g + 1, :], (gsz, tn)) for g in range(N_EXPERT_GROUPS)], axis=0)
    rows = lax.broadcasted_iota(I32, (N_EXPERTS, tn), 0)
    cur = jnp.where(emask > 0.5, choice, neg)
    sel = jnp.zeros((N_EXPERTS, tn), F32)
    hits, eks, wks = [], [], []
    for _ in range(TOP_K):
        _, idx = top1(cur, rows, N_EXPERTS)
        hit = rows == idx
        hits.append(hit)
        eks.append(idx)
        wks.append(jnp.sum(jnp.where(hit, s, 0.0), axis=0, keepdims=True))
        cur = jnp.where(hit, neg, cur)
        sel = jnp.where(hit, 1.0, sel)
    wsum = wks[0]
    for w in wks[1:]:
        wsum = wsum + w
    scale = ROUTED_SCALE / (wsum + 1e-20)

    ti = lax.broadcasted_iota(I32, (tn, tn), 0)
    tj = lax.broadcasted_iota(I32, (tn, tn), 1)
    before = jnp.where(ti < tj, 1.0, 0.0).astype(BF16)
    rank = _dot(sel.astype(BF16), before) + carry_ref[...]
    rks = [jnp.sum(jnp.where(hit, rank, 0.0), axis=0, keepdims=True) for hit in hits]

    lane = lax.broadcasted_iota(I32, (N_EXPERTS, TILE_LANES), 1)
    carry = carry_ref[...]
    cb = cb_ref[...]
    ct = ct_ref[...]
    for sub in range(tn // COMBINE_TT):
        cnt = jnp.sum(sel[:, sub * COMBINE_TT:(sub + 1) * COMBINE_TT], axis=1, keepdims=True)
        col = pl.program_id(0) * (tn // COMBINE_TT) + sub
        cb = jnp.where(lane == col, carry, cb)
        ct = jnp.where(lane == col, cnt, ct)
        carry = carry + cnt
    cb_ref[...] = cb
    ct_ref[...] = ct
    carry_ref[...] = carry

    ek_ref[...] = jnp.concatenate(eks, axis=0)
    wk_ref[...] = jnp.concatenate(wks, axis=0) * scale
    rk_ref[...] = jnp.concatenate(rks, axis=0).astype(I32)
    cnt_ref[...] = carry_ref[...].astype(I32)


def _route(logits_t, bias_col):
    tn = ROUTE_TN
    kt = pl.BlockSpec((TOP_K, tn), lambda i: (0, i))
    return pl.pallas_call(
        _route_kernel,
        grid=(SEQ // tn,),
        in_specs=[pl.BlockSpec((N_EXPERTS, tn), lambda i: (0, i)),
                  pl.BlockSpec((N_EXPERTS, 1), lambda i: (0, 0))],
        out_specs=[kt, kt, kt, pl.BlockSpec((N_EXPERTS, 1), lambda i: (0, 0)),
                   pl.BlockSpec((N_EXPERTS, TILE_LANES), lambda i: (0, 0)),
                   pl.BlockSpec((N_EXPERTS, TILE_LANES), lambda i: (0, 0))],
        out_shape=[jax.ShapeDtypeStruct((TOP_K, SEQ), I32),
                   jax.ShapeDtypeStruct((TOP_K, SEQ), F32),
                   jax.ShapeDtypeStruct((TOP_K, SEQ), I32),
                   jax.ShapeDtypeStruct((N_EXPERTS, 1), I32),
                   jax.ShapeDtypeStruct((N_EXPERTS, TILE_LANES), F32),
                   jax.ShapeDtypeStruct((N_EXPERTS, TILE_LANES), F32)],
        scratch_shapes=[pltpu.VMEM((N_EXPERTS, 1), F32)],
        compiler_params=_cparams(("arbitrary",)),
        name="route",
    )(logits_t, bias_col)


def _dest_kernel(ek_ref, rk_ref, ps_ref, shift_ref, o_ref, pos_ref):
    tt = COMBINE_TT
    rows = lax.broadcasted_iota(I32, (N_EXPERTS, tt), 0)
    lane = lax.broadcasted_iota(I32, (N_EXPERTS, TILE_LANES), 1)
    dests, poss = [], []
    for sub in range(ROUTE_TN // tt):
        tile = pl.program_id(0) * (ROUTE_TN // tt) + sub
        shift_col = jnp.sum(jnp.where(lane == tile, shift_ref[...], 0), axis=1, keepdims=True)
        ds, ps = [], []
        for k in range(TOP_K):
            hit = rows == ek_ref[k:k + 1, sub * tt:(sub + 1) * tt]
            d = (jnp.sum(jnp.where(hit, ps_ref[...], 0), axis=0, keepdims=True)
                 + rk_ref[k:k + 1, sub * tt:(sub + 1) * tt])
            ds.append(d)
            ps.append(d + jnp.sum(jnp.where(hit, shift_col, 0), axis=0, keepdims=True))
        dests.append(jnp.concatenate(ds, axis=0))
        poss.append(jnp.concatenate(ps, axis=0))
    o_ref[...] = jnp.concatenate(dests, axis=1)
    pos_ref[...] = jnp.concatenate(poss, axis=1)


def _dest(ek, rk, pstarts_col, shift):
    tn = ROUTE_TN
    kt = pl.BlockSpec((TOP_K, tn), lambda i: (0, i))
    return pl.pallas_call(
        _dest_kernel,
        grid=(SEQ // tn,),
        in_specs=[kt, kt, pl.BlockSpec((N_EXPERTS, 1), lambda i: (0, 0)),
                  pl.BlockSpec((N_EXPERTS, TILE_LANES), lambda i: (0, 0))],
        out_specs=[kt, kt],
        out_shape=[jax.ShapeDtypeStruct((TOP_K, SEQ), I32), jax.ShapeDtypeStruct((TOP_K, SEQ), I32)],
        compiler_params=_cparams(("arbitrary",)),
        name="dest_rows",
    )(ek, rk, pstarts_col, shift)


DISPATCH_TT = 256
PAD_BITS = tuple(1 << b for b in range(MOE_BM.bit_length() - 2, 2, -1))


def _dispatch_kernel(dest_ref, padoff_ref, padlen_ref, h_ref, out_ref, zeros_ref, sem, zsem):
    tt = DISPATCH_TT
    i = pl.program_id(0)
    base = i * tt

    def issue(t, carry):
        for k in range(TOP_K):
            d = dest_ref[k * SEQ + base + t]
            pltpu.make_async_copy(h_ref.at[pl.ds(t, 1), :], out_ref.at[pl.ds(d, 1), :],
                                  sem).start(priority=k % 2)
        return carry

    lax.fori_loop(0, tt, issue, 0)

    def pad_copy(off, n):
        return pltpu.make_async_copy(zeros_ref.at[pl.ds(0, n), :], out_ref.at[pl.ds(off, n), :], zsem)

    def pad_rows(wait):
        def body(e, carry):
            start = padoff_ref[e]
            head = (-start) & (SUBLANES - 1)
            for j in range(SUBLANES - 1):
                @pl.when(j < head)
                def _():
                    cp = pad_copy(start + j, 1)
                    cp.wait() if wait else cp.start()
            off = start + head
            n = padlen_ref[e] - head
            for bit in PAD_BITS:
                @pl.when((n & bit) != 0)
                def _():
                    cp = pad_copy(pl.multiple_of(off, SUBLANES), bit)
                    cp.wait() if wait else cp.start()
                off = off + (n & bit)
            return carry
        lax.fori_loop(0, N_EXPERTS, body, 0)

    @pl.when(i == 0)
    def _():
        zeros_ref[...] = jnp.zeros_like(zeros_ref)
        pad_rows(wait=False)
        pad_rows(wait=True)

    for _ in range(TOP_K):
        pltpu.make_async_copy(h_ref, out_ref.at[pl.ds(0, tt), :], sem).wait()


def _dispatch(dest_flat, pad_off, pad_len, h2):
    tt = DISPATCH_TT
    grid_spec = pltpu.PrefetchScalarGridSpec(
        num_scalar_prefetch=3,
        grid=(SEQ // tt,),
        in_specs=[pl.BlockSpec((tt, D_MODEL), lambda i, d, po, pn: (i, 0))],
        out_specs=pl.BlockSpec(memory_space=pl.ANY),
        scratch_shapes=[
            pltpu.VMEM((MOE_BM // 2, D_MODEL), F32),
            pltpu.SemaphoreType.DMA,
            pltpu.SemaphoreType.DMA,
        ],
    )
    return pl.pallas_call(
        _dispatch_kernel,
        grid_spec=grid_spec,
        out_shape=jax.ShapeDtypeStruct((MOE_ROWS, D_MODEL), F32),
        compiler_params=_cparams(("arbitrary",)),
        name="dispatch",
    )(dest_flat, pad_off, pad_len, h2)


def _experts_kernel(bexp_ref, first_ref, slot_ref, next_ref, nused_ref,
                    x_ref, wg_hbm, wu_hbm, wd_hbm, o_ref,
                    wgf_ref, wuf_ref, wdf_ref, wgb_ref, wub_ref, wdb_ref, sem):
    b = pl.program_id(0)
    active = b < nused_ref[0]

    def weight_copies(e, s):
        return (pltpu.make_async_copy(wg_hbm.at[e], wgf_ref.at[s], sem.at[s]),
                pltpu.make_async_copy(wu_hbm.at[e], wuf_ref.at[s], sem.at[s]),
                pltpu.make_async_copy(wd_hbm.at[e], wdf_ref.at[s], sem.at[s]))

    @pl.when(jnp.logical_and(active, b == 0))
    def _():
        for cp in weight_copies(bexp_ref[0], 0):
            cp.start()

    @pl.when(jnp.logical_and(active, first_ref[b] == 1))
    def _():
        s = slot_ref[b]
        nxt = next_ref[b]

        @pl.when(nxt < N_EXPERTS)
        def _():
            for cp in weight_copies(nxt, 1 - s):
                cp.start()

        for cp in weight_copies(bexp_ref[b], s):
            cp.wait()
        wgb_ref[...] = wgf_ref[s].astype(BF16)
        wub_ref[...] = wuf_ref[s].astype(BF16)
        wdb_ref[...] = wdf_ref[s].astype(BF16)

    @pl.when(active)
    def _():
        x = x_ref[...].astype(BF16)
        gate = _dot(x, wgb_ref[...])
        up = _dot(x, wub_ref[...])
        hid = (_silu(gate) * up).astype(BF16)
        o_ref[...] = _dot(hid, wdb_ref[...]).astype(BF16)


def _experts(block_exp, first, slot, next_exp, nused, xs_perm, w_gate, w_up, w_down):
    bm = MOE_BM

    def xmap(b, be, fi, sl, ne, nu):
        return (jnp.minimum(b, nu[0] - 1), 0)

    hbm = pl.BlockSpec(memory_space=pl.ANY)
    grid_spec = pltpu.PrefetchScalarGridSpec(
        num_scalar_prefetch=5,
        grid=(MOE_NB,),
        in_specs=[pl.BlockSpec((bm, D_MODEL), xmap), hbm, hbm, hbm],
        out_specs=pl.BlockSpec((bm, D_MODEL), xmap),
        scratch_shapes=[
            pltpu.VMEM((2, D_MODEL, EXPERT_DIM), F32),
            pltpu.VMEM((2, D_MODEL, EXPERT_DIM), F32),
            pltpu.VMEM((2, EXPERT_DIM, D_MODEL), F32),
            pltpu.VMEM((D_MODEL, EXPERT_DIM), BF16),
            pltpu.VMEM((D_MODEL, EXPERT_DIM), BF16),
            pltpu.VMEM((EXPERT_DIM, D_MODEL), BF16),
            pltpu.SemaphoreType.DMA((2,)),
        ],
    )
    return pl.pallas_call(
        _experts_kernel,
        grid_spec=grid_spec,
        out_shape=jax.ShapeDtypeStruct((MOE_ROWS, D_MODEL), BF16),
        compiler_params=_cparams(("arbitrary",)),
        name="experts",
    )(block_exp, first, slot, next_exp, nused, xs_perm, w_gate, w_up, w_down)


SHARED_TM = 512


def _shared_kernel(h2_ref, wg_ref, wu_ref, wd_ref, o_ref):
    h = h2_ref[...].astype(BF16)
    hid = (_silu(_dot(h, wg_ref[...])) * _dot(h, wu_ref[...])).astype(BF16)
    o_ref[...] = _dot(hid, wd_ref[...])


def _shared(h2, wsg, wsu, wsd):
    tm = SHARED_TM
    row = pl.BlockSpec((tm, D_MODEL), lambda i: (i, 0))
    return pl.pallas_call(
        _shared_kernel,
        grid=(SEQ // tm,),
        in_specs=[row,
                  pl.BlockSpec((D_MODEL, EXPERT_DIM), lambda i: (0, 0)),
                  pl.BlockSpec((D_MODEL, EXPERT_DIM), lambda i: (0, 0)),
                  pl.BlockSpec((EXPERT_DIM, D_MODEL), lambda i: (0, 0))],
        out_specs=row,
        out_shape=jax.ShapeDtypeStruct((SEQ, D_MODEL), F32),
        compiler_params=_cparams(("arbitrary",)),
        name="shared_expert",
    )(h2, wsg, wsu, wsd)


COMBINE_ALIGN = 16
COMBINE_CHUNK = 512
COMBINE_ROWS = pl.cdiv(COMBINE_TT * TOP_K + 2 * (COMBINE_ALIGN - 1) * N_EXPERTS,
                       COMBINE_CHUNK) * COMBINE_CHUNK
COMBINE_MIN_CHUNKS = pl.cdiv(COMBINE_TT * TOP_K + (COMBINE_ALIGN - 1) * N_EXPERTS, COMBINE_CHUNK)


def _combine_kernel(ws_ref, wl_ref, so_ref, ysh_ref, x1_ref, wk_ref, pos_ref, g2_ref, nf_ref, eo_ref,
                    o_ref, buf_ref, acc_ref, sem):
    tt = COMBINE_TT
    i = pl.program_id(0)
    slot = i % 2

    def start_windows(tile, slot_):
        def body(e, carry):
            idx = tile * N_EXPERTS + e
            n = pl.multiple_of(wl_ref[idx], COMBINE_ALIGN)
            src = pl.multiple_of(ws_ref[idx], COMBINE_ALIGN)
            dst = pl.multiple_of(slot_ * COMBINE_ROWS + so_ref[idx], COMBINE_ALIGN)

            @pl.when(n > 0)
            def _():
                pltpu.make_async_copy(eo_ref.at[pl.ds(src, n), :], buf_ref.at[pl.ds(dst, n), :],
                                      sem.at[slot_]).start()
            return carry

        lax.fori_loop(0, N_EXPERTS, body, 0)

    @pl.when(i == 0)
    def _():
        buf_ref[...] = jnp.zeros_like(buf_ref)
        start_windows(0, 0)

    @pl.when(i + 1 < pl.num_programs(0))
    def _():
        start_windows(i + 1, 1 - slot)

    last = i * N_EXPERTS + N_EXPERTS - 1
    staged = pl.multiple_of(so_ref[last] + wl_ref[last], COMBINE_ALIGN)
    sbase = pl.multiple_of(slot * COMBINE_ROWS, COMBINE_ROWS)
    pltpu.make_async_copy(eo_ref.at[pl.ds(0, staged), :], buf_ref.at[pl.ds(sbase, staged), :],
                          sem.at[slot]).wait()
    wk = wk_ref[...]
    pos = pos_ref[...]
    def routed(c):
        col = lax.broadcasted_iota(I32, (tt, COMBINE_CHUNK), 1) + c * COMBINE_CHUNK
        w = jnp.zeros((tt, COMBINE_CHUNK), F32)
        for k in range(TOP_K):
            w = w + jnp.where(col == pos[:, k:k + 1], wk[:, k:k + 1], 0.0)
        w_hi = w.astype(BF16)
        w_lo = (w - w_hi.astype(F32)).astype(BF16)
        rows = buf_ref[pl.ds(sbase + c * COMBINE_CHUNK, COMBINE_CHUNK), :]
        both = _dot(jnp.concatenate([w_hi, w_lo], axis=0), rows)
        return both[:tt] + both[tt:]

    acc = ysh_ref[...]
    for c in range(COMBINE_MIN_CHUNKS):
        acc = acc + routed(c)
    acc_ref[...] = acc
    for c in range(COMBINE_MIN_CHUNKS, COMBINE_ROWS // COMBINE_CHUNK):
        @pl.when(c * COMBINE_CHUNK < staged)
        def _():
            acc_ref[...] += routed(c)
    x2 = x1_ref[...] + g2_ref[...] * acc_ref[...]
    o_ref[...] = _rms(x2, nf_ref[...])


def _combine(win_start, win_len, stage_off, y_shared, x1, wk_t, pos_t, g2, nf, eo):
    tt = COMBINE_TT
    vec = pl.BlockSpec((1, D_MODEL), lambda i, a, b, c: (0, 0))
    row = pl.BlockSpec((tt, D_MODEL), lambda i, a, b, c: (i, 0))
    slots = pl.BlockSpec((tt, TOP_K), lambda i, a, b, c: (i, 0))
    grid_spec = pltpu.PrefetchScalarGridSpec(
        num_scalar_prefetch=3,
        grid=(N_TILES,),
        in_specs=[row, row, slots, slots, vec, vec, pl.BlockSpec(memory_space=pl.ANY)],
        out_specs=row,
        scratch_shapes=[
            pltpu.VMEM((2 * COMBINE_ROWS, D_MODEL), BF16),
            pltpu.VMEM((tt, D_MODEL), F32),
            pltpu.SemaphoreType.DMA((2,)),
        ],
    )
    return pl.pallas_call(
        _combine_kernel,
        grid_spec=grid_spec,
        out_shape=jax.ShapeDtypeStruct((SEQ, D_MODEL), F32),
        compiler_params=_cparams(("arbitrary",)),
        name="combine_final",
    )(win_start, win_len, stage_off, y_shared, x1, wk_t, pos_t, g2, nf, eo)


def kernel(x, c, w_ada, b_ada, norm1_g, w_in, conv_w, conv_b, dt_bias, a_log, d_skip, ssd_norm_g, gla_w_gate, gla_b_gate, gla_norm_g, w_out, norm2_g, w_router, router_bias, w_e_gate, w_e_up, w_e_down, w_s_gate, w_s_up, w_s_down, normf_g):
    layer = 0
    x2d = x.reshape(SEQ, D_MODEL)
    mod = _ada(c.reshape(D_MODEL, 1), w_ada[layer], b_ada[layer].reshape(1, -1))
    sh1, sc1, g1, sh2, sc2, g2 = [mod[:, i * D_MODEL:(i + 1) * D_MODEL] for i in range(6)]

    w_big, w_small = _pack_w_in(jnp.swapaxes(w_in[layer], 0, 1))

    proj, small = _inproj(x2d, norm1_g[layer].reshape(1, -1), sc1, sh1, w_big, w_small)

    d_exp = jnp.repeat(d_skip[layer], SSD_HEAD_DIM).reshape(1, -1)
    mixed = _mixer(proj, small, conv_w[layer], conv_b[layer].reshape(1, -1),
                   dt_bias[layer].reshape(1, -1), a_log[layer].reshape(1, -1), d_exp,
                   ssd_norm_g[layer].reshape(1, -1), gla_w_gate[layer],
                   gla_b_gate[layer].reshape(1, -1), gla_norm_g[layer].reshape(1, -1))

    x1, h2, logits_t = _outproj(mixed, x2d, w_out[layer].astype(BF16), g1,
                                norm2_g[layer].reshape(1, -1), sc2, sh2, w_router[layer].T)

    ek, wk, rk, counts, cbf, ctf = _route(logits_t, router_bias[layer].reshape(-1, 1))

    counts = counts.reshape(-1)
    padded = (counts + MOE_BM - 1) // MOE_BM * MOE_BM
    pends = jnp.cumsum(padded)
    pstarts = pends - padded
    block_start = jnp.arange(MOE_NB, dtype=I32) * MOE_BM
    block_exp = jnp.minimum(
        jnp.sum((pends[None, :] <= block_start[:, None]).astype(I32), axis=1), N_EXPERTS - 1)
    nused = (pends[-1:] // MOE_BM).astype(I32)
    bidx = jnp.arange(MOE_NB, dtype=I32)
    prev_exp = jnp.concatenate([jnp.full((1,), -1, I32), block_exp[:-1]])
    first = jnp.logical_and(bidx < nused[0], block_exp != prev_exp)
    slot = (jnp.cumsum(first.astype(I32)) - 1) & 1
    later_first = jnp.logical_and(first[None, :], bidx[None, :] > bidx[:, None])
    next_first = jnp.min(jnp.where(later_first, bidx[None, :], MOE_NB), axis=1)
    next_exp = jnp.sum(jnp.where(bidx[None, :] == next_first[:, None], block_exp[None, :], 0), axis=1)
    next_exp = jnp.where(next_first < MOE_NB, next_exp, N_EXPERTS).astype(I32)

    cb = cbf[:, :N_TILES].astype(I32)
    ct = ctf[:, :N_TILES].astype(I32)
    run_start = pstarts[:, None].astype(I32) + cb
    win_start = run_start & -COMBINE_ALIGN
    win_end = (run_start + ct + COMBINE_ALIGN - 1) & -COMBINE_ALIGN
    win_len = jnp.where(ct > 0, win_end - win_start, 0)
    stage_off = jnp.cumsum(win_len, axis=0) - win_len
    shift = jnp.pad(stage_off - win_start, ((0, 0), (0, TILE_LANES - N_TILES)))
    tile_major = lambda a: a.T.reshape(-1).astype(I32)

    dest, pos = _dest(ek, rk, pstarts.reshape(-1, 1).astype(I32), shift)
    xs_perm = _dispatch(dest.reshape(-1), (pstarts + counts).astype(I32), (padded - counts).astype(I32), h2)
    eo = _experts(block_exp.astype(I32), first.astype(I32), slot.astype(I32), next_exp, nused, xs_perm,
                  w_e_gate[layer], w_e_up[layer], w_e_down[layer])
    y_shared = _shared(h2, w_s_gate[layer].astype(BF16), w_s_up[layer].astype(BF16),
                       w_s_down[layer].astype(BF16))
    out = _combine(tile_major(win_start), tile_major(win_len), tile_major(stage_off), y_shared, x1,
                   wk.T, pos.T, g2, normf_g.reshape(1, -1), eo)
    return out.reshape(x.shape)
```

```python
import functools

import jax
import jax.numpy as jnp
from jax import lax
from jax.experimental import pallas as pl
from jax.experimental.pallas import tpu as pltpu

F32 = jnp.float32
BF16 = jnp.bfloat16
I32 = jnp.int32

D_MODEL = 2048
SEQ = 8192
NORM_EPS = 1e-6
SSD_HEADS = 32
SSD_HEAD_DIM = 64
SSD_INNER = 2048
SSD_GROUPS = 4
SSD_STATE = 128
SSD_CHUNK = 128
HEADS_PER_GROUP = SSD_HEADS // SSD_GROUPS
GROUP_COLS = HEADS_PER_GROUP * SSD_HEAD_DIM
GLA_HEADS = 4
GLA_KEY_DIM = 1024
GLA_VAL_DIM = 2048
GLA_HEAD_K = 256
GLA_HEAD_V = 512
GLA_GATE_RANK = 16
GLA_GATE_NORM = 16.0
GLA_CHUNK = 64
GLA_STEP = 128
N_EXPERTS = 64
TOP_K = 8
N_EXPERT_GROUPS = 8
TOPK_GROUPS = 4
EXPERT_DIM = 512
ROUTED_SCALE = 2.5

VMEM_LIMIT_BYTES = 56 * 1024 * 1024
SUBLANES = 8

COL_Z, COL_XS, COL_BC, COL_Q, COL_V, COL_R, COL_MA, COL_MB, COL_K = (
    0, 2048, 4096, 5120, 6144, 8192, 10240, 12288, 14336)
PROJ_COLS = 15360
SMALL_COLS = 128

MOE_BM = 512
MOE_NB = SEQ * TOP_K // MOE_BM + N_EXPERTS
MOE_ROWS = MOE_NB * MOE_BM


def _cparams(sem):
    return pltpu.CompilerParams(dimension_semantics=sem, vmem_limit_bytes=VMEM_LIMIT_BYTES)


def _split3(a):
    hi = a.astype(BF16)
    r1 = a - hi.astype(F32)
    mid = r1.astype(BF16)
    lo = (r1 - mid.astype(F32)).astype(BF16)
    return hi, mid, lo


def _dot(a, b):
    return jnp.dot(a, b, preferred_element_type=F32)


def _dot_nt(a, b):
    return lax.dot_general(a, b, (((1,), (1,)), ((), ())), preferred_element_type=F32)


def _dot_tn(a, b):
    return lax.dot_general(a, b, (((0,), (0,)), ((), ())), preferred_element_type=F32)


def _sel_dot(sel_bf16, a):
    hi, mid, lo = _split3(a)
    return _dot(sel_bf16, hi) + _dot(sel_bf16, mid) + _dot(sel_bf16, lo)


def _dot_sel(a, sel_bf16):
    hi, mid, lo = _split3(a)
    return _dot(hi, sel_bf16) + _dot(mid, sel_bf16) + _dot(lo, sel_bf16)


def _dot_sel_stacked(a, sel3):
    hi = a.astype(BF16).astype(F32)
    r1 = a - hi
    mid = r1.astype(BF16).astype(F32)
    lo = (r1 - mid).astype(BF16).astype(F32)
    return _dot(jnp.concatenate([hi, mid, lo], axis=1).astype(BF16), sel3)


def _sel_dot_stacked(sel3, a):
    hi, mid, lo = _split3(a)
    return _dot(sel3, jnp.concatenate([hi, mid, lo], axis=0))


def _dot3(a, b):
    ah = a.astype(BF16)
    al = (a - ah.astype(F32)).astype(BF16)
    bh = b.astype(BF16)
    bl = (b - bh.astype(F32)).astype(BF16)
    return _dot(ah, bh) + _dot(ah, bl) + _dot(al, bh)


def _dot3_nt(a, b):
    ah = a.astype(BF16)
    al = (a - ah.astype(F32)).astype(BF16)
    bh = b.astype(BF16)
    bl = (b - bh.astype(F32)).astype(BF16)
    return _dot_nt(ah, bh) + _dot_nt(ah, bl) + _dot_nt(al, bh)


def _sigmoid(x):
    return 0.5 * jnp.tanh(0.5 * x) + 0.5


def _silu(x):
    h = 0.5 * x
    return h + h * jnp.tanh(h)


def _softplus(x):
    return jnp.maximum(x, 0.0) + jnp.log(1.0 + jnp.exp(-jnp.abs(x)))


def _log_sigmoid(x):
    return jnp.minimum(x, 0.0) - jnp.log(1.0 + jnp.exp(-jnp.abs(x)))


def _rms(x, g):
    ms = jnp.mean(x * x, axis=-1, keepdims=True)
    return x * lax.rsqrt(ms + NORM_EPS) * g


ADA_TN = 1024


def _ada_kernel(c_ref, w_ref, b_ref, o_ref):
    ca = _silu(c_ref[...])
    o_ref[...] = jnp.sum(w_ref[...] * ca, axis=0, keepdims=True) + b_ref[...]


def _ada(c_col, w_ada, b_ada):
    n = w_ada.shape[1]
    return pl.pallas_call(
        _ada_kernel,
        grid=(n // ADA_TN,),
        in_specs=[
            pl.BlockSpec((D_MODEL, 1), lambda j: (0, 0)),
            pl.BlockSpec((D_MODEL, ADA_TN), lambda j: (0, j)),
            pl.BlockSpec((1, ADA_TN), lambda j: (0, j)),
        ],
        out_specs=pl.BlockSpec((1, ADA_TN), lambda j: (0, j)),
        out_shape=jax.ShapeDtypeStruct((1, n), F32),
        compiler_params=_cparams(("arbitrary",)),
        name="ada_mod",
    )(c_col, w_ada, b_ada)


IN_DIM = 15408
PACK_TN = 1024
W_IN_SEGMENTS = ((0, 5120), (5152, 1024), (7200, 4096), (11312, 4096), (6176, 1024))
W_IN_DT_COL = 5120
W_IN_GLR_COL = 11296
PACK_STARTS = tuple(s + t * PACK_TN for s, w in W_IN_SEGMENTS for t in range(w // PACK_TN))


def _pack_kernel(start_ref, dt_ref, glr_ref, wt_hbm, o_ref, small_ref, win_ref, sem):
    j = pl.program_id(0)
    slot = j % 2

    @pl.when(j == 0)
    def _():
        small_ref[...] = jnp.zeros_like(small_ref)
        small_ref[0:SSD_HEADS, :] = dt_ref[...]
        small_ref[SSD_HEADS:SSD_HEADS + GLA_GATE_RANK, :] = glr_ref[...]

    def window_copy(t, s):
        row = pl.multiple_of(start_ref[t], 2 * SUBLANES)
        return pltpu.make_async_copy(wt_hbm.at[pl.ds(row, PACK_TN), :], win_ref.at[s], sem.at[s])

    @pl.when(j == 0)
    def _():
        window_copy(0, 0).start()

    @pl.when(j + 1 < pl.num_programs(0))
    def _():
        window_copy(j + 1, 1 - slot).start()

    window_copy(j, slot).wait()
    o_ref[...] = win_ref[slot].T.astype(BF16)


def _pack_w_in(w_in_t):
    starts = jnp.asarray(PACK_STARTS, I32)
    grid_spec = pltpu.PrefetchScalarGridSpec(
        num_scalar_prefetch=1,
        grid=(len(PACK_STARTS),),
        in_specs=[pl.BlockSpec((SSD_HEADS, D_MODEL), lambda j, s: (W_IN_DT_COL // SSD_HEADS, 0)),
                  pl.BlockSpec((GLA_GATE_RANK, D_MODEL), lambda j, s: (W_IN_GLR_COL // GLA_GATE_RANK, 0)),
                  pl.BlockSpec(memory_space=pl.ANY)],
        out_specs=[pl.BlockSpec((D_MODEL, PACK_TN), lambda j, s: (0, j)),
                   pl.BlockSpec((SMALL_COLS, D_MODEL), lambda j, s: (0, 0))],
        scratch_shapes=[pltpu.VMEM((2, PACK_TN, D_MODEL), F32), pltpu.SemaphoreType.DMA((2,))],
    )
    return pl.pallas_call(
        _pack_kernel,
        grid_spec=grid_spec,
        out_shape=[jax.ShapeDtypeStruct((D_MODEL, PROJ_COLS), BF16),
                   jax.ShapeDtypeStruct((SMALL_COLS, D_MODEL), F32)],
        compiler_params=_cparams(("arbitrary",)),
        name="pack_w_in",
    )(starts, w_in_t, w_in_t, w_in_t)


INPROJ_TM = 1024
INPROJ_TN = 1024


def _inproj_kernel(x_ref, g_ref, sc_ref, sh_ref, wb_ref, ws_ref, o_ref, os_ref, h_ref):
    @pl.when(pl.program_id(1) == 0)
    def _():
        h = _rms(x_ref[...], g_ref[...]) * (1.0 + sc_ref[...]) + sh_ref[...]
        hb = h.astype(BF16)
        h_ref[...] = hb
        os_ref[...] = _dot_nt(hb, ws_ref[...].astype(BF16))

    o_ref[...] = _dot(h_ref[...], wb_ref[...]).astype(BF16)


def _inproj(x2d, g, sc, sh, w_big, w_small):
    vec = pl.BlockSpec((1, D_MODEL), lambda i, j: (0, 0))
    return pl.pallas_call(
        _inproj_kernel,
        grid=(SEQ // INPROJ_TM, PROJ_COLS // INPROJ_TN),
        in_specs=[
            pl.BlockSpec((INPROJ_TM, D_MODEL), lambda i, j: (i, 0)),
            vec, vec, vec,
            pl.BlockSpec((D_MODEL, INPROJ_TN), lambda i, j: (0, j)),
            pl.BlockSpec((SMALL_COLS, D_MODEL), lambda i, j: (0, 0)),
        ],
        out_specs=[
            pl.BlockSpec((INPROJ_TM, INPROJ_TN), lambda i, j: (i, j)),
            pl.BlockSpec((INPROJ_TM, SMALL_COLS), lambda i, j: (i, 0)),
        ],
        out_shape=[
            jax.ShapeDtypeStruct((SEQ, PROJ_COLS), BF16),
            jax.ShapeDtypeStruct((SEQ, SMALL_COLS), F32),
        ],
        scratch_shapes=[pltpu.VMEM((INPROJ_TM, D_MODEL), BF16)],
        compiler_params=_cparams(("arbitrary", "arbitrary")),
        name="inproj",
    )(x2d, g, sc, sh, w_big, w_small)


def _causal_conv_silu(cur, hist_ref, w_ref, b_ref):
    L = cur.shape[0]
    hist_ref[SUBLANES:, :] = cur
    acc = cur * w_ref[3:4, :] + b_ref[...]
    for s in (1, 2, 3):
        acc = acc + hist_ref[SUBLANES - s:SUBLANES - s + L, :] * w_ref[3 - s:4 - s, :]
    hist_ref[:SUBLANES, :] = cur[L - SUBLANES:]
    return _silu(acc)


def _ssd_chunk(xs_ref, bc_ref, z_ref, ma_ref, sm_ref, cwx_ref, cwb_ref, cbx_ref, cbb_ref,
               dtb_ref, alog_ref, dexp_ref, ng_ref, px_ref, pb_ref, st_ref):
    L = SSD_CHUNK
    xs_raw = xs_ref[...].astype(F32)
    bc_raw = bc_ref[...].astype(F32)
    xs = _causal_conv_silu(xs_raw, px_ref, cwx_ref, cbx_ref)
    bc = _causal_conv_silu(bc_raw, pb_ref, cwb_ref, cbb_ref)

    dt = _softplus(sm_ref[:, 0:SSD_HEADS] + dtb_ref[...])
    d_a = dt * (-jnp.exp(alog_ref[...]))
    ri = lax.broadcasted_iota(I32, (L, L), 0)
    ci = lax.broadcasted_iota(I32, (L, L), 1)
    causal = ri >= ci
    tril = jnp.where(causal, 1.0, 0.0).astype(BF16)
    acum = _sel_dot(tril, d_a)
    hi32 = lax.broadcasted_iota(I32, (SSD_HEADS, SSD_HEADS), 0)
    hj32 = lax.broadcasted_iota(I32, (SSD_HEADS, SSD_HEADS), 1)
    eye32 = jnp.where(hi32 == hj32, 1.0, 0.0).astype(BF16)
    a_hi, a_mid, a_lo = _split3(acum)
    acum_t = _dot_nt(eye32, a_hi) + _dot_nt(eye32, a_mid) + _dot_nt(eye32, a_lo)
    last = acum[L - 1:L, :]
    e_a = jnp.exp(acum)
    to_end = jnp.exp(last - acum)
    eh = lax.broadcasted_iota(I32, (SSD_HEADS, SSD_INNER), 0)
    ec = lax.broadcasted_iota(I32, (SSD_HEADS, SSD_INNER), 1)
    expand = jnp.where((ec >> 6) == eh, 1.0, 0.0).astype(BF16)
    expand3 = jnp.concatenate([expand, expand, expand], axis=0)
    dt_x = _dot_sel_stacked(dt, expand3)
    dte_x = _dot_sel_stacked(dt * to_end, expand3)
    ea_x = _dot_sel_stacked(e_a, expand3)
    cd_x = ea_x[L - 1:L, :]

    xdt = xs * dt_x
    lane = lax.broadcasted_iota(I32, xdt.shape, 1)
    left = (lane & 64) == 0
    xdt_l = jnp.where(left, xdt, 0.0).astype(BF16)
    xdt_r = jnp.where(left, 0.0, xdt).astype(BF16)
    xdte = (xs * dte_x).astype(BF16)

    y_diag = []
    y_off = []
    for g in range(SSD_GROUPS):
        bg = bc[:, g * SSD_STATE:(g + 1) * SSD_STATE].astype(BF16)
        cg = bc[:, 512 + g * SSD_STATE:512 + (g + 1) * SSD_STATE].astype(BF16)
        cb = _dot_nt(cg, bg)
        gsl = slice(g * GROUP_COLS, (g + 1) * GROUP_COLS)
        st = st_ref[g]
        y_off.append(_dot(cg, st.astype(BF16)))
        st_ref[g] = st * cd_x[:, gsl] + _dot_tn(bg, xdte[:, gsl])
        for p in range(HEADS_PER_GROUP // 2):
            ms = []
            for h in (g * HEADS_PER_GROUP + 2 * p, g * HEADS_PER_GROUP + 2 * p + 1):
                seg = acum[:, h:h + 1] - acum_t[h:h + 1, :]
                decay = jnp.exp(jnp.where(causal, seg, -jnp.inf))
                ms.append((cb * decay).astype(BF16))
            m_cat = jnp.concatenate(ms, axis=1)
            psl = slice((g * 4 + p) * 128, (g * 4 + p + 1) * 128)
            x_cat = jnp.concatenate([xdt_l[:, psl], xdt_r[:, psl]], axis=0)
            y_diag.append(_dot(m_cat, x_cat))

    y = (jnp.concatenate(y_diag, axis=1) + jnp.concatenate(y_off, axis=1) * ea_x
         + dexp_ref[...] * xs)
    yf = y * _silu(z_ref[...].astype(F32))
    outs = []
    for g in range(SSD_GROUPS):
        seg = yf[:, g * GROUP_COLS:(g + 1) * GROUP_COLS]
        ms = jnp.mean(seg * seg, axis=-1, keepdims=True)
        outs.append(seg * lax.rsqrt(ms + NORM_EPS))
    y_ssd = jnp.concatenate(outs, axis=1) * ng_ref[...]
    return _sigmoid(ma_ref[...].astype(F32)) * y_ssd


def _gla_chunks(q_ref, k_ref, v_ref, r_ref, mb_ref, sm_ref, wg_ref, bg_ref, ng_ref, s_ref, y_ssd, o_ref):
    L = GLA_CHUNK
    ri = lax.broadcasted_iota(I32, (L, L), 0)
    ci = lax.broadcasted_iota(I32, (L, L), 1)
    causal = ri >= ci
    r3 = lax.broadcasted_iota(I32, (L, 3 * L), 0)
    c3 = lax.broadcasted_iota(I32, (L, 3 * L), 1)
    tril3 = jnp.where(r3 >= c3 % L, 1.0, 0.0).astype(BF16)

    for c in range(GLA_STEP // L):
        rs = slice(c * L, (c + 1) * L)
        glr = sm_ref[rs, SSD_HEADS:SSD_HEADS + GLA_GATE_RANK]
        pre = _dot3(glr, wg_ref[...]) + bg_ref[...]
        gk = _log_sigmoid(pre) / GLA_GATE_NORM
        bcum = _sel_dot_stacked(tril3, gk)
        bmid = bcum[L // 2:L // 2 + 1, :]
        blast = bcum[L - 1:L, :]
        q = q_ref[rs, :].astype(F32) * (GLA_HEAD_K ** -0.5)
        k = k_ref[rs, :].astype(F32)
        q_rel = (q * jnp.exp(bcum - bmid)).astype(BF16)
        k_rel = (k * jnp.exp(bmid - bcum)).astype(BF16)
        q_int = (q * jnp.exp(bcum)).astype(BF16)
        k_end = (k * jnp.exp(blast - bcum)).astype(BF16)
        dec = jnp.exp(blast)
        v = v_ref[rs, :]

        outs = []
        for h in range(GLA_HEADS):
            ks = slice(h * GLA_HEAD_K, (h + 1) * GLA_HEAD_K)
            vs = slice(h * GLA_HEAD_V, (h + 1) * GLA_HEAD_V)
            att = jnp.where(causal, _dot_nt(q_rel[:, ks], k_rel[:, ks]), 0.0)
            s_t = s_ref[h]
            o_h = _dot(att.astype(BF16), v[:, vs]) + _dot_nt(q_int[:, ks], s_t.astype(BF16))
            s_ref[h] = s_t * dec[:, ks] + _dot_tn(v[:, vs], k_end[:, ks])
            outs.append(_rms(o_h, ng_ref[...]))
        o = jnp.concatenate(outs, axis=1)
        o = o * _silu(r_ref[rs, :].astype(F32)) * _sigmoid(mb_ref[rs, :].astype(F32))
        o_ref[rs, :] = (y_ssd[rs, :] + o).astype(BF16)


N_SSD_IN = 13
N_GLA_IN = 9


def _mixer_kernel(*refs):
    ssd_in = refs[:N_SSD_IN]
    gla_in = refs[N_SSD_IN:N_SSD_IN + N_GLA_IN]
    o_ref, px_ref, pb_ref, st_ref, s_ref = refs[N_SSD_IN + N_GLA_IN:]

    @pl.when(pl.program_id(0) == 0)
    def _():
        px_ref[...] = jnp.zeros_like(px_ref)
        pb_ref[...] = jnp.zeros_like(pb_ref)
        st_ref[...] = jnp.zeros_like(st_ref)
        s_ref[...] = jnp.zeros_like(s_ref)

    y_ssd = _ssd_chunk(*ssd_in, px_ref, pb_ref, st_ref)
    _gla_chunks(*gla_in, s_ref, y_ssd, o_ref)


def _mixer(proj, small, conv_w, conv_b, dt_bias, a_log, d_exp, ssd_norm_g, w_gate, b_gate, gla_norm_g):
    L = SSD_CHUNK
    full = lambda shape: pl.BlockSpec(shape, lambda i: (0, 0))
    cols = lambda width, col: pl.BlockSpec((L, width), lambda i: (i, col // width))
    ssd_specs = [
        cols(2048, COL_XS), cols(1024, COL_BC), cols(2048, COL_Z), cols(2048, COL_MA),
        pl.BlockSpec((L, SMALL_COLS), lambda i: (i, 0)),
        pl.BlockSpec((4, 2048), lambda i: (0, 0)),
        pl.BlockSpec((4, 1024), lambda i: (0, 2)),
        pl.BlockSpec((1, 2048), lambda i: (0, 0)),
        pl.BlockSpec((1, 1024), lambda i: (0, 2)),
        full((1, SSD_HEADS)), full((1, SSD_HEADS)),
        full((1, SSD_INNER)), full((1, SSD_INNER)),
    ]
    gla_specs = [
        cols(1024, COL_Q), cols(1024, COL_K), cols(2048, COL_V), cols(2048, COL_R), cols(2048, COL_MB),
        pl.BlockSpec((L, SMALL_COLS), lambda i: (i, 0)),
        full((GLA_GATE_RANK, GLA_KEY_DIM)), full((1, GLA_KEY_DIM)), full((1, GLA_HEAD_V)),
    ]
    assert len(ssd_specs) == N_SSD_IN and len(gla_specs) == N_GLA_IN and GLA_STEP == L
    return pl.pallas_call(
        _mixer_kernel,
        grid=(SEQ // L,),
        in_specs=ssd_specs + gla_specs,
        out_specs=pl.BlockSpec((L, D_MODEL), lambda i: (i, 0)),
        out_shape=jax.ShapeDtypeStruct((SEQ, D_MODEL), BF16),
        scratch_shapes=[
            pltpu.VMEM((SUBLANES + L, 2048), F32),
            pltpu.VMEM((SUBLANES + L, 1024), F32),
            pltpu.VMEM((SSD_GROUPS, SSD_STATE, GROUP_COLS), F32),
            pltpu.VMEM((GLA_HEADS, GLA_HEAD_V, GLA_HEAD_K), F32),
        ],
        compiler_params=_cparams(("arbitrary",)),
        name="mixer",
    )(proj, proj, proj, proj, small, conv_w, conv_w, conv_b, conv_b, dt_bias, a_log, d_exp, ssd_norm_g,
      proj, proj, proj, proj, proj, small, w_gate, b_gate, gla_norm_g)


OUTPROJ_TM = 512


def _outproj_kernel(m_ref, x_ref, w_ref, g1_ref, n2_ref, sc_ref, sh_ref, wr_ref,
                    x1_ref, h2_ref, lg_ref):
    x1 = x_ref[...] + g1_ref[...] * _dot(m_ref[...], w_ref[...])
    x1_ref[...] = x1
    h2 = _rms(x1, n2_ref[...]) * (1.0 + sc_ref[...]) + sh_ref[...]
    h2_ref[...] = h2
    lg_ref[...] = _dot3_nt(wr_ref[...], h2)


def _outproj(mixed, x2d, w_out_bf, g1, n2g, sc2, sh2, w_router_t):
    tm = OUTPROJ_TM
    vec = pl.BlockSpec((1, D_MODEL), lambda i: (0, 0))
    row = pl.BlockSpec((tm, D_MODEL), lambda i: (i, 0))
    return pl.pallas_call(
        _outproj_kernel,
        grid=(SEQ // tm,),
        in_specs=[row, row,
                  pl.BlockSpec((D_MODEL, D_MODEL), lambda i: (0, 0), pipeline_mode=pl.Buffered(1)),
                  vec, vec, vec, vec,
                  pl.BlockSpec((N_EXPERTS, D_MODEL), lambda i: (0, 0))],
        out_specs=[row, row, pl.BlockSpec((N_EXPERTS, tm), lambda i: (0, i))],
        out_shape=[jax.ShapeDtypeStruct((SEQ, D_MODEL), F32),
                   jax.ShapeDtypeStruct((SEQ, D_MODEL), F32),
                   jax.ShapeDtypeStruct((N_EXPERTS, SEQ), F32)],
        compiler_params=_cparams(("arbitrary",)),
        name="outproj_router",
    )(mixed, x2d, w_out_bf, g1, n2g, sc2, sh2, w_router_t)


ROUTE_TN = 512
COMBINE_TT = 128
N_TILES = SEQ // COMBINE_TT
TILE_LANES = 128


def _route_kernel(lg_ref, bias_ref, ek_ref, wk_ref, rk_ref, cnt_ref, cb_ref, ct_ref, carry_ref):
    tn = ROUTE_TN
    gsz = N_EXPERTS // N_EXPERT_GROUPS

    @pl.when(pl.program_id(0) == 0)
    def _():
        carry_ref[...] = jnp.zeros_like(carry_ref)
        cb_ref[...] = jnp.zeros_like(cb_ref)
        ct_ref[...] = jnp.zeros_like(ct_ref)

    s = jax.nn.sigmoid(lg_ref[...])
    choice = s + bias_ref[...]
    r8 = lax.broadcasted_iota(I32, (gsz, tn), 0)
    neg = -jnp.inf

    def top1(cur, rows, nrows):
        m = jnp.max(cur, axis=0, keepdims=True)
        idx = jnp.min(jnp.where(cur == m, rows, nrows), axis=0, keepdims=True)
        return m, idx

    gscores = []
    for g in range(N_EXPERT_GROUPS):
        blk = choice[g * gsz:(g + 1) * gsz, :]
        m1, i1 = top1(blk, r8, gsz)
        m2, _ = top1(jnp.where(r8 == i1, neg, blk), r8, gsz)
        gscores.append(m1 + m2)
    cur = jnp.concatenate(gscores, axis=0)
    gsel = jnp.zeros((N_EXPERT_GROUPS, tn), F32)
    for _ in range(TOPK_GROUPS):
        _, idx = top1(cur, r8, N_EXPERT_GROUPS)
        hit = r8 == idx
        gsel = jnp.where(hit, 1.0, gsel)
        cur = jnp.where(hit, neg, cur)
    emask = jnp.concatenate(
        [jnp.broadcast_to(gsel[g:g + 1, :], (gsz, tn)) for g in range(N_EXPERT_GROUPS)], axis=0)
    rows = lax.broadcasted_iota(I32, (N_EXPERTS, tn), 0)
    cur = jnp.where(emask > 0.5, choice, neg)
    sel = jnp.zeros((N_EXPERTS, tn), F32)
    hits, eks, wks = [], [], []
    for _ in range(TOP_K):
        _, idx = top1(cur, rows, N_EXPERTS)
        hit = rows == idx
        hits.append(hit)
        eks.append(idx)
        wks.append(jnp.sum(jnp.where(hit, s, 0.0), axis=0, keepdims=True))
        cur = jnp.where(hit, neg, cur)
        sel = jnp.where(hit, 1.0, sel)
    wsum = wks[0]
    for w in wks[1:]:
        wsum = wsum + w
    scale = ROUTED_SCALE / (wsum + 1e-20)

    ti = lax.broadcasted_iota(I32, (tn, tn), 0)
    tj = lax.broadcasted_iota(I32, (tn, tn), 1)
    before = jnp.where(ti < tj, 1.0, 0.0).astype(BF16)
    rank = _dot(sel.astype(BF16), before) + carry_ref[...]
    rks = [jnp.sum(jnp.where(hit, rank, 0.0), axis=0, keepdims=True) for hit in hits]

    lane = lax.broadcasted_iota(I32, (N_EXPERTS, TILE_LANES), 1)
    carry = carry_ref[...]
    cb = cb_ref[...]
    ct = ct_ref[...]
    for sub in range(tn // COMBINE_TT):
        cnt = jnp.sum(sel[:, sub * COMBINE_TT:(sub + 1) * COMBINE_TT], axis=1, keepdims=True)
        col = pl.program_id(0) * (tn // COMBINE_TT) + sub
        cb = jnp.where(lane == col, carry, cb)
        ct = jnp.where(lane == col, cnt, ct)
        carry = carry + cnt
    cb_ref[...] = cb
    ct_ref[...] = ct
    carry_ref[...] = carry

    ek_ref[...] = jnp.concatenate(eks, axis=0)
    wk_ref[...] = jnp.concatenate(wks, axis=0) * scale
    rk_ref[...] = jnp.concatenate(rks, axis=0).astype(I32)
    cnt_ref[...] = carry_ref[...].astype(I32)


def _route(logits_t, bias_col):
    tn = ROUTE_TN
    kt = pl.BlockSpec((TOP_K, tn), lambda i: (0, i))
    return pl.pallas_call(
        _route_kernel,
        grid=(SEQ // tn,),
        in_specs=[pl.BlockSpec((N_EXPERTS, tn), lambda i: (0, i)),
                  pl.BlockSpec((N_EXPERTS, 1), lambda i: (0, 0))],
        out_specs=[kt, kt, kt, pl.BlockSpec((N_EXPERTS, 1), lambda i: (0, 0)),
                   pl.BlockSpec((N_EXPERTS, TILE_LANES), lambda i: (0, 0)),
                   pl.BlockSpec((N_EXPERTS, TILE_LANES), lambda i: (0, 0))],
        out_shape=[jax.ShapeDtypeStruct((TOP_K, SEQ), I32),
                   jax.ShapeDtypeStruct((TOP_K, SEQ), F32),
                   jax.ShapeDtypeStruct((TOP_K, SEQ), I32),
                   jax.ShapeDtypeStruct((N_EXPERTS, 1), I32),
                   jax.ShapeDtypeStruct((N_EXPERTS, TILE_LANES), F32),
                   jax.ShapeDtypeStruct((N_EXPERTS, TILE_LANES), F32)],
        scratch_shapes=[pltpu.VMEM((N_EXPERTS, 1), F32)],
        compiler_params=_cparams(("arbitrary",)),
        name="route",
    )(logits_t, bias_col)


def _dest_kernel(ek_ref, rk_ref, ps_ref, shift_ref, o_ref, pos_ref):
    tt = COMBINE_TT
    rows = lax.broadcasted_iota(I32, (N_EXPERTS, tt), 0)
    lane = lax.broadcasted_iota(I32, (N_EXPERTS, TILE_LANES), 1)
    dests, poss = [], []
    for sub in range(ROUTE_TN // tt):
        tile = pl.program_id(0) * (ROUTE_TN // tt) + sub
        shift_col = jnp.sum(jnp.where(lane == tile, shift_ref[...], 0), axis=1, keepdims=True)
        ds, ps = [], []
        for k in range(TOP_K):
            hit = rows == ek_ref[k:k + 1, sub * tt:(sub + 1) * tt]
            d = (jnp.sum(jnp.where(hit, ps_ref[...], 0), axis=0, keepdims=True)
                 + rk_ref[k:k + 1, sub * tt:(sub + 1) * tt])
            ds.append(d)
            ps.append(d + jnp.sum(jnp.where(hit, shift_col, 0), axis=0, keepdims=True))
        dests.append(jnp.concatenate(ds, axis=0))
        poss.append(jnp.concatenate(ps, axis=0))
    o_ref[...] = jnp.concatenate(dests, axis=1)
    pos_ref[...] = jnp.concatenate(poss, axis=1)


def _dest(ek, rk, pstarts_col, shift):
    tn = ROUTE_TN
    kt = pl.BlockSpec((TOP_K, tn), lambda i: (0, i))
    return pl.pallas_call(
        _dest_kernel,
        grid=(SEQ // tn,),
        in_specs=[kt, kt, pl.BlockSpec((N_EXPERTS, 1), lambda i: (0, 0)),
                  pl.BlockSpec((N_EXPERTS, TILE_LANES), lambda i: (0, 0))],
        out_specs=[kt, kt],
        out_shape=[jax.ShapeDtypeStruct((TOP_K, SEQ), I32), jax.ShapeDtypeStruct((TOP_K, SEQ), I32)],
        compiler_params=_cparams(("arbitrary",)),
        name="dest_rows",
    )(ek, rk, pstarts_col, shift)


DISPATCH_TT = 256


def _dispatch_kernel(dest_ref, padoff_ref, padlen_ref, h_ref, out_ref, zeros_ref, sem, zsem):
    tt = DISPATCH_TT
    i = pl.program_id(0)
    base = i * tt

    def issue(t, carry):
        for k in range(TOP_K):
            d = dest_ref[k * SEQ + base + t]
            pltpu.make_async_copy(h_ref.at[pl.ds(t, 1), :], out_ref.at[pl.ds(d, 1), :],
                                  sem).start(priority=k % 2)
        return carry

    lax.fori_loop(0, tt, issue, 0)

    def pad_copy(off, n):
        return pltpu.make_async_copy(zeros_ref.at[pl.ds(0, n), :], out_ref.at[pl.ds(off, n), :], zsem)

    def pad_rows(wait):
        def body(e, carry):
            start = padoff_ref[e]
            head = (-start) & (SUBLANES - 1)
            for j in range(SUBLANES - 1):
                @pl.when(j < head)
                def _():
                    cp = pad_copy(start + j, 1)
                    cp.wait() if wait else cp.start()
            n = pl.multiple_of(padlen_ref[e] - head, SUBLANES)

            @pl.when(n > 0)
            def _():
                cp = pad_copy(pl.multiple_of(start + head, SUBLANES), n)
                cp.wait() if wait else cp.start()
            return carry
        lax.fori_loop(0, N_EXPERTS, body, 0)

    @pl.when(i == 0)
    def _():
        zeros_ref[...] = jnp.zeros_like(zeros_ref)
        pad_rows(wait=False)

    @pl.when(i == pl.num_programs(0) - 1)
    def _():
        pad_rows(wait=True)

    for _ in range(TOP_K):
        pltpu.make_async_copy(h_ref, out_ref.at[pl.ds(0, tt), :], sem).wait()


def _dispatch(dest_flat, pad_off, pad_len, h2):
    tt = DISPATCH_TT
    grid_spec = pltpu.PrefetchScalarGridSpec(
        num_scalar_prefetch=3,
        grid=(SEQ // tt,),
        in_specs=[pl.BlockSpec((tt, D_MODEL), lambda i, d, po, pn: (i, 0))],
        out_specs=pl.BlockSpec(memory_space=pl.ANY),
        scratch_shapes=[
            pltpu.VMEM((MOE_BM, D_MODEL), F32),
            pltpu.SemaphoreType.DMA,
            pltpu.SemaphoreType.DMA,
        ],
    )
    return pl.pallas_call(
        _dispatch_kernel,
        grid_spec=grid_spec,
        out_shape=jax.ShapeDtypeStruct((MOE_ROWS, D_MODEL), F32),
        compiler_params=_cparams(("arbitrary",)),
        name="dispatch",
    )(dest_flat, pad_off, pad_len, h2)


def _experts_kernel(bexp_ref, first_ref, slot_ref, next_ref, nused_ref,
                    x_ref, wg_hbm, wu_hbm, wd_hbm, o_ref,
                    wgf_ref, wuf_ref, wdf_ref, wgb_ref, wub_ref, wdb_ref, sem):
    b = pl.program_id(0)
    active = b < nused_ref[0]

    def weight_copies(e, s):
        return (pltpu.make_async_copy(wg_hbm.at[e], wgf_ref.at[s], sem.at[s]),
                pltpu.make_async_copy(wu_hbm.at[e], wuf_ref.at[s], sem.at[s]),
                pltpu.make_async_copy(wd_hbm.at[e], wdf_ref.at[s], sem.at[s]))

    @pl.when(jnp.logical_and(active, b == 0))
    def _():
        for cp in weight_copies(bexp_ref[0], 0):
            cp.start(priority=1)

    @pl.when(jnp.logical_and(active, first_ref[b] == 1))
    def _():
        s = slot_ref[b]
        nxt = next_ref[b]

        @pl.when(nxt < N_EXPERTS)
        def _():
            for cp in weight_copies(nxt, 1 - s):
                cp.start(priority=1)

        for cp in weight_copies(bexp_ref[b], s):
            cp.wait()
        wgb_ref[...] = wgf_ref[s].astype(BF16)
        wub_ref[...] = wuf_ref[s].astype(BF16)
        wdb_ref[...] = wdf_ref[s].astype(BF16)

    @pl.when(active)
    def _():
        x = x_ref[...].astype(BF16)
        gate = _dot(x, wgb_ref[...])
        up = _dot(x, wub_ref[...])
        hid = (_silu(gate) * up).astype(BF16)
        o_ref[...] = _dot(hid, wdb_ref[...]).astype(BF16)


def _experts(block_exp, first, slot, next_exp, nused, xs_perm, w_gate, w_up, w_down):
    bm = MOE_BM

    def xmap(b, be, fi, sl, ne, nu):
        return (jnp.minimum(b, nu[0] - 1), 0)

    hbm = pl.BlockSpec(memory_space=pl.ANY)
    grid_spec = pltpu.PrefetchScalarGridSpec(
        num_scalar_prefetch=5,
        grid=(MOE_NB,),
        in_specs=[pl.BlockSpec((bm, D_MODEL), xmap), hbm, hbm, hbm],
        out_specs=pl.BlockSpec((bm, D_MODEL), xmap),
        scratch_shapes=[
            pltpu.VMEM((2, D_MODEL, EXPERT_DIM), F32),
            pltpu.VMEM((2, D_MODEL, EXPERT_DIM), F32),
            pltpu.VMEM((2, EXPERT_DIM, D_MODEL), F32),
            pltpu.VMEM((D_MODEL, EXPERT_DIM), BF16),
            pltpu.VMEM((D_MODEL, EXPERT_DIM), BF16),
            pltpu.VMEM((EXPERT_DIM, D_MODEL), BF16),
            pltpu.SemaphoreType.DMA((2,)),
        ],
    )
    return pl.pallas_call(
        _experts_kernel,
        grid_spec=grid_spec,
        out_shape=jax.ShapeDtypeStruct((MOE_ROWS, D_MODEL), BF16),
        compiler_params=_cparams(("arbitrary",)),
        name="experts",
    )(block_exp, first, slot, next_exp, nused, xs_perm, w_gate, w_up, w_down)


SHARED_TM = 512


def _shared_kernel(h2_ref, wg_ref, wu_ref, wd_ref, o_ref):
    h = h2_ref[...].astype(BF16)
    hid = (_silu(_dot(h, wg_ref[...])) * _dot(h, wu_ref[...])).astype(BF16)
    o_ref[...] = _dot(hid, wd_ref[...])


def _shared(h2, wsg, wsu, wsd):
    tm = SHARED_TM
    row = pl.BlockSpec((tm, D_MODEL), lambda i: (i, 0))
    return pl.pallas_call(
        _shared_kernel,
        grid=(SEQ // tm,),
        in_specs=[row,
                  pl.BlockSpec((D_MODEL, EXPERT_DIM), lambda i: (0, 0)),
                  pl.BlockSpec((D_MODEL, EXPERT_DIM), lambda i: (0, 0)),
                  pl.BlockSpec((EXPERT_DIM, D_MODEL), lambda i: (0, 0))],
        out_specs=row,
        out_shape=jax.ShapeDtypeStruct((SEQ, D_MODEL), F32),
        compiler_params=_cparams(("arbitrary",)),
        name="shared_expert",
    )(h2, wsg, wsu, wsd)


COMBINE_ALIGN = 16
COMBINE_CHUNK = 512
COMBINE_ROWS = pl.cdiv(COMBINE_TT * TOP_K + 2 * (COMBINE_ALIGN - 1) * N_EXPERTS,
                       COMBINE_CHUNK) * COMBINE_CHUNK
COMBINE_MIN_CHUNKS = pl.cdiv(COMBINE_TT * TOP_K + (COMBINE_ALIGN - 1) * N_EXPERTS, COMBINE_CHUNK)


def _combine_kernel(ws_ref, wl_ref, so_ref, ysh_ref, x1_ref, wk_ref, pos_ref, g2_ref, nf_ref, eo_ref,
                    o_ref, buf_ref, acc_ref, sem):
    tt = COMBINE_TT
    i = pl.program_id(0)
    slot = i % 2

    def start_windows(tile, slot_):
        def body(e, carry):
            idx = tile * N_EXPERTS + e
            n = pl.multiple_of(wl_ref[idx], COMBINE_ALIGN)
            src = pl.multiple_of(ws_ref[idx], COMBINE_ALIGN)
            dst = pl.multiple_of(slot_ * COMBINE_ROWS + so_ref[idx], COMBINE_ALIGN)
            pltpu.make_async_copy(eo_ref.at[pl.ds(src, n), :], buf_ref.at[pl.ds(dst, n), :],
                                  sem.at[slot_]).start()
            return carry

        lax.fori_loop(0, N_EXPERTS, body, 0)

    @pl.when(i == 0)
    def _():
        buf_ref[...] = jnp.zeros_like(buf_ref)
        start_windows(0, 0)

    @pl.when(i + 1 < pl.num_programs(0))
    def _():
        start_windows(i + 1, 1 - slot)

    last = i * N_EXPERTS + N_EXPERTS - 1
    staged = pl.multiple_of(so_ref[last] + wl_ref[last], COMBINE_ALIGN)
    sbase = pl.multiple_of(slot * COMBINE_ROWS, COMBINE_ROWS)
    pltpu.make_async_copy(eo_ref.at[pl.ds(0, staged), :], buf_ref.at[pl.ds(sbase, staged), :],
                          sem.at[slot]).wait()
    wk = wk_ref[...]
    pos = pos_ref[...]
    def routed(c):
        col = lax.broadcasted_iota(I32, (tt, COMBINE_CHUNK), 1) + c * COMBINE_CHUNK
        w = jnp.zeros((tt, COMBINE_CHUNK), F32)
        for k in range(TOP_K):
            w = w + jnp.where(col == pos[:, k:k + 1], wk[:, k:k + 1], 0.0)
        w_hi = w.astype(BF16)
        w_lo = (w - w_hi.astype(F32)).astype(BF16)
        rows = buf_ref[pl.ds(sbase + c * COMBINE_CHUNK, COMBINE_CHUNK), :]
        both = _dot(jnp.concatenate([w_hi, w_lo], axis=0), rows)
        return both[:tt] + both[tt:]

    acc = ysh_ref[...]
    for c in range(COMBINE_MIN_CHUNKS):
        acc = acc + routed(c)
    acc_ref[...] = acc
    for c in range(COMBINE_MIN_CHUNKS, COMBINE_ROWS // COMBINE_CHUNK):
        @pl.when(c * COMBINE_CHUNK < staged)
        def _():
            acc_ref[...] += routed(c)
    x2 = x1_ref[...] + g2_ref[...] * acc_ref[...]
    o_ref[...] = _rms(x2, nf_ref[...])


def _combine(win_start, win_len, stage_off, y_shared, x1, wk_t, pos_t, g2, nf, eo):
    tt = COMBINE_TT
    vec = pl.BlockSpec((1, D_MODEL), lambda i, a, b, c: (0, 0))
    row = pl.BlockSpec((tt, D_MODEL), lambda i, a, b, c: (i, 0))
    slots = pl.BlockSpec((tt, TOP_K), lambda i, a, b, c: (i, 0))
    grid_spec = pltpu.PrefetchScalarGridSpec(
        num_scalar_prefetch=3,
        grid=(N_TILES,),
        in_specs=[row, row, slots, slots, vec, vec, pl.BlockSpec(memory_space=pl.ANY)],
        out_specs=row,
        scratch_shapes=[
            pltpu.VMEM((2 * COMBINE_ROWS, D_MODEL), BF16),
            pltpu.VMEM((tt, D_MODEL), F32),
            pltpu.SemaphoreType.DMA((2,)),
        ],
    )
    return pl.pallas_call(
        _combine_kernel,
        grid_spec=grid_spec,
        out_shape=jax.ShapeDtypeStruct((SEQ, D_MODEL), F32),
        compiler_params=_cparams(("arbitrary",)),
        name="combine_final",
    )(win_start, win_len, stage_off, y_shared, x1, wk_t, pos_t, g2, nf, eo)


def kernel(x, c, w_ada, b_ada, norm1_g, w_in, conv_w, conv_b, dt_bias, a_log, d_skip, ssd_norm_g, gla_w_gate, gla_b_gate, gla_norm_g, w_out, norm2_g, w_router, router_bias, w_e_gate, w_e_up, w_e_down, w_s_gate, w_s_up, w_s_down, normf_g):
    layer = 0
    x2d = x.reshape(SEQ, D_MODEL)
    mod = _ada(c.reshape(D_MODEL, 1), w_ada[layer], b_ada[layer].reshape(1, -1))
    sh1, sc1, g1, sh2, sc2, g2 = [mod[:, i * D_MODEL:(i + 1) * D_MODEL] for i in range(6)]

    w_big, w_small = _pack_w_in(jnp.swapaxes(w_in[layer], 0, 1))

    proj, small = _inproj(x2d, norm1_g[layer].reshape(1, -1), sc1, sh1, w_big, w_small)

    d_exp = jnp.repeat(d_skip[layer], SSD_HEAD_DIM).reshape(1, -1)
    mixed = _mixer(proj, small, conv_w[layer], conv_b[layer].reshape(1, -1),
                   dt_bias[layer].reshape(1, -1), a_log[layer].reshape(1, -1), d_exp,
                   ssd_norm_g[layer].reshape(1, -1), gla_w_gate[layer],
                   gla_b_gate[layer].reshape(1, -1), gla_norm_g[layer].reshape(1, -1))

    x1, h2, logits_t = _outproj(mixed, x2d, w_out[layer].astype(BF16), g1,
                                norm2_g[layer].reshape(1, -1), sc2, sh2, w_router[layer].T)

    ek, wk, rk, counts, cbf, ctf = _route(logits_t, router_bias[layer].reshape(-1, 1))

    counts = counts.reshape(-1)
    padded = (counts + MOE_BM - 1) // MOE_BM * MOE_BM
    pends = jnp.cumsum(padded)
    pstarts = pends - padded
    block_start = jnp.arange(MOE_NB, dtype=I32) * MOE_BM
    block_exp = jnp.minimum(
        jnp.sum((pends[None, :] <= block_start[:, None]).astype(I32), axis=1), N_EXPERTS - 1)
    nused = (pends[-1:] // MOE_BM).astype(I32)
    bidx = jnp.arange(MOE_NB, dtype=I32)
    prev_exp = jnp.concatenate([jnp.full((1,), -1, I32), block_exp[:-1]])
    first = jnp.logical_and(bidx < nused[0], block_exp != prev_exp)
    slot = (jnp.cumsum(first.astype(I32)) - 1) & 1
    later_first = jnp.logical_and(first[None, :], bidx[None, :] > bidx[:, None])
    next_first = jnp.min(jnp.where(later_first, bidx[None, :], MOE_NB), axis=1)
    next_exp = jnp.sum(jnp.where(bidx[None, :] == next_first[:, None], block_exp[None, :], 0), axis=1)
    next_exp = jnp.where(next_first < MOE_NB, next_exp, N_EXPERTS).astype(I32)

    cb = cbf[:, :N_TILES].astype(I32)
    ct = ctf[:, :N_TILES].astype(I32)
    run_start = pstarts[:, None].astype(I32) + cb
    win_start = jnp.where(ct > 0, run_start & -COMBINE_ALIGN, 0)
    win_end = (run_start + ct + COMBINE_ALIGN - 1) & -COMBINE_ALIGN
    win_len = jnp.where(ct > 0, win_end - win_start, COMBINE_ALIGN)
    stage_off = jnp.cumsum(win_len, axis=0) - win_len
    shift = jnp.pad(stage_off - win_start, ((0, 0), (0, TILE_LANES - N_TILES)))
    tile_major = lambda a: a.T.reshape(-1).astype(I32)

    dest, pos = _dest(ek, rk, pstarts.reshape(-1, 1).astype(I32), shift)
    xs_perm = _dispatch(dest.reshape(-1), (pstarts + counts).astype(I32), (padded - counts).astype(I32), h2)
    eo = _experts(block_exp.astype(I32), first.astype(I32), slot.astype(I32), next_exp, nused, xs_perm,
                  w_e_gate[layer], w_e_up[layer], w_e_down[layer])
    y_shared = _shared(h2, w_s_gate[layer].astype(BF16), w_s_up[layer].astype(BF16),
                       w_s_down[layer].astype(BF16))
    out = _combine(tile_major(win_start), tile_major(win_len), tile_major(stage_off), y_shared, x1,
                   wk.T, pos.T, g2, normf_g.reshape(1, -1), eo)
    return out.reshape(x.shape)
```

```python
import functools

import jax
import jax.numpy as jnp
from jax import lax
from jax.experimental import pallas as pl
from jax.experimental.pallas import tpu as pltpu

F32 = jnp.float32
BF16 = jnp.bfloat16
I32 = jnp.int32

D_MODEL = 2048
SEQ = 8192
NORM_EPS = 1e-6
SSD_HEADS = 32
SSD_HEAD_DIM = 64
SSD_INNER = 2048
SSD_GROUPS = 4
SSD_STATE = 128
SSD_CHUNK = 128
HEADS_PER_GROUP = SSD_HEADS // SSD_GROUPS
GROUP_COLS = HEADS_PER_GROUP * SSD_HEAD_DIM
GLA_HEADS = 4
GLA_KEY_DIM = 1024
GLA_VAL_DIM = 2048
GLA_HEAD_K = 256
GLA_HEAD_V = 512
GLA_GATE_RANK = 16
GLA_GATE_NORM = 16.0
GLA_CHUNK = 64
GLA_STEP = 128
N_EXPERTS = 64
TOP_K = 8
N_EXPERT_GROUPS = 8
TOPK_GROUPS = 4
EXPERT_DIM = 512
ROUTED_SCALE = 2.5

VMEM_LIMIT_BYTES = 56 * 1024 * 1024
SUBLANES = 8

COL_Z, COL_XS, COL_BC, COL_Q, COL_V, COL_R, COL_MA, COL_MB, COL_K = (
    0, 2048, 4096, 5120, 6144, 8192, 10240, 12288, 14336)
PROJ_COLS = 15360
SMALL_COLS = 128

MOE_BM = 512
MOE_NB = SEQ * TOP_K // MOE_BM + N_EXPERTS
MOE_ROWS = MOE_NB * MOE_BM


def _cparams(sem):
    return pltpu.CompilerParams(dimension_semantics=sem, vmem_limit_bytes=VMEM_LIMIT_BYTES)


def _split3(a):
    hi = a.astype(BF16)
    r1 = a - hi.astype(F32)
    mid = r1.astype(BF16)
    lo = (r1 - mid.astype(F32)).astype(BF16)
    return hi, mid, lo


def _dot(a, b):
    return jnp.dot(a, b, preferred_element_type=F32)


def _dot_nt(a, b):
    return lax.dot_general(a, b, (((1,), (1,)), ((), ())), preferred_element_type=F32)


def _dot_tn(a, b):
    return lax.dot_general(a, b, (((0,), (0,)), ((), ())), preferred_element_type=F32)


def _sel_dot(sel_bf16, a):
    hi, mid, lo = _split3(a)
    return _dot(sel_bf16, hi) + _dot(sel_bf16, mid) + _dot(sel_bf16, lo)


def _dot_sel(a, sel_bf16):
    hi, mid, lo = _split3(a)
    return _dot(hi, sel_bf16) + _dot(mid, sel_bf16) + _dot(lo, sel_bf16)


def _dot_sel_stacked(a, sel3):
    hi = a.astype(BF16).astype(F32)
    r1 = a - hi
    mid = r1.astype(BF16).astype(F32)
    lo = (r1 - mid).astype(BF16).astype(F32)
    return _dot(jnp.concatenate([hi, mid, lo], axis=1).astype(BF16), sel3)


def _sel_dot_stacked(sel3, a):
    hi, mid, lo = _split3(a)
    return _dot(sel3, jnp.concatenate([hi, mid, lo], axis=0))


def _dot3(a, b):
    ah = a.astype(BF16)
    al = (a - ah.astype(F32)).astype(BF16)
    bh = b.astype(BF16)
    bl = (b - bh.astype(F32)).astype(BF16)
    return _dot(ah, bh) + _dot(ah, bl) + _dot(al, bh)


def _dot3_nt(a, b):
    ah = a.astype(BF16)
    al = (a - ah.astype(F32)).astype(BF16)
    bh = b.astype(BF16)
    bl = (b - bh.astype(F32)).astype(BF16)
    return _dot_nt(ah, bh) + _dot_nt(ah, bl) + _dot_nt(al, bh)


def _sigmoid(x):
    return 0.5 * jnp.tanh(0.5 * x) + 0.5


def _silu(x):
    h = 0.5 * x
    return h + h * jnp.tanh(h)


def _softplus(x):
    return jnp.maximum(x, 0.0) + jnp.log(1.0 + jnp.exp(-jnp.abs(x)))


def _log_sigmoid(x):
    return jnp.minimum(x, 0.0) - jnp.log(1.0 + jnp.exp(-jnp.abs(x)))


def _rms(x, g):
    ms = jnp.mean(x * x, axis=-1, keepdims=True)
    return x * lax.rsqrt(ms + NORM_EPS) * g


ADA_TN = 1024


def _ada_kernel(c_ref, w_ref, b_ref, o_ref):
    ca = _silu(c_ref[...])
    o_ref[...] = jnp.sum(w_ref[...] * ca, axis=0, keepdims=True) + b_ref[...]


def _ada(c_col, w_ada, b_ada):
    n = w_ada.shape[1]
    return pl.pallas_call(
        _ada_kernel,
        grid=(n // ADA_TN,),
        in_specs=[
            pl.BlockSpec((D_MODEL, 1), lambda j: (0, 0)),
            pl.BlockSpec((D_MODEL, ADA_TN), lambda j: (0, j)),
            pl.BlockSpec((1, ADA_TN), lambda j: (0, j)),
        ],
        out_specs=pl.BlockSpec((1, ADA_TN), lambda j: (0, j)),
        out_shape=jax.ShapeDtypeStruct((1, n), F32),
        compiler_params=_cparams(("arbitrary",)),
        name="ada_mod",
    )(c_col, w_ada, b_ada)


IN_DIM = 15408
PACK_TN = 1024
W_IN_SEGMENTS = ((0, 5120), (5152, 1024), (7200, 4096), (11312, 4096), (6176, 1024))
W_IN_DT_COL = 5120
W_IN_GLR_COL = 11296
PACK_STARTS = tuple(s + t * PACK_TN for s, w in W_IN_SEGMENTS for t in range(w // PACK_TN))


def _pack_kernel(start_ref, dt_ref, glr_ref, wt_hbm, o_ref, small_ref, win_ref, sem):
    j = pl.program_id(0)
    slot = j % 2

    @pl.when(j == 0)
    def _():
        small_ref[...] = jnp.zeros_like(small_ref)
        small_ref[0:SSD_HEADS, :] = dt_ref[...]
        small_ref[SSD_HEADS:SSD_HEADS + GLA_GATE_RANK, :] = glr_ref[...]

    def window_copy(t, s):
        row = pl.multiple_of(start_ref[t], 2 * SUBLANES)
        return pltpu.make_async_copy(wt_hbm.at[pl.ds(row, PACK_TN), :], win_ref.at[s], sem.at[s])

    @pl.when(j == 0)
    def _():
        window_copy(0, 0).start()

    @pl.when(j + 1 < pl.num_programs(0))
    def _():
        window_copy(j + 1, 1 - slot).start()

    window_copy(j, slot).wait()
    o_ref[...] = win_ref[slot].T.astype(BF16)


def _pack_w_in(w_in_t):
    starts = jnp.asarray(PACK_STARTS, I32)
    grid_spec = pltpu.PrefetchScalarGridSpec(
        num_scalar_prefetch=1,
        grid=(len(PACK_STARTS),),
        in_specs=[pl.BlockSpec((SSD_HEADS, D_MODEL), lambda j, s: (W_IN_DT_COL // SSD_HEADS, 0)),
                  pl.BlockSpec((GLA_GATE_RANK, D_MODEL), lambda j, s: (W_IN_GLR_COL // GLA_GATE_RANK, 0)),
                  pl.BlockSpec(memory_space=pl.ANY)],
        out_specs=[pl.BlockSpec((D_MODEL, PACK_TN), lambda j, s: (0, j)),
                   pl.BlockSpec((SMALL_COLS, D_MODEL), lambda j, s: (0, 0))],
        scratch_shapes=[pltpu.VMEM((2, PACK_TN, D_MODEL), F32), pltpu.SemaphoreType.DMA((2,))],
    )
    return pl.pallas_call(
        _pack_kernel,
        grid_spec=grid_spec,
        out_shape=[jax.ShapeDtypeStruct((D_MODEL, PROJ_COLS), BF16),
                   jax.ShapeDtypeStruct((SMALL_COLS, D_MODEL), F32)],
        compiler_params=_cparams(("arbitrary",)),
        name="pack_w_in",
    )(starts, w_in_t, w_in_t, w_in_t)


INPROJ_TM = 1024
INPROJ_TN = 1024


def _inproj_kernel(x_ref, g_ref, sc_ref, sh_ref, wb_ref, ws_ref, o_ref, os_ref, h_ref):
    @pl.when(pl.program_id(1) == 0)
    def _():
        h = _rms(x_ref[...], g_ref[...]) * (1.0 + sc_ref[...]) + sh_ref[...]
        hb = h.astype(BF16)
        h_ref[...] = hb
        os_ref[...] = _dot_nt(hb, ws_ref[...].astype(BF16))

    o_ref[...] = _dot(h_ref[...], wb_ref[...]).astype(BF16)


def _inproj(x2d, g, sc, sh, w_big, w_small):
    vec = pl.BlockSpec((1, D_MODEL), lambda i, j: (0, 0))
    return pl.pallas_call(
        _inproj_kernel,
        grid=(SEQ // INPROJ_TM, PROJ_COLS // INPROJ_TN),
        in_specs=[
            pl.BlockSpec((INPROJ_TM, D_MODEL), lambda i, j: (i, 0)),
            vec, vec, vec,
            pl.BlockSpec((D_MODEL, INPROJ_TN), lambda i, j: (0, j)),
            pl.BlockSpec((SMALL_COLS, D_MODEL), lambda i, j: (0, 0)),
        ],
        out_specs=[
            pl.BlockSpec((INPROJ_TM, INPROJ_TN), lambda i, j: (i, j)),
            pl.BlockSpec((INPROJ_TM, SMALL_COLS), lambda i, j: (i, 0)),
        ],
        out_shape=[
            jax.ShapeDtypeStruct((SEQ, PROJ_COLS), BF16),
            jax.ShapeDtypeStruct((SEQ, SMALL_COLS), F32),
        ],
        scratch_shapes=[pltpu.VMEM((INPROJ_TM, D_MODEL), BF16)],
        compiler_params=_cparams(("arbitrary", "arbitrary")),
        name="inproj",
    )(x2d, g, sc, sh, w_big, w_small)


def _causal_conv_silu(cur, hist_ref, w_ref, b_ref):
    L = cur.shape[0]
    hist_ref[SUBLANES:, :] = cur
    acc = cur * w_ref[3:4, :] + b_ref[...]
    for s in (1, 2, 3):
        acc = acc + hist_ref[SUBLANES - s:SUBLANES - s + L, :] * w_ref[3 - s:4 - s, :]
    hist_ref[:SUBLANES, :] = cur[L - SUBLANES:]
    return _silu(acc)


def _ssd_chunk(xs_ref, bc_ref, z_ref, ma_ref, sm_ref, cwx_ref, cwb_ref, cbx_ref, cbb_ref,
               dtb_ref, alog_ref, dexp_ref, ng_ref, px_ref, pb_ref, st_ref):
    L = SSD_CHUNK
    xs_raw = xs_ref[...].astype(F32)
    bc_raw = bc_ref[...].astype(F32)
    xs = _causal_conv_silu(xs_raw, px_ref, cwx_ref, cbx_ref)
    bc = _causal_conv_silu(bc_raw, pb_ref, cwb_ref, cbb_ref)

    dt = _softplus(sm_ref[:, 0:SSD_HEADS] + dtb_ref[...])
    d_a = dt * (-jnp.exp(alog_ref[...]))
    ri = lax.broadcasted_iota(I32, (L, L), 0)
    ci = lax.broadcasted_iota(I32, (L, L), 1)
    causal = ri >= ci
    tril = jnp.where(causal, 1.0, 0.0).astype(BF16)
    acum = _sel_dot(tril, d_a)
    hi32 = lax.broadcasted_iota(I32, (SSD_HEADS, SSD_HEADS), 0)
    hj32 = lax.broadcasted_iota(I32, (SSD_HEADS, SSD_HEADS), 1)
    eye32 = jnp.where(hi32 == hj32, 1.0, 0.0).astype(BF16)
    a_hi, a_mid, a_lo = _split3(acum)
    acum_t = _dot_nt(eye32, a_hi) + _dot_nt(eye32, a_mid) + _dot_nt(eye32, a_lo)
    last = acum[L - 1:L, :]
    e_a = jnp.exp(acum)
    to_end = jnp.exp(last - acum)
    eh = lax.broadcasted_iota(I32, (SSD_HEADS, SSD_INNER), 0)
    ec = lax.broadcasted_iota(I32, (SSD_HEADS, SSD_INNER), 1)
    expand = jnp.where((ec >> 6) == eh, 1.0, 0.0).astype(BF16)
    expand3 = jnp.concatenate([expand, expand, expand], axis=0)
    dt_x = _dot_sel_stacked(dt, expand3)
    dte_x = _dot_sel_stacked(dt * to_end, expand3)
    ea_x = _dot_sel_stacked(e_a, expand3)
    cd_x = ea_x[L - 1:L, :]

    xdt = xs * dt_x
    lane = lax.broadcasted_iota(I32, xdt.shape, 1)
    left = (lane & 64) == 0
    xdt_l = jnp.where(left, xdt, 0.0).astype(BF16)
    xdt_r = jnp.where(left, 0.0, xdt).astype(BF16)
    xdte = (xs * dte_x).astype(BF16)

    y_diag = []
    y_off = []
    for g in range(SSD_GROUPS):
        bg = bc[:, g * SSD_STATE:(g + 1) * SSD_STATE].astype(BF16)
        cg = bc[:, 512 + g * SSD_STATE:512 + (g + 1) * SSD_STATE].astype(BF16)
        cb = _dot_nt(cg, bg)
        gsl = slice(g * GROUP_COLS, (g + 1) * GROUP_COLS)
        st = st_ref[g]
        y_off.append(_dot(cg, st.astype(BF16)))
        st_ref[g] = st * cd_x[:, gsl] + _dot_tn(bg, xdte[:, gsl])
        for p in range(HEADS_PER_GROUP // 2):
            ms = []
            for h in (g * HEADS_PER_GROUP + 2 * p, g * HEADS_PER_GROUP + 2 * p + 1):
                seg = acum[:, h:h + 1] - acum_t[h:h + 1, :]
                decay = jnp.exp(jnp.where(causal, seg, -jnp.inf))
                ms.append((cb * decay).astype(BF16))
            m_cat = jnp.concatenate(ms, axis=1)
            psl = slice((g * 4 + p) * 128, (g * 4 + p + 1) * 128)
            x_cat = jnp.concatenate([xdt_l[:, psl], xdt_r[:, psl]], axis=0)
            y_diag.append(_dot(m_cat, x_cat))

    y = (jnp.concatenate(y_diag, axis=1) + jnp.concatenate(y_off, axis=1) * ea_x
         + dexp_ref[...] * xs)
    yf = y * _silu(z_ref[...].astype(F32))
    outs = []
    for g in range(SSD_GROUPS):
        seg = yf[:, g * GROUP_COLS:(g + 1) * GROUP_COLS]
        ms = jnp.mean(seg * seg, axis=-1, keepdims=True)
        outs.append(seg * lax.rsqrt(ms + NORM_EPS))
    y_ssd = jnp.concatenate(outs, axis=1) * ng_ref[...]
    return _sigmoid(ma_ref[...].astype(F32)) * y_ssd


def _gla_chunks(q_ref, k_ref, v_ref, r_ref, mb_ref, sm_ref, wg_ref, bg_ref, ng_ref, s_ref, y_ssd, o_ref):
    L = GLA_CHUNK
    ri = lax.broadcasted_iota(I32, (L, L), 0)
    ci = lax.broadcasted_iota(I32, (L, L), 1)
    causal = ri >= ci
    r3 = lax.broadcasted_iota(I32, (L, 3 * L), 0)
    c3 = lax.broadcasted_iota(I32, (L, 3 * L), 1)
    tril3 = jnp.where(r3 >= c3 % L, 1.0, 0.0).astype(BF16)

    for c in range(GLA_STEP // L):
        rs = slice(c * L, (c + 1) * L)
        glr = sm_ref[rs, SSD_HEADS:SSD_HEADS + GLA_GATE_RANK]
        pre = _dot3(glr, wg_ref[...]) + bg_ref[...]
        gk = _log_sigmoid(pre) / GLA_GATE_NORM
        bcum = _sel_dot_stacked(tril3, gk)
        bmid = bcum[L // 2:L // 2 + 1, :]
        blast = bcum[L - 1:L, :]
        q = q_ref[rs, :].astype(F32) * (GLA_HEAD_K ** -0.5)
        k = k_ref[rs, :].astype(F32)
        q_rel = (q * jnp.exp(bcum - bmid)).astype(BF16)
        k_rel = (k * jnp.exp(bmid - bcum)).astype(BF16)
        q_int = (q * jnp.exp(bcum)).astype(BF16)
        k_end = (k * jnp.exp(blast - bcum)).astype(BF16)
        dec = jnp.exp(blast)
        v = v_ref[rs, :]

        outs = []
        for h in range(GLA_HEADS):
            ks = slice(h * GLA_HEAD_K, (h + 1) * GLA_HEAD_K)
            vs = slice(h * GLA_HEAD_V, (h + 1) * GLA_HEAD_V)
            att = jnp.where(causal, _dot_nt(q_rel[:, ks], k_rel[:, ks]), 0.0)
            s_t = s_ref[h]
            o_h = _dot(att.astype(BF16), v[:, vs]) + _dot_nt(q_int[:, ks], s_t.astype(BF16))
            s_ref[h] = s_t * dec[:, ks] + _dot_tn(v[:, vs], k_end[:, ks])
            outs.append(_rms(o_h, ng_ref[...]))
        o = jnp.concatenate(outs, axis=1)
        o = o * _silu(r_ref[rs, :].astype(F32)) * _sigmoid(mb_ref[rs, :].astype(F32))
        o_ref[rs, :] = (y_ssd[rs, :] + o).astype(BF16)


N_SSD_IN = 13
N_GLA_IN = 9


def _mixer_kernel(*refs):
    ssd_in = refs[:N_SSD_IN]
    gla_in = refs[N_SSD_IN:N_SSD_IN + N_GLA_IN]
    o_ref, px_ref, pb_ref, st_ref, s_ref = refs[N_SSD_IN + N_GLA_IN:]

    @pl.when(pl.program_id(0) == 0)
    def _():
        px_ref[...] = jnp.zeros_like(px_ref)
        pb_ref[...] = jnp.zeros_like(pb_ref)
        st_ref[...] = jnp.zeros_like(st_ref)
        s_ref[...] = jnp.zeros_like(s_ref)

    y_ssd = _ssd_chunk(*ssd_in, px_ref, pb_ref, st_ref)
    _gla_chunks(*gla_in, s_ref, y_ssd, o_ref)


def _mixer(proj, small, conv_w, conv_b, dt_bias, a_log, d_exp, ssd_norm_g, w_gate, b_gate, gla_norm_g):
    L = SSD_CHUNK
    full = lambda shape: pl.BlockSpec(shape, lambda i: (0, 0))
    cols = lambda width, col: pl.BlockSpec((L, width), lambda i: (i, col // width))
    ssd_specs = [
        cols(2048, COL_XS), cols(1024, COL_BC), cols(2048, COL_Z), cols(2048, COL_MA),
        pl.BlockSpec((L, SMALL_COLS), lambda i: (i, 0)),
        pl.BlockSpec((4, 2048), lambda i: (0, 0)),
        pl.BlockSpec((4, 1024), lambda i: (0, 2)),
        pl.BlockSpec((1, 2048), lambda i: (0, 0)),
        pl.BlockSpec((1, 1024), lambda i: (0, 2)),
        full((1, SSD_HEADS)), full((1, SSD_HEADS)),
        full((1, SSD_INNER)), full((1, SSD_INNER)),
    ]
    gla_specs = [
        cols(1024, COL_Q), cols(1024, COL_K), cols(2048, COL_V), cols(2048, COL_R), cols(2048, COL_MB),
        pl.BlockSpec((L, SMALL_COLS), lambda i: (i, 0)),
        full((GLA_GATE_RANK, GLA_KEY_DIM)), full((1, GLA_KEY_DIM)), full((1, GLA_HEAD_V)),
    ]
    assert len(ssd_specs) == N_SSD_IN and len(gla_specs) == N_GLA_IN and GLA_STEP == L
    return pl.pallas_call(
        _mixer_kernel,
        grid=(SEQ // L,),
        in_specs=ssd_specs + gla_specs,
        out_specs=pl.BlockSpec((L, D_MODEL), lambda i: (i, 0)),
        out_shape=jax.ShapeDtypeStruct((SEQ, D_MODEL), BF16),
        scratch_shapes=[
            pltpu.VMEM((SUBLANES + L, 2048), F32),
            pltpu.VMEM((SUBLANES + L, 1024), F32),
            pltpu.VMEM((SSD_GROUPS, SSD_STATE, GROUP_COLS), F32),
            pltpu.VMEM((GLA_HEADS, GLA_HEAD_V, GLA_HEAD_K), F32),
        ],
        compiler_params=_cparams(("arbitrary",)),
        name="mixer",
    )(proj, proj, proj, proj, small, conv_w, conv_w, conv_b, conv_b, dt_bias, a_log, d_exp, ssd_norm_g,
      proj, proj, proj, proj, proj, small, w_gate, b_gate, gla_norm_g)


OUTPROJ_TM = 512


def _outproj_kernel(m_ref, x_ref, w_ref, g1_ref, n2_ref, sc_ref, sh_ref, wr_ref,
                    x1_ref, h2_ref, lg_ref):
    x1 = x_ref[...] + g1_ref[...] * _dot(m_ref[...], w_ref[...])
    x1_ref[...] = x1
    h2 = _rms(x1, n2_ref[...]) * (1.0 + sc_ref[...]) + sh_ref[...]
    h2_ref[...] = h2
    lg_ref[...] = _dot3_nt(wr_ref[...], h2)


def _outproj(mixed, x2d, w_out_bf, g1, n2g, sc2, sh2, w_router_t):
    tm = OUTPROJ_TM
    vec = pl.BlockSpec((1, D_MODEL), lambda i: (0, 0))
    row = pl.BlockSpec((tm, D_MODEL), lambda i: (i, 0))
    return pl.pallas_call(
        _outproj_kernel,
        grid=(SEQ // tm,),
        in_specs=[row, row,
                  pl.BlockSpec((D_MODEL, D_MODEL), lambda i: (0, 0), pipeline_mode=pl.Buffered(1)),
                  vec, vec, vec, vec,
                  pl.BlockSpec((N_EXPERTS, D_MODEL), lambda i: (0, 0))],
        out_specs=[row, row, pl.BlockSpec((N_EXPERTS, tm), lambda i: (0, i))],
        out_shape=[jax.ShapeDtypeStruct((SEQ, D_MODEL), F32),
                   jax.ShapeDtypeStruct((SEQ, D_MODEL), F32),
                   jax.ShapeDtypeStruct((N_EXPERTS, SEQ), F32)],
        compiler_params=_cparams(("arbitrary",)),
        name="outproj_router",
    )(mixed, x2d, w_out_bf, g1, n2g, sc2, sh2, w_router_t)


ROUTE_TN = 512
COMBINE_TT = 128
N_TILES = SEQ // COMBINE_TT
TILE_LANES = 128


def _route_kernel(lg_ref, bias_ref, ek_ref, wk_ref, rk_ref, cnt_ref, cb_ref, ct_ref, carry_ref):
    tn = ROUTE_TN
    gsz = N_EXPERTS // N_EXPERT_GROUPS

    @pl.when(pl.program_id(0) == 0)
    def _():
        carry_ref[...] = jnp.zeros_like(carry_ref)
        cb_ref[...] = jnp.zeros_like(cb_ref)
        ct_ref[...] = jnp.zeros_like(ct_ref)

    s = jax.nn.sigmoid(lg_ref[...])
    choice = s + bias_ref[...]
    r8 = lax.broadcasted_iota(I32, (gsz, tn), 0)
    neg = -jnp.inf

    def top1(cur, rows, nrows):
        m = jnp.max(cur, axis=0, keepdims=True)
        idx = jnp.min(jnp.where(cur == m, rows, nrows), axis=0, keepdims=True)
        return m, idx

    gscores = []
    for g in range(N_EXPERT_GROUPS):
        blk = choice[g * gsz:(g + 1) * gsz, :]
        m1, i1 = top1(blk, r8, gsz)
        m2, _ = top1(jnp.where(r8 == i1, neg, blk), r8, gsz)
        gscores.append(m1 + m2)
    cur = jnp.concatenate(gscores, axis=0)
    gsel = jnp.zeros((N_EXPERT_GROUPS, tn), F32)
    for _ in range(TOPK_GROUPS):
        _, idx = top1(cur, r8, N_EXPERT_GROUPS)
        hit = r8 == idx
        gsel = jnp.where(hit, 1.0, gsel)
        cur = jnp.where(hit, neg, cur)
    emask = jnp.concatenate(
        [jnp.broadcast_to(gsel[g:g + 1, :], (gsz, tn)) for g in range(N_EXPERT_GROUPS)], axis=0)
    rows = lax.broadcasted_iota(I32, (N_EXPERTS, tn), 0)
    cur = jnp.where(emask > 0.5, choice, neg)
    sel = jnp.zeros((N_EXPERTS, tn), F32)
    hits, eks, wks = [], [], []
    for _ in range(TOP_K):
        _, idx = top1(cur, rows, N_EXPERTS)
        hit = rows == idx
        hits.append(hit)
        eks.append(idx)
        wks.append(jnp.sum(jnp.where(hit, s, 0.0), axis=0, keepdims=True))
        cur = jnp.where(hit, neg, cur)
        sel = jnp.where(hit, 1.0, sel)
    wsum = wks[0]
    for w in wks[1:]:
        wsum = wsum + w
    scale = ROUTED_SCALE / (wsum + 1e-20)

    ti = lax.broadcasted_iota(I32, (tn, tn), 0)
    tj = lax.broadcasted_iota(I32, (tn, tn), 1)
    before = jnp.where(ti < tj, 1.0, 0.0).astype(BF16)
    rank = _dot(sel.astype(BF16), before) + carry_ref[...]
    rks = [jnp.sum(jnp.where(hit, rank, 0.0), axis=0, keepdims=True) for hit in hits]

    lane = lax.broadcasted_iota(I32, (N_EXPERTS, TILE_LANES), 1)
    carry = carry_ref[...]
    cb = cb_ref[...]
    ct = ct_ref[...]
    for sub in range(tn // COMBINE_TT):
        cnt = jnp.sum(sel[:, sub * COMBINE_TT:(sub + 1) * COMBINE_TT], axis=1, keepdims=True)
        col = pl.program_id(0) * (tn // COMBINE_TT) + sub
        cb = jnp.where(lane == col, carry, cb)
        ct = jnp.where(lane == col, cnt, ct)
        carry = carry + cnt
    cb_ref[...] = cb
    ct_ref[...] = ct
    carry_ref[...] = carry

    ek_ref[...] = jnp.concatenate(eks, axis=0)
    wk_ref[...] = jnp.concatenate(wks, axis=0) * scale
    rk_ref[...] = jnp.concatenate(rks, axis=0).astype(I32)
    cnt_ref[...] = carry_ref[...].astype(I32)


def _route(logits_t, bias_col):
    tn = ROUTE_TN
    kt = pl.BlockSpec((TOP_K, tn), lambda i: (0, i))
    return pl.pallas_call(
        _route_kernel,
        grid=(SEQ // tn,),
        in_specs=[pl.BlockSpec((N_EXPERTS, tn), lambda i: (0, i)),
                  pl.BlockSpec((N_EXPERTS, 1), lambda i: (0, 0))],
        out_specs=[kt, kt, kt, pl.BlockSpec((N_EXPERTS, 1), lambda i: (0, 0)),
                   pl.BlockSpec((N_EXPERTS, TILE_LANES), lambda i: (0, 0)),
                   pl.BlockSpec((N_EXPERTS, TILE_LANES), lambda i: (0, 0))],
        out_shape=[jax.ShapeDtypeStruct((TOP_K, SEQ), I32),
                   jax.ShapeDtypeStruct((TOP_K, SEQ), F32),
                   jax.ShapeDtypeStruct((TOP_K, SEQ), I32),
                   jax.ShapeDtypeStruct((N_EXPERTS, 1), I32),
                   jax.ShapeDtypeStruct((N_EXPERTS, TILE_LANES), F32),
                   jax.ShapeDtypeStruct((N_EXPERTS, TILE_LANES), F32)],
        scratch_shapes=[pltpu.VMEM((N_EXPERTS, 1), F32)],
        compiler_params=_cparams(("arbitrary",)),
        name="route",
    )(logits_t, bias_col)


def _dest_kernel(ek_ref, rk_ref, ps_ref, shift_ref, o_ref, pos_ref):
    tt = COMBINE_TT
    rows = lax.broadcasted_iota(I32, (N_EXPERTS, tt), 0)
    lane = lax.broadcasted_iota(I32, (N_EXPERTS, TILE_LANES), 1)
    dests, poss = [], []
    for sub in range(ROUTE_TN // tt):
        tile = pl.program_id(0) * (ROUTE_TN // tt) + sub
        shift_col = jnp.sum(jnp.where(lane == tile, shift_ref[...], 0), axis=1, keepdims=True)
        ds, ps = [], []
        for k in range(TOP_K):
            hit = rows == ek_ref[k:k + 1, sub * tt:(sub + 1) * tt]
            d = (jnp.sum(jnp.where(hit, ps_ref[...], 0), axis=0, keepdims=True)
                 + rk_ref[k:k + 1, sub * tt:(sub + 1) * tt])
            ds.append(d)
            ps.append(d + jnp.sum(jnp.where(hit, shift_col, 0), axis=0, keepdims=True))
        dests.append(jnp.concatenate(ds, axis=0))
        poss.append(jnp.concatenate(ps, axis=0))
    o_ref[...] = jnp.concatenate(dests, axis=1)
    pos_ref[...] = jnp.concatenate(poss, axis=1)


def _dest(ek, rk, pstarts_col, shift):
    tn = ROUTE_TN
    kt = pl.BlockSpec((TOP_K, tn), lambda i: (0, i))
    return pl.pallas_call(
        _dest_kernel,
        grid=(SEQ // tn,),
        in_specs=[kt, kt, pl.BlockSpec((N_EXPERTS, 1), lambda i: (0, 0)),
                  pl.BlockSpec((N_EXPERTS, TILE_LANES), lambda i: (0, 0))],
        out_specs=[kt, kt],
        out_shape=[jax.ShapeDtypeStruct((TOP_K, SEQ), I32), jax.ShapeDtypeStruct((TOP_K, SEQ), I32)],
        compiler_params=_cparams(("arbitrary",)),
        name="dest_rows",
    )(ek, rk, pstarts_col, shift)


DISPATCH_TT = 256


def _dispatch_kernel(dest_ref, padoff_ref, padlen_ref, h_ref, wg_ref, wu_ref, wd_ref,
                     out_ref, ysh_ref, zeros_ref, sem, zsem):
    tt = DISPATCH_TT
    i = pl.program_id(0)
    base = i * tt

    def issue(t, carry):
        for k in range(TOP_K):
            d = dest_ref[k * SEQ + base + t]
            pltpu.make_async_copy(h_ref.at[pl.ds(t, 1), :], out_ref.at[pl.ds(d, 1), :],
                                  sem).start(priority=k % 2)
        return carry

    lax.fori_loop(0, tt, issue, 0)

    def pad_copy(off, n):
        return pltpu.make_async_copy(zeros_ref.at[pl.ds(0, n), :], out_ref.at[pl.ds(off, n), :], zsem)

    def pad_rows(wait):
        def body(e, carry):
            start = padoff_ref[e]
            head = (-start) & (SUBLANES - 1)
            for j in range(SUBLANES - 1):
                @pl.when(j < head)
                def _():
                    cp = pad_copy(start + j, 1)
                    cp.wait() if wait else cp.start()
            n = pl.multiple_of(padlen_ref[e] - head, SUBLANES)

            @pl.when(n > 0)
            def _():
                cp = pad_copy(pl.multiple_of(start + head, SUBLANES), n)
                cp.wait() if wait else cp.start()
            return carry
        lax.fori_loop(0, N_EXPERTS, body, 0)

    @pl.when(i == 0)
    def _():
        zeros_ref[...] = jnp.zeros_like(zeros_ref)
        pad_rows(wait=False)

    @pl.when(i == pl.num_programs(0) - 1)
    def _():
        pad_rows(wait=True)

    h = h_ref[...].astype(BF16)
    hid = (_silu(_dot(h, wg_ref[...])) * _dot(h, wu_ref[...])).astype(BF16)
    ysh_ref[...] = _dot(hid, wd_ref[...])

    for _ in range(TOP_K):
        pltpu.make_async_copy(h_ref, out_ref.at[pl.ds(0, tt), :], sem).wait()


def _dispatch(dest_flat, pad_off, pad_len, h2, wsg, wsu, wsd):
    tt = DISPATCH_TT
    const = lambda shape: pl.BlockSpec(shape, lambda i, d, po, pn: (0, 0))
    grid_spec = pltpu.PrefetchScalarGridSpec(
        num_scalar_prefetch=3,
        grid=(SEQ // tt,),
        in_specs=[pl.BlockSpec((tt, D_MODEL), lambda i, d, po, pn: (i, 0)),
                  const((D_MODEL, EXPERT_DIM)), const((D_MODEL, EXPERT_DIM)), const((EXPERT_DIM, D_MODEL))],
        out_specs=[pl.BlockSpec(memory_space=pl.ANY),
                   pl.BlockSpec((tt, D_MODEL), lambda i, d, po, pn: (i, 0))],
        scratch_shapes=[
            pltpu.VMEM((MOE_BM, D_MODEL), F32),
            pltpu.SemaphoreType.DMA,
            pltpu.SemaphoreType.DMA,
        ],
    )
    return pl.pallas_call(
        _dispatch_kernel,
        grid_spec=grid_spec,
        out_shape=[jax.ShapeDtypeStruct((MOE_ROWS, D_MODEL), F32),
                   jax.ShapeDtypeStruct((SEQ, D_MODEL), F32)],
        compiler_params=_cparams(("arbitrary",)),
        name="dispatch_shared",
    )(dest_flat, pad_off, pad_len, h2, wsg, wsu, wsd)


def _experts_kernel(bexp_ref, first_ref, slot_ref, next_ref, nused_ref,
                    x_ref, wg_hbm, wu_hbm, wd_hbm, o_ref,
                    wgf_ref, wuf_ref, wdf_ref, wgb_ref, wub_ref, wdb_ref, sem):
    b = pl.program_id(0)
    active = b < nused_ref[0]

    def weight_copies(e, s):
        return (pltpu.make_async_copy(wg_hbm.at[e], wgf_ref.at[s], sem.at[s]),
                pltpu.make_async_copy(wu_hbm.at[e], wuf_ref.at[s], sem.at[s]),
                pltpu.make_async_copy(wd_hbm.at[e], wdf_ref.at[s], sem.at[s]))

    @pl.when(jnp.logical_and(active, b == 0))
    def _():
        for cp in weight_copies(bexp_ref[0], 0):
            cp.start(priority=1)

    @pl.when(jnp.logical_and(active, first_ref[b] == 1))
    def _():
        s = slot_ref[b]
        nxt = next_ref[b]

        @pl.when(nxt < N_EXPERTS)
        def _():
            for cp in weight_copies(nxt, 1 - s):
                cp.start(priority=1)

        for cp in weight_copies(bexp_ref[b], s):
            cp.wait()
        wgb_ref[...] = wgf_ref[s].astype(BF16)
        wub_ref[...] = wuf_ref[s].astype(BF16)
        wdb_ref[...] = wdf_ref[s].astype(BF16)

    @pl.when(active)
    def _():
        x = x_ref[...].astype(BF16)
        gate = _dot(x, wgb_ref[...])
        up = _dot(x, wub_ref[...])
        hid = (_silu(gate) * up).astype(BF16)
        o_ref[...] = _dot(hid, wdb_ref[...]).astype(BF16)


def _experts(block_exp, first, slot, next_exp, nused, xs_perm, w_gate, w_up, w_down):
    bm = MOE_BM

    def xmap(b, be, fi, sl, ne, nu):
        return (jnp.minimum(b, nu[0] - 1), 0)

    hbm = pl.BlockSpec(memory_space=pl.ANY)
    grid_spec = pltpu.PrefetchScalarGridSpec(
        num_scalar_prefetch=5,
        grid=(MOE_NB,),
        in_specs=[pl.BlockSpec((bm, D_MODEL), xmap), hbm, hbm, hbm],
        out_specs=pl.BlockSpec((bm, D_MODEL), xmap),
        scratch_shapes=[
            pltpu.VMEM((2, D_MODEL, EXPERT_DIM), F32),
            pltpu.VMEM((2, D_MODEL, EXPERT_DIM), F32),
            pltpu.VMEM((2, EXPERT_DIM, D_MODEL), F32),
            pltpu.VMEM((D_MODEL, EXPERT_DIM), BF16),
            pltpu.VMEM((D_MODEL, EXPERT_DIM), BF16),
            pltpu.VMEM((EXPERT_DIM, D_MODEL), BF16),
            pltpu.SemaphoreType.DMA((2,)),
        ],
    )
    return pl.pallas_call(
        _experts_kernel,
        grid_spec=grid_spec,
        out_shape=jax.ShapeDtypeStruct((MOE_ROWS, D_MODEL), BF16),
        compiler_params=_cparams(("arbitrary",)),
        name="experts",
    )(block_exp, first, slot, next_exp, nused, xs_perm, w_gate, w_up, w_down)


COMBINE_ALIGN = 16
COMBINE_CHUNK = 512
COMBINE_ROWS = pl.cdiv(COMBINE_TT * TOP_K + 2 * (COMBINE_ALIGN - 1) * N_EXPERTS,
                       COMBINE_CHUNK) * COMBINE_CHUNK
COMBINE_MIN_CHUNKS = pl.cdiv(COMBINE_TT * TOP_K + (COMBINE_ALIGN - 1) * N_EXPERTS, COMBINE_CHUNK)


def _combine_kernel(ws_ref, wl_ref, so_ref, ysh_ref, x1_ref, wk_ref, pos_ref, g2_ref, nf_ref, eo_ref,
                    o_ref, buf_ref, acc_ref, sem):
    tt = COMBINE_TT
    i = pl.program_id(0)
    slot = i % 2

    def start_windows(tile, slot_):
        def body(e, carry):
            idx = tile * N_EXPERTS + e
            n = pl.multiple_of(wl_ref[idx], COMBINE_ALIGN)
            src = pl.multiple_of(ws_ref[idx], COMBINE_ALIGN)
            dst = pl.multiple_of(slot_ * COMBINE_ROWS + so_ref[idx], COMBINE_ALIGN)
            pltpu.make_async_copy(eo_ref.at[pl.ds(src, n), :], buf_ref.at[pl.ds(dst, n), :],
                                  sem.at[slot_]).start()
            return carry

        lax.fori_loop(0, N_EXPERTS, body, 0)

    @pl.when(i == 0)
    def _():
        buf_ref[...] = jnp.zeros_like(buf_ref)
        start_windows(0, 0)

    @pl.when(i + 1 < pl.num_programs(0))
    def _():
        start_windows(i + 1, 1 - slot)

    last = i * N_EXPERTS + N_EXPERTS - 1
    staged = pl.multiple_of(so_ref[last] + wl_ref[last], COMBINE_ALIGN)
    sbase = pl.multiple_of(slot * COMBINE_ROWS, COMBINE_ROWS)
    pltpu.make_async_copy(eo_ref.at[pl.ds(0, staged), :], buf_ref.at[pl.ds(sbase, staged), :],
                          sem.at[slot]).wait()
    wk = wk_ref[...]
    pos = pos_ref[...]
    def routed(c):
        col = lax.broadcasted_iota(I32, (tt, COMBINE_CHUNK), 1) + c * COMBINE_CHUNK
        w = jnp.zeros((tt, COMBINE_CHUNK), F32)
        for k in range(TOP_K):
            w = w + jnp.where(col == pos[:, k:k + 1], wk[:, k:k + 1], 0.0)
        w_hi = w.astype(BF16)
        w_lo = (w - w_hi.astype(F32)).astype(BF16)
        rows = buf_ref[pl.ds(sbase + c * COMBINE_CHUNK, COMBINE_CHUNK), :]
        both = _dot(jnp.concatenate([w_hi, w_lo], axis=0), rows)
        return both[:tt] + both[tt:]

    acc = ysh_ref[...]
    for c in range(COMBINE_MIN_CHUNKS):
        acc = acc + routed(c)
    acc_ref[...] = acc
    for c in range(COMBINE_MIN_CHUNKS, COMBINE_ROWS // COMBINE_CHUNK):
        @pl.when(c * COMBINE_CHUNK < staged)
        def _():
            acc_ref[...] += routed(c)
    x2 = x1_ref[...] + g2_ref[...] * acc_ref[...]
    o_ref[...] = _rms(x2, nf_ref[...])


def _combine(win_start, win_len, stage_off, y_shared, x1, wk_t, pos_t, g2, nf, eo):
    tt = COMBINE_TT
    vec = pl.BlockSpec((1, D_MODEL), lambda i, a, b, c: (0, 0))
    row = pl.BlockSpec((tt, D_MODEL), lambda i, a, b, c: (i, 0))
    slots = pl.BlockSpec((tt, TOP_K), lambda i, a, b, c: (i, 0))
    grid_spec = pltpu.PrefetchScalarGridSpec(
        num_scalar_prefetch=3,
        grid=(N_TILES,),
        in_specs=[row, row, slots, slots, vec, vec, pl.BlockSpec(memory_space=pl.ANY)],
        out_specs=row,
        scratch_shapes=[
            pltpu.VMEM((2 * COMBINE_ROWS, D_MODEL), BF16),
            pltpu.VMEM((tt, D_MODEL), F32),
            pltpu.SemaphoreType.DMA((2,)),
        ],
    )
    return pl.pallas_call(
        _combine_kernel,
        grid_spec=grid_spec,
        out_shape=jax.ShapeDtypeStruct((SEQ, D_MODEL), F32),
        compiler_params=_cparams(("arbitrary",)),
        name="combine_final",
    )(win_start, win_len, stage_off, y_shared, x1, wk_t, pos_t, g2, nf, eo)


def kernel(x, c, w_ada, b_ada, norm1_g, w_in, conv_w, conv_b, dt_bias, a_log, d_skip, ssd_norm_g, gla_w_gate, gla_b_gate, gla_norm_g, w_out, norm2_g, w_router, router_bias, w_e_gate, w_e_up, w_e_down, w_s_gate, w_s_up, w_s_down, normf_g):
    layer = 0
    x2d = x.reshape(SEQ, D_MODEL)
    mod = _ada(c.reshape(D_MODEL, 1), w_ada[layer], b_ada[layer].reshape(1, -1))
    sh1, sc1, g1, sh2, sc2, g2 = [mod[:, i * D_MODEL:(i + 1) * D_MODEL] for i in range(6)]

    w_big, w_small = _pack_w_in(jnp.swapaxes(w_in[layer], 0, 1))

    proj, small = _inproj(x2d, norm1_g[layer].reshape(1, -1), sc1, sh1, w_big, w_small)

    d_exp = jnp.repeat(d_skip[layer], SSD_HEAD_DIM).reshape(1, -1)
    mixed = _mixer(proj, small, conv_w[layer], conv_b[layer].reshape(1, -1),
                   dt_bias[layer].reshape(1, -1), a_log[layer].reshape(1, -1), d_exp,
                   ssd_norm_g[layer].reshape(1, -1), gla_w_gate[layer],
                   gla_b_gate[layer].reshape(1, -1), gla_norm_g[layer].reshape(1, -1))

    x1, h2, logits_t = _outproj(mixed, x2d, w_out[layer].astype(BF16), g1,
                                norm2_g[layer].reshape(1, -1), sc2, sh2, w_router[layer].T)

    ek, wk, rk, counts, cbf, ctf = _route(logits_t, router_bias[layer].reshape(-1, 1))

    counts = counts.reshape(-1)
    padded = (counts + MOE_BM - 1) // MOE_BM * MOE_BM
    pends = jnp.cumsum(padded)
    pstarts = pends - padded
    block_start = jnp.arange(MOE_NB, dtype=I32) * MOE_BM
    block_exp = jnp.minimum(
        jnp.sum((pends[None, :] <= block_start[:, None]).astype(I32), axis=1), N_EXPERTS - 1)
    nused = (pends[-1:] // MOE_BM).astype(I32)
    bidx = jnp.arange(MOE_NB, dtype=I32)
    prev_exp = jnp.concatenate([jnp.full((1,), -1, I32), block_exp[:-1]])
    first = jnp.logical_and(bidx < nused[0], block_exp != prev_exp)
    slot = (jnp.cumsum(first.astype(I32)) - 1) & 1
    later_first = jnp.logical_and(first[None, :], bidx[None, :] > bidx[:, None])
    next_first = jnp.min(jnp.where(later_first, bidx[None, :], MOE_NB), axis=1)
    next_exp = jnp.sum(jnp.where(bidx[None, :] == next_first[:, None], block_exp[None, :], 0), axis=1)
    next_exp = jnp.where(next_first < MOE_NB, next_exp, N_EXPERTS).astype(I32)

    cb = cbf[:, :N_TILES].astype(I32)
    ct = ctf[:, :N_TILES].astype(I32)
    run_start = pstarts[:, None].astype(I32) + cb
    win_start = jnp.where(ct > 0, run_start & -COMBINE_ALIGN, 0)
    win_end = (run_start + ct + COMBINE_ALIGN - 1) & -COMBINE_ALIGN
    win_len = jnp.where(ct > 0, win_end - win_start, COMBINE_ALIGN)
    stage_off = jnp.cumsum(win_len, axis=0) - win_len
    shift = jnp.pad(stage_off - win_start, ((0, 0), (0, TILE_LANES - N_TILES)))
    tile_major = lambda a: a.T.reshape(-1).astype(I32)

    dest, pos = _dest(ek, rk, pstarts.reshape(-1, 1).astype(I32), shift)
    xs_perm, y_shared = _dispatch(
        dest.reshape(-1), (pstarts + counts).astype(I32), (padded - counts).astype(I32), h2,
        w_s_gate[layer].astype(BF16), w_s_up[layer].astype(BF16), w_s_down[layer].astype(BF16))
    eo = _experts(block_exp.astype(I32), first.astype(I32), slot.astype(I32), next_exp, nused, xs_perm,
                  w_e_gate[layer], w_e_up[layer], w_e_down[layer])
    out = _combine(tile_major(win_start), tile_major(win_len), tile_major(stage_off), y_shared, x1,
                   wk.T, pos.T, g2, normf_g.reshape(1, -1), eo)
    return out.reshape(x.shape)
```

```python
import functools

import jax
import jax.numpy as jnp
from jax import lax
from jax.experimental import pallas as pl
from jax.experimental.pallas import tpu as pltpu

F32 = jnp.float32
BF16 = jnp.bfloat16
I32 = jnp.int32

D_MODEL = 2048
SEQ = 8192
NORM_EPS = 1e-6
SSD_HEADS = 32
SSD_HEAD_DIM = 64
SSD_INNER = 2048
SSD_GROUPS = 4
SSD_STATE = 128
SSD_CHUNK = 128
HEADS_PER_GROUP = SSD_HEADS // SSD_GROUPS
GROUP_COLS = HEADS_PER_GROUP * SSD_HEAD_DIM
GLA_HEADS = 4
GLA_KEY_DIM = 1024
GLA_VAL_DIM = 2048
GLA_HEAD_K = 256
GLA_HEAD_V = 512
GLA_GATE_RANK = 16
GLA_GATE_NORM = 16.0
GLA_CHUNK = 64
GLA_STEP = 128
N_EXPERTS = 64
TOP_K = 8
N_EXPERT_GROUPS = 8
TOPK_GROUPS = 4
EXPERT_DIM = 512
ROUTED_SCALE = 2.5

VMEM_LIMIT_BYTES = 56 * 1024 * 1024
SUBLANES = 8

COL_Z, COL_XS, COL_BC, COL_Q, COL_V, COL_R, COL_MA, COL_MB, COL_K = (
    0, 2048, 4096, 5120, 6144, 8192, 10240, 12288, 14336)
PROJ_COLS = 15360
SMALL_COLS = 128

MOE_BM = 512
MOE_NB = SEQ * TOP_K // MOE_BM + N_EXPERTS
MOE_ROWS = MOE_NB * MOE_BM


def _cparams(sem):
    return pltpu.CompilerParams(dimension_semantics=sem, vmem_limit_bytes=VMEM_LIMIT_BYTES)


def _split3(a):
    hi = a.astype(BF16)
    r1 = a - hi.astype(F32)
    mid = r1.astype(BF16)
    lo = (r1 - mid.astype(F32)).astype(BF16)
    return hi, mid, lo


def _dot(a, b):
    return jnp.dot(a, b, preferred_element_type=F32)


def _dot_nt(a, b):
    return lax.dot_general(a, b, (((1,), (1,)), ((), ())), preferred_element_type=F32)


def _dot_tn(a, b):
    return lax.dot_general(a, b, (((0,), (0,)), ((), ())), preferred_element_type=F32)


def _sel_dot(sel_bf16, a):
    hi, mid, lo = _split3(a)
    return _dot(sel_bf16, hi) + _dot(sel_bf16, mid) + _dot(sel_bf16, lo)


def _dot_sel(a, sel_bf16):
    hi, mid, lo = _split3(a)
    return _dot(hi, sel_bf16) + _dot(mid, sel_bf16) + _dot(lo, sel_bf16)


def _dot_sel_stacked(a, sel3):
    hi = a.astype(BF16).astype(F32)
    r1 = a - hi
    mid = r1.astype(BF16).astype(F32)
    lo = (r1 - mid).astype(BF16).astype(F32)
    return _dot(jnp.concatenate([hi, mid, lo], axis=1).astype(BF16), sel3)


def _sel_dot_stacked(sel3, a):
    hi, mid, lo = _split3(a)
    return _dot(sel3, jnp.concatenate([hi, mid, lo], axis=0))


def _dot3(a, b):
    ah = a.astype(BF16)
    al = (a - ah.astype(F32)).astype(BF16)
    bh = b.astype(BF16)
    bl = (b - bh.astype(F32)).astype(BF16)
    return _dot(ah, bh) + _dot(ah, bl) + _dot(al, bh)


def _dot3_nt(a, b):
    ah = a.astype(BF16)
    al = (a - ah.astype(F32)).astype(BF16)
    bh = b.astype(BF16)
    bl = (b - bh.astype(F32)).astype(BF16)
    return _dot_nt(ah, bh) + _dot_nt(ah, bl) + _dot_nt(al, bh)


def _sigmoid(x):
    return 0.5 * jnp.tanh(0.5 * x) + 0.5


def _silu(x):
    h = 0.5 * x
    return h + h * jnp.tanh(h)


def _softplus(x):
    return jnp.maximum(x, 0.0) + jnp.log(1.0 + jnp.exp(-jnp.abs(x)))


def _log_sigmoid(x):
    return jnp.minimum(x, 0.0) - jnp.log(1.0 + jnp.exp(-jnp.abs(x)))


def _rms(x, g):
    ms = jnp.mean(x * x, axis=-1, keepdims=True)
    return x * lax.rsqrt(ms + NORM_EPS) * g


ADA_TN = 1024


def _ada_kernel(c_ref, w_ref, b_ref, o_ref):
    ca = _silu(c_ref[...])
    o_ref[...] = jnp.sum(w_ref[...] * ca, axis=0, keepdims=True) + b_ref[...]


def _ada(c_col, w_ada, b_ada):
    n = w_ada.shape[1]
    return pl.pallas_call(
        _ada_kernel,
        grid=(n // ADA_TN,),
        in_specs=[
            pl.BlockSpec((D_MODEL, 1), lambda j: (0, 0)),
            pl.BlockSpec((D_MODEL, ADA_TN), lambda j: (0, j)),
            pl.BlockSpec((1, ADA_TN), lambda j: (0, j)),
        ],
        out_specs=pl.BlockSpec((1, ADA_TN), lambda j: (0, j)),
        out_shape=jax.ShapeDtypeStruct((1, n), F32),
        compiler_params=_cparams(("arbitrary",)),
        name="ada_mod",
    )(c_col, w_ada, b_ada)


IN_DIM = 15408
PACK_TN = 1024
W_IN_SEGMENTS = ((0, 5120), (5152, 1024), (7200, 4096), (11312, 4096), (6176, 1024))
W_IN_DT_COL = 5120
W_IN_GLR_COL = 11296
PACK_STARTS = tuple(s + t * PACK_TN for s, w in W_IN_SEGMENTS for t in range(w // PACK_TN))


def _pack_kernel(start_ref, dt_ref, glr_ref, wt_hbm, o_ref, small_ref, win_ref, sem):
    j = pl.program_id(0)
    slot = j % 2

    @pl.when(j == 0)
    def _():
        small_ref[...] = jnp.zeros_like(small_ref)
        small_ref[0:SSD_HEADS, :] = dt_ref[...]
        small_ref[SSD_HEADS:SSD_HEADS + GLA_GATE_RANK, :] = glr_ref[...]

    def window_copy(t, s):
        row = pl.multiple_of(start_ref[t], 2 * SUBLANES)
        return pltpu.make_async_copy(wt_hbm.at[pl.ds(row, PACK_TN), :], win_ref.at[s], sem.at[s])

    @pl.when(j == 0)
    def _():
        window_copy(0, 0).start()

    @pl.when(j + 1 < pl.num_programs(0))
    def _():
        window_copy(j + 1, 1 - slot).start()

    window_copy(j, slot).wait()
    o_ref[...] = win_ref[slot].T.astype(BF16)


def _pack_w_in(w_in_t):
    starts = jnp.asarray(PACK_STARTS, I32)
    grid_spec = pltpu.PrefetchScalarGridSpec(
        num_scalar_prefetch=1,
        grid=(len(PACK_STARTS),),
        in_specs=[pl.BlockSpec((SSD_HEADS, D_MODEL), lambda j, s: (W_IN_DT_COL // SSD_HEADS, 0)),
                  pl.BlockSpec((GLA_GATE_RANK, D_MODEL), lambda j, s: (W_IN_GLR_COL // GLA_GATE_RANK, 0)),
                  pl.BlockSpec(memory_space=pl.ANY)],
        out_specs=[pl.BlockSpec((D_MODEL, PACK_TN), lambda j, s: (0, j)),
                   pl.BlockSpec((SMALL_COLS, D_MODEL), lambda j, s: (0, 0))],
        scratch_shapes=[pltpu.VMEM((2, PACK_TN, D_MODEL), F32), pltpu.SemaphoreType.DMA((2,))],
    )
    return pl.pallas_call(
        _pack_kernel,
        grid_spec=grid_spec,
        out_shape=[jax.ShapeDtypeStruct((D_MODEL, PROJ_COLS), BF16),
                   jax.ShapeDtypeStruct((SMALL_COLS, D_MODEL), F32)],
        compiler_params=_cparams(("arbitrary",)),
        name="pack_w_in",
    )(starts, w_in_t, w_in_t, w_in_t)


INPROJ_TM = 1024
INPROJ_TN = 1024


def _inproj_kernel(x_ref, g_ref, sc_ref, sh_ref, wb_ref, ws_ref, o_ref, os_ref, h_ref):
    @pl.when(pl.program_id(1) == 0)
    def _():
        h = _rms(x_ref[...], g_ref[...]) * (1.0 + sc_ref[...]) + sh_ref[...]
        hb = h.astype(BF16)
        h_ref[...] = hb
        os_ref[...] = _dot_nt(hb, ws_ref[...].astype(BF16))

    o_ref[...] = _dot(h_ref[...], wb_ref[...]).astype(BF16)


def _inproj(x2d, g, sc, sh, w_big, w_small):
    vec = pl.BlockSpec((1, D_MODEL), lambda i, j: (0, 0))
    return pl.pallas_call(
        _inproj_kernel,
        grid=(SEQ // INPROJ_TM, PROJ_COLS // INPROJ_TN),
        in_specs=[
            pl.BlockSpec((INPROJ_TM, D_MODEL), lambda i, j: (i, 0)),
            vec, vec, vec,
            pl.BlockSpec((D_MODEL, INPROJ_TN), lambda i, j: (0, j)),
            pl.BlockSpec((SMALL_COLS, D_MODEL), lambda i, j: (0, 0)),
        ],
        out_specs=[
            pl.BlockSpec((INPROJ_TM, INPROJ_TN), lambda i, j: (i, j)),
            pl.BlockSpec((INPROJ_TM, SMALL_COLS), lambda i, j: (i, 0)),
        ],
        out_shape=[
            jax.ShapeDtypeStruct((SEQ, PROJ_COLS), BF16),
            jax.ShapeDtypeStruct((SEQ, SMALL_COLS), F32),
        ],
        scratch_shapes=[pltpu.VMEM((INPROJ_TM, D_MODEL), BF16)],
        compiler_params=_cparams(("arbitrary", "arbitrary")),
        name="inproj",
    )(x2d, g, sc, sh, w_big, w_small)


def _causal_conv_silu(cur, hist_ref, w_ref, b_ref):
    L = cur.shape[0]
    hist_ref[SUBLANES:, :] = cur
    acc = cur * w_ref[3:4, :] + b_ref[...]
    for s in (1, 2, 3):
        acc = acc + hist_ref[SUBLANES - s:SUBLANES - s + L, :] * w_ref[3 - s:4 - s, :]
    hist_ref[:SUBLANES, :] = cur[L - SUBLANES:]
    return _silu(acc)


def _ssd_chunk(xs_ref, bc_ref, z_ref, ma_ref, sm_ref, cwx_ref, cwb_ref, cbx_ref, cbb_ref,
               dtb_ref, alog_ref, dexp_ref, ng_ref, px_ref, pb_ref, st_ref):
    L = SSD_CHUNK
    xs_raw = xs_ref[...].astype(F32)
    bc_raw = bc_ref[...].astype(F32)
    xs = _causal_conv_silu(xs_raw, px_ref, cwx_ref, cbx_ref)
    bc = _causal_conv_silu(bc_raw, pb_ref, cwb_ref, cbb_ref)

    dt = _softplus(sm_ref[:, 0:SSD_HEADS] + dtb_ref[...])
    d_a = dt * (-jnp.exp(alog_ref[...]))
    ri = lax.broadcasted_iota(I32, (L, L), 0)
    ci = lax.broadcasted_iota(I32, (L, L), 1)
    causal = ri >= ci
    tril = jnp.where(causal, 1.0, 0.0).astype(BF16)
    acum = _sel_dot(tril, d_a)
    hi32 = lax.broadcasted_iota(I32, (SSD_HEADS, SSD_HEADS), 0)
    hj32 = lax.broadcasted_iota(I32, (SSD_HEADS, SSD_HEADS), 1)
    eye32 = jnp.where(hi32 == hj32, 1.0, 0.0).astype(BF16)
    a_hi, a_mid, a_lo = _split3(acum)
    acum_t = _dot_nt(eye32, a_hi) + _dot_nt(eye32, a_mid) + _dot_nt(eye32, a_lo)
    last = acum[L - 1:L, :]
    e_a = jnp.exp(acum)
    to_end = jnp.exp(last - acum)
    eh = lax.broadcasted_iota(I32, (SSD_HEADS, SSD_INNER), 0)
    ec = lax.broadcasted_iota(I32, (SSD_HEADS, SSD_INNER), 1)
    expand = jnp.where((ec >> 6) == eh, 1.0, 0.0).astype(BF16)
    expand3 = jnp.concatenate([expand, expand, expand], axis=0)
    dt_x = _dot_sel_stacked(dt, expand3)
    dte_x = _dot_sel_stacked(dt * to_end, expand3)
    ea_x = _dot_sel_stacked(e_a, expand3)
    cd_x = ea_x[L - 1:L, :]

    xdt = xs * dt_x
    lane = lax.broadcasted_iota(I32, xdt.shape, 1)
    left = (lane & 64) == 0
    xdt_l = jnp.where(left, xdt, 0.0).astype(BF16)
    xdt_r = jnp.where(left, 0.0, xdt).astype(BF16)
    xdte = (xs * dte_x).astype(BF16)

    y_diag = []
    y_off = []
    for g in range(SSD_GROUPS):
        bg = bc[:, g * SSD_STATE:(g + 1) * SSD_STATE].astype(BF16)
        cg = bc[:, 512 + g * SSD_STATE:512 + (g + 1) * SSD_STATE].astype(BF16)
        cb = _dot_nt(cg, bg)
        gsl = slice(g * GROUP_COLS, (g + 1) * GROUP_COLS)
        st = st_ref[g]
        y_off.append(_dot(cg, st.astype(BF16)))
        st_ref[g] = st * cd_x[:, gsl] + _dot_tn(bg, xdte[:, gsl])
        for p in range(HEADS_PER_GROUP // 2):
            ms = []
            for h in (g * HEADS_PER_GROUP + 2 * p, g * HEADS_PER_GROUP + 2 * p + 1):
                seg = acum[:, h:h + 1] - acum_t[h:h + 1, :]
                decay = jnp.exp(jnp.where(causal, seg, -jnp.inf))
                ms.append((cb * decay).astype(BF16))
            m_cat = jnp.concatenate(ms, axis=1)
            psl = slice((g * 4 + p) * 128, (g * 4 + p + 1) * 128)
            x_cat = jnp.concatenate([xdt_l[:, psl], xdt_r[:, psl]], axis=0)
            y_diag.append(_dot(m_cat, x_cat))

    y = (jnp.concatenate(y_diag, axis=1) + jnp.concatenate(y_off, axis=1) * ea_x
         + dexp_ref[...] * xs)
    yf = y * _silu(z_ref[...].astype(F32))
    outs = []
    for g in range(SSD_GROUPS):
        seg = yf[:, g * GROUP_COLS:(g + 1) * GROUP_COLS]
        ms = jnp.mean(seg * seg, axis=-1, keepdims=True)
        outs.append(seg * lax.rsqrt(ms + NORM_EPS))
    y_ssd = jnp.concatenate(outs, axis=1) * ng_ref[...]
    return _sigmoid(ma_ref[...].astype(F32)) * y_ssd


def _gla_chunks(q_ref, k_ref, v_ref, r_ref, mb_ref, sm_ref, wg_ref, bg_ref, ng_ref, s_ref, y_ssd, o_ref):
    L = GLA_CHUNK
    ri = lax.broadcasted_iota(I32, (L, L), 0)
    ci = lax.broadcasted_iota(I32, (L, L), 1)
    causal = ri >= ci
    r3 = lax.broadcasted_iota(I32, (L, 3 * L), 0)
    c3 = lax.broadcasted_iota(I32, (L, 3 * L), 1)
    tril3 = jnp.where(r3 >= c3 % L, 1.0, 0.0).astype(BF16)

    for c in range(GLA_STEP // L):
        rs = slice(c * L, (c + 1) * L)
        glr = sm_ref[rs, SSD_HEADS:SSD_HEADS + GLA_GATE_RANK]
        pre = _dot3(glr, wg_ref[...]) + bg_ref[...]
        gk = _log_sigmoid(pre) / GLA_GATE_NORM
        bcum = _sel_dot_stacked(tril3, gk)
        bmid = bcum[L // 2:L // 2 + 1, :]
        blast = bcum[L - 1:L, :]
        q = q_ref[rs, :].astype(F32) * (GLA_HEAD_K ** -0.5)
        k = k_ref[rs, :].astype(F32)
        q_rel = (q * jnp.exp(bcum - bmid)).astype(BF16)
        k_rel = (k * jnp.exp(bmid - bcum)).astype(BF16)
        q_int = (q * jnp.exp(bcum)).astype(BF16)
        k_end = (k * jnp.exp(blast - bcum)).astype(BF16)
        dec = jnp.exp(blast)
        v = v_ref[rs, :]

        outs = []
        for h in range(GLA_HEADS):
            ks = slice(h * GLA_HEAD_K, (h + 1) * GLA_HEAD_K)
            vs = slice(h * GLA_HEAD_V, (h + 1) * GLA_HEAD_V)
            att = jnp.where(causal, _dot_nt(q_rel[:, ks], k_rel[:, ks]), 0.0)
            s_t = s_ref[h]
            o_h = _dot(att.astype(BF16), v[:, vs]) + _dot_nt(q_int[:, ks], s_t.astype(BF16))
            s_ref[h] = s_t * dec[:, ks] + _dot_tn(v[:, vs], k_end[:, ks])
            outs.append(_rms(o_h, ng_ref[...]))
        o = jnp.concatenate(outs, axis=1)
        o = o * _silu(r_ref[rs, :].astype(F32)) * _sigmoid(mb_ref[rs, :].astype(F32))
        o_ref[rs, :] = (y_ssd[rs, :] + o).astype(BF16)


N_SSD_IN = 13
N_GLA_IN = 9


def _mixer_kernel(*refs):
    ssd_in = refs[:N_SSD_IN]
    gla_in = refs[N_SSD_IN:N_SSD_IN + N_GLA_IN]
    o_ref, px_ref, pb_ref, st_ref, s_ref = refs[N_SSD_IN + N_GLA_IN:]

    @pl.when(pl.program_id(0) == 0)
    def _():
        px_ref[...] = jnp.zeros_like(px_ref)
        pb_ref[...] = jnp.zeros_like(pb_ref)
        st_ref[...] = jnp.zeros_like(st_ref)
        s_ref[...] = jnp.zeros_like(s_ref)

    y_ssd = _ssd_chunk(*ssd_in, px_ref, pb_ref, st_ref)
    _gla_chunks(*gla_in, s_ref, y_ssd, o_ref)


def _mixer(proj, small, conv_w, conv_b, dt_bias, a_log, d_exp, ssd_norm_g, w_gate, b_gate, gla_norm_g):
    L = SSD_CHUNK
    full = lambda shape: pl.BlockSpec(shape, lambda i: (0, 0))
    cols = lambda width, col: pl.BlockSpec((L, width), lambda i: (i, col // width))
    ssd_specs = [
        cols(2048, COL_XS), cols(1024, COL_BC), cols(2048, COL_Z), cols(2048, COL_MA),
        pl.BlockSpec((L, SMALL_COLS), lambda i: (i, 0)),
        pl.BlockSpec((4, 2048), lambda i: (0, 0)),
        pl.BlockSpec((4, 1024), lambda i: (0, 2)),
        pl.BlockSpec((1, 2048), lambda i: (0, 0)),
        pl.BlockSpec((1, 1024), lambda i: (0, 2)),
        full((1, SSD_HEADS)), full((1, SSD_HEADS)),
        full((1, SSD_INNER)), full((1, SSD_INNER)),
    ]
    gla_specs = [
        cols(1024, COL_Q), cols(1024, COL_K), cols(2048, COL_V), cols(2048, COL_R), cols(2048, COL_MB),
        pl.BlockSpec((L, SMALL_COLS), lambda i: (i, 0)),
        full((GLA_GATE_RANK, GLA_KEY_DIM)), full((1, GLA_KEY_DIM)), full((1, GLA_HEAD_V)),
    ]
    assert len(ssd_specs) == N_SSD_IN and len(gla_specs) == N_GLA_IN and GLA_STEP == L
    return pl.pallas_call(
        _mixer_kernel,
        grid=(SEQ // L,),
        in_specs=ssd_specs + gla_specs,
        out_specs=pl.BlockSpec((L, D_MODEL), lambda i: (i, 0)),
        out_shape=jax.ShapeDtypeStruct((SEQ, D_MODEL), BF16),
        scratch_shapes=[
            pltpu.VMEM((SUBLANES + L, 2048), F32),
            pltpu.VMEM((SUBLANES + L, 1024), F32),
            pltpu.VMEM((SSD_GROUPS, SSD_STATE, GROUP_COLS), F32),
            pltpu.VMEM((GLA_HEADS, GLA_HEAD_V, GLA_HEAD_K), F32),
        ],
        compiler_params=_cparams(("arbitrary",)),
        name="mixer",
    )(proj, proj, proj, proj, small, conv_w, conv_w, conv_b, conv_b, dt_bias, a_log, d_exp, ssd_norm_g,
      proj, proj, proj, proj, proj, small, w_gate, b_gate, gla_norm_g)


OUTPROJ_TM = 512


def _outproj_kernel(m_ref, x_ref, w_ref, g1_ref, n2_ref, sc_ref, sh_ref, wr_ref,
                    x1_ref, h2_ref, lg_ref):
    x1 = x_ref[...] + g1_ref[...] * _dot(m_ref[...], w_ref[...])
    x1_ref[...] = x1
    h2 = _rms(x1, n2_ref[...]) * (1.0 + sc_ref[...]) + sh_ref[...]
    h2_ref[...] = h2
    lg_ref[...] = _dot3_nt(wr_ref[...], h2)


def _outproj(mixed, x2d, w_out_bf, g1, n2g, sc2, sh2, w_router_t):
    tm = OUTPROJ_TM
    vec = pl.BlockSpec((1, D_MODEL), lambda i: (0, 0))
    row = pl.BlockSpec((tm, D_MODEL), lambda i: (i, 0))
    return pl.pallas_call(
        _outproj_kernel,
        grid=(SEQ // tm,),
        in_specs=[row, row,
                  pl.BlockSpec((D_MODEL, D_MODEL), lambda i: (0, 0), pipeline_mode=pl.Buffered(1)),
                  vec, vec, vec, vec,
                  pl.BlockSpec((N_EXPERTS, D_MODEL), lambda i: (0, 0))],
        out_specs=[row, row, pl.BlockSpec((N_EXPERTS, tm), lambda i: (0, i))],
        out_shape=[jax.ShapeDtypeStruct((SEQ, D_MODEL), F32),
                   jax.ShapeDtypeStruct((SEQ, D_MODEL), F32),
                   jax.ShapeDtypeStruct((N_EXPERTS, SEQ), F32)],
        compiler_params=_cparams(("arbitrary",)),
        name="outproj_router",
    )(mixed, x2d, w_out_bf, g1, n2g, sc2, sh2, w_router_t)


ROUTE_TN = 512
COMBINE_TT = 128
N_TILES = SEQ // COMBINE_TT
TILE_LANES = 128


def _route_kernel(lg_ref, bias_ref, ek_ref, wk_ref, rk_ref, cnt_ref, cb_ref, ct_ref, carry_ref):
    tn = ROUTE_TN
    gsz = N_EXPERTS // N_EXPERT_GROUPS

    @pl.when(pl.program_id(0) == 0)
    def _():
        carry_ref[...] = jnp.zeros_like(carry_ref)
        cb_ref[...] = jnp.zeros_like(cb_ref)
        ct_ref[...] = jnp.zeros_like(ct_ref)

    s = jax.nn.sigmoid(lg_ref[...])
    choice = s + bias_ref[...]
    r8 = lax.broadcasted_iota(I32, (gsz, tn), 0)
    neg = -jnp.inf

    def top1(cur, rows, nrows):
        m = jnp.max(cur, axis=0, keepdims=True)
        idx = jnp.min(jnp.where(cur == m, rows, nrows), axis=0, keepdims=True)
        return m, idx

    gscores = []
    for g in range(N_EXPERT_GROUPS):
        blk = choice[g * gsz:(g + 1) * gsz, :]
        m1, i1 = top1(blk, r8, gsz)
        m2, _ = top1(jnp.where(r8 == i1, neg, blk), r8, gsz)
        gscores.append(m1 + m2)
    cur = jnp.concatenate(gscores, axis=0)
    gsel = jnp.zeros((N_EXPERT_GROUPS, tn), F32)
    for _ in range(TOPK_GROUPS):
        _, idx = top1(cur, r8, N_EXPERT_GROUPS)
        hit = r8 == idx
        gsel = jnp.where(hit, 1.0, gsel)
        cur = jnp.where(hit, neg, cur)
    emask = jnp.concatenate(
        [jnp.broadcast_to(gsel[g:g + 1, :], (gsz, tn)) for g in range(N_EXPERT_GROUPS)], axis=0)
    rows = lax.broadcasted_iota(I32, (N_EXPERTS, tn), 0)
    cur = jnp.where(emask > 0.5, choice, neg)
    sel = jnp.zeros((N_EXPERTS, tn), F32)
    hits, eks, wks = [], [], []
    for _ in range(TOP_K):
        _, idx = top1(cur, rows, N_EXPERTS)
        hit = rows == idx
        hits.append(hit)
        eks.append(idx)
        wks.append(jnp.sum(jnp.where(hit, s, 0.0), axis=0, keepdims=True))
        cur = jnp.where(hit, neg, cur)
        sel = jnp.where(hit, 1.0, sel)
    wsum = wks[0]
    for w in wks[1:]:
        wsum = wsum + w
    scale = ROUTED_SCALE / (wsum + 1e-20)

    ti = lax.broadcasted_iota(I32, (tn, tn), 0)
    tj = lax.broadcasted_iota(I32, (tn, tn), 1)
    before = jnp.where(ti < tj, 1.0, 0.0).astype(BF16)
    rank = _dot(sel.astype(BF16), before) + carry_ref[...]
    rks = [jnp.sum(jnp.where(hit, rank, 0.0), axis=0, keepdims=True) for hit in hits]

    lane = lax.broadcasted_iota(I32, (N_EXPERTS, TILE_LANES), 1)
    carry = carry_ref[...]
    cb = cb_ref[...]
    ct = ct_ref[...]
    for sub in range(tn // COMBINE_TT):
        cnt = jnp.sum(sel[:, sub * COMBINE_TT:(sub + 1) * COMBINE_TT], axis=1, keepdims=True)
        col = pl.program_id(0) * (tn // COMBINE_TT) + sub
        cb = jnp.where(lane == col, carry, cb)
        ct = jnp.where(lane == col, cnt, ct)
        carry = carry + cnt
    cb_ref[...] = cb
    ct_ref[...] = ct
    carry_ref[...] = carry

    ek_ref[...] = jnp.concatenate(eks, axis=0)
    wk_ref[...] = jnp.concatenate(wks, axis=0) * scale
    rk_ref[...] = jnp.concatenate(rks, axis=0).astype(I32)
    cnt_ref[...] = carry_ref[...].astype(I32)


def _route(logits_t, bias_col):
    tn = ROUTE_TN
    kt = pl.BlockSpec((TOP_K, tn), lambda i: (0, i))
    return pl.pallas_call(
        _route_kernel,
        grid=(SEQ // tn,),
        in_specs=[pl.BlockSpec((N_EXPERTS, tn), lambda i: (0, i)),
                  pl.BlockSpec((N_EXPERTS, 1), lambda i: (0, 0))],
        out_specs=[kt, kt, kt, pl.BlockSpec((N_EXPERTS, 1), lambda i: (0, 0)),
                   pl.BlockSpec((N_EXPERTS, TILE_LANES), lambda i: (0, 0)),
                   pl.BlockSpec((N_EXPERTS, TILE_LANES), lambda i: (0, 0))],
        out_shape=[jax.ShapeDtypeStruct((TOP_K, SEQ), I32),
                   jax.ShapeDtypeStruct((TOP_K, SEQ), F32),
                   jax.ShapeDtypeStruct((TOP_K, SEQ), I32),
                   jax.ShapeDtypeStruct((N_EXPERTS, 1), I32),
                   jax.ShapeDtypeStruct((N_EXPERTS, TILE_LANES), F32),
                   jax.ShapeDtypeStruct((N_EXPERTS, TILE_LANES), F32)],
        scratch_shapes=[pltpu.VMEM((N_EXPERTS, 1), F32)],
        compiler_params=_cparams(("arbitrary",)),
        name="route",
    )(logits_t, bias_col)


def _dest_kernel(ek_ref, rk_ref, ps_ref, shift_ref, o_ref, pos_ref):
    tt = COMBINE_TT
    rows = lax.broadcasted_iota(I32, (N_EXPERTS, tt), 0)
    lane = lax.broadcasted_iota(I32, (N_EXPERTS, TILE_LANES), 1)
    dests, poss = [], []
    for sub in range(ROUTE_TN // tt):
        tile = pl.program_id(0) * (ROUTE_TN // tt) + sub
        shift_col = jnp.sum(jnp.where(lane == tile, shift_ref[...], 0), axis=1, keepdims=True)
        ds, ps = [], []
        for k in range(TOP_K):
            hit = rows == ek_ref[k:k + 1, sub * tt:(sub + 1) * tt]
            d = (jnp.sum(jnp.where(hit, ps_ref[...], 0), axis=0, keepdims=True)
                 + rk_ref[k:k + 1, sub * tt:(sub + 1) * tt])
            ds.append(d)
            ps.append(d + jnp.sum(jnp.where(hit, shift_col, 0), axis=0, keepdims=True))
        dests.append(jnp.concatenate(ds, axis=0))
        poss.append(jnp.concatenate(ps, axis=0))
    o_ref[...] = jnp.concatenate(dests, axis=1)
    pos_ref[...] = jnp.concatenate(poss, axis=1)


def _dest(ek, rk, pstarts_col, shift):
    tn = ROUTE_TN
    kt = pl.BlockSpec((TOP_K, tn), lambda i: (0, i))
    return pl.pallas_call(
        _dest_kernel,
        grid=(SEQ // tn,),
        in_specs=[kt, kt, pl.BlockSpec((N_EXPERTS, 1), lambda i: (0, 0)),
                  pl.BlockSpec((N_EXPERTS, TILE_LANES), lambda i: (0, 0))],
        out_specs=[kt, kt],
        out_shape=[jax.ShapeDtypeStruct((TOP_K, SEQ), I32), jax.ShapeDtypeStruct((TOP_K, SEQ), I32)],
        compiler_params=_cparams(("arbitrary",)),
        name="dest_rows",
    )(ek, rk, pstarts_col, shift)


DISPATCH_TT = 256


def _dispatch_kernel(dest_ref, padoff_ref, padlen_ref, h_ref, wg_ref, wu_ref, wd_ref,
                     out_ref, ysh_ref, zeros_ref, sem, zsem):
    tt = DISPATCH_TT
    i = pl.program_id(0)
    base = i * tt

    def issue(t, carry):
        for k in range(TOP_K):
            d = dest_ref[k * SEQ + base + t]
            pltpu.make_async_copy(h_ref.at[pl.ds(t, 1), :], out_ref.at[pl.ds(d, 1), :],
                                  sem).start(priority=k % 2)
        return carry

    lax.fori_loop(0, tt, issue, 0)

    def pad_copy(off, n):
        return pltpu.make_async_copy(zeros_ref.at[pl.ds(0, n), :], out_ref.at[pl.ds(off, n), :], zsem)

    def pad_rows(wait):
        def body(e, carry):
            start = padoff_ref[e]
            head = (-start) & (SUBLANES - 1)
            for j in range(SUBLANES - 1):
                @pl.when(j < head)
                def _():
                    cp = pad_copy(start + j, 1)
                    cp.wait() if wait else cp.start()
            n = pl.multiple_of(padlen_ref[e] - head, SUBLANES)

            @pl.when(n > 0)
            def _():
                cp = pad_copy(pl.multiple_of(start + head, SUBLANES), n)
                cp.wait() if wait else cp.start()
            return carry
        lax.fori_loop(0, N_EXPERTS, body, 0)

    @pl.when(i == 0)
    def _():
        zeros_ref[...] = jnp.zeros_like(zeros_ref)
        pad_rows(wait=False)

    @pl.when(i == pl.num_programs(0) - 1)
    def _():
        pad_rows(wait=True)

    h = h_ref[...].astype(BF16)
    hid = (_silu(_dot(h, wg_ref[...])) * _dot(h, wu_ref[...])).astype(BF16)
    ysh_ref[...] = _dot(hid, wd_ref[...])

    for _ in range(TOP_K):
        pltpu.make_async_copy(h_ref, out_ref.at[pl.ds(0, tt), :], sem).wait()


def _dispatch(dest_flat, pad_off, pad_len, h2, wsg, wsu, wsd):
    tt = DISPATCH_TT
    const = lambda shape: pl.BlockSpec(shape, lambda i, d, po, pn: (0, 0))
    grid_spec = pltpu.PrefetchScalarGridSpec(
        num_scalar_prefetch=3,
        grid=(SEQ // tt,),
        in_specs=[pl.BlockSpec((tt, D_MODEL), lambda i, d, po, pn: (i, 0)),
                  const((D_MODEL, EXPERT_DIM)), const((D_MODEL, EXPERT_DIM)), const((EXPERT_DIM, D_MODEL))],
        out_specs=[pl.BlockSpec(memory_space=pl.ANY),
                   pl.BlockSpec((tt, D_MODEL), lambda i, d, po, pn: (i, 0))],
        scratch_shapes=[
            pltpu.VMEM((MOE_BM, D_MODEL), F32),
            pltpu.SemaphoreType.DMA,
            pltpu.SemaphoreType.DMA,
        ],
    )
    return pl.pallas_call(
        _dispatch_kernel,
        grid_spec=grid_spec,
        out_shape=[jax.ShapeDtypeStruct((MOE_ROWS, D_MODEL), F32),
                   jax.ShapeDtypeStruct((SEQ, D_MODEL), F32)],
        compiler_params=_cparams(("arbitrary",)),
        name="dispatch_shared",
    )(dest_flat, pad_off, pad_len, h2, wsg, wsu, wsd)


def _experts_kernel(bexp_ref, first_ref, slot_ref, next_ref, nused_ref,
                    x_ref, wg_hbm, wu_hbm, wd_hbm, o_ref,
                    wgf_ref, wuf_ref, wdf_ref, wgb_ref, wub_ref, wdb_ref, sem):
    b = pl.program_id(0)
    active = b < nused_ref[0]

    def weight_copies(e, s):
        return (pltpu.make_async_copy(wg_hbm.at[e], wgf_ref.at[s], sem.at[s]),
                pltpu.make_async_copy(wu_hbm.at[e], wuf_ref.at[s], sem.at[s]),
                pltpu.make_async_copy(wd_hbm.at[e], wdf_ref.at[s], sem.at[s]))

    @pl.when(jnp.logical_and(active, b == 0))
    def _():
        for cp in weight_copies(bexp_ref[0], 0):
            cp.start(priority=1)

    @pl.when(jnp.logical_and(active, first_ref[b] == 1))
    def _():
        s = slot_ref[b]
        nxt = next_ref[b]

        @pl.when(nxt < N_EXPERTS)
        def _():
            for cp in weight_copies(nxt, 1 - s):
                cp.start(priority=1)

        for cp in weight_copies(bexp_ref[b], s):
            cp.wait()
        wgb_ref[...] = wgf_ref[s].astype(BF16)
        wub_ref[...] = wuf_ref[s].astype(BF16)
        wdb_ref[...] = wdf_ref[s].astype(BF16)

    @pl.when(active)
    def _():
        x = x_ref[...].astype(BF16)
        gate = _dot(x, wgb_ref[...])
        up = _dot(x, wub_ref[...])
        hid = (_silu(gate) * up).astype(BF16)
        o_ref[...] = _dot(hid, wdb_ref[...]).astype(BF16)


def _experts(block_exp, first, slot, next_exp, nused, xs_perm, w_gate, w_up, w_down):
    bm = MOE_BM

    def xmap(b, be, fi, sl, ne, nu):
        return (jnp.minimum(b, nu[0] - 1), 0)

    hbm = pl.BlockSpec(memory_space=pl.ANY)
    grid_spec = pltpu.PrefetchScalarGridSpec(
        num_scalar_prefetch=5,
        grid=(MOE_NB,),
        in_specs=[pl.BlockSpec((bm, D_MODEL), xmap), hbm, hbm, hbm],
        out_specs=pl.BlockSpec((bm, D_MODEL), xmap),
        scratch_shapes=[
            pltpu.VMEM((2, D_MODEL, EXPERT_DIM), F32),
            pltpu.VMEM((2, D_MODEL, EXPERT_DIM), F32),
            pltpu.VMEM((2, EXPERT_DIM, D_MODEL), F32),
            pltpu.VMEM((D_MODEL, EXPERT_DIM), BF16),
            pltpu.VMEM((D_MODEL, EXPERT_DIM), BF16),
            pltpu.VMEM((EXPERT_DIM, D_MODEL), BF16),
            pltpu.SemaphoreType.DMA((2,)),
        ],
    )
    return pl.pallas_call(
        _experts_kernel,
        grid_spec=grid_spec,
        out_shape=jax.ShapeDtypeStruct((MOE_ROWS, D_MODEL), BF16),
        compiler_params=_cparams(("arbitrary",)),
        name="experts",
    )(block_exp, first, slot, next_exp, nused, xs_perm, w_gate, w_up, w_down)


COMBINE_ALIGN = 16
COMBINE_CHUNK = 512
COMBINE_ROWS = pl.cdiv(COMBINE_TT * TOP_K + 2 * (COMBINE_ALIGN - 1) * N_EXPERTS,
                       COMBINE_CHUNK) * COMBINE_CHUNK
COMBINE_MIN_CHUNKS = pl.cdiv(COMBINE_TT * TOP_K + (COMBINE_ALIGN - 1) * N_EXPERTS, COMBINE_CHUNK)
COMBINE_ISSUE = 32
assert COMBINE_ISSUE * COMBINE_MIN_CHUNKS >= N_EXPERTS


def _combine_kernel(ws_ref, wl_ref, so_ref, ysh_ref, x1_ref, wk_ref, pos_ref, g2_ref, nf_ref, eo_ref,
                    o_ref, buf_ref, acc_ref, sem):
    tt = COMBINE_TT
    i = pl.program_id(0)
    slot = i % 2

    last_tile = pl.num_programs(0) - 1

    def start_window(tile, e, slot_):
        idx = tile * N_EXPERTS + e
        n = pl.multiple_of(wl_ref[idx], COMBINE_ALIGN)
        src = pl.multiple_of(ws_ref[idx], COMBINE_ALIGN)
        dst = pl.multiple_of(slot_ * COMBINE_ROWS + so_ref[idx], COMBINE_ALIGN)
        pltpu.make_async_copy(eo_ref.at[pl.ds(src, n), :], buf_ref.at[pl.ds(dst, n), :],
                              sem.at[slot_]).start()

    def wait_windows(tile, slot_):
        last = tile * N_EXPERTS + N_EXPERTS - 1
        staged_ = pl.multiple_of(so_ref[last] + wl_ref[last], COMBINE_ALIGN)
        base = pl.multiple_of(slot_ * COMBINE_ROWS, COMBINE_ROWS)
        pltpu.make_async_copy(eo_ref.at[pl.ds(0, staged_), :], buf_ref.at[pl.ds(base, staged_), :],
                              sem.at[slot_]).wait()
        return staged_, base

    @pl.when(i == 0)
    def _():
        buf_ref[...] = jnp.zeros_like(buf_ref)

        def body(e, carry):
            start_window(0, e, 0)
            return carry

        lax.fori_loop(0, N_EXPERTS, body, 0)

    nxt = jnp.minimum(i + 1, last_tile)
    staged, sbase = wait_windows(i, slot)
    wk = wk_ref[...]
    pos = pos_ref[...]
    def routed(c):
        col = lax.broadcasted_iota(I32, (tt, COMBINE_CHUNK), 1) + c * COMBINE_CHUNK
        w = jnp.zeros((tt, COMBINE_CHUNK), F32)
        for k in range(TOP_K):
            w = w + jnp.where(col == pos[:, k:k + 1], wk[:, k:k + 1], 0.0)
        w_hi = w.astype(BF16)
        w_lo = (w - w_hi.astype(F32)).astype(BF16)
        rows = buf_ref[pl.ds(sbase + c * COMBINE_CHUNK, COMBINE_CHUNK), :]
        both = _dot(jnp.concatenate([w_hi, w_lo], axis=0), rows)
        return both[:tt] + both[tt:]

    acc = ysh_ref[...]
    for c in range(COMBINE_MIN_CHUNKS):
        for e in range(c * COMBINE_ISSUE, min((c + 1) * COMBINE_ISSUE, N_EXPERTS)):
            start_window(nxt, e, 1 - slot)
        acc = acc + routed(c)
    acc_ref[...] = acc

    @pl.when(i == last_tile)
    def _():
        wait_windows(nxt, 1 - slot)

    for c in range(COMBINE_MIN_CHUNKS, COMBINE_ROWS // COMBINE_CHUNK):
        @pl.when(c * COMBINE_CHUNK < staged)
        def _():
            acc_ref[...] += routed(c)
    x2 = x1_ref[...] + g2_ref[...] * acc_ref[...]
    o_ref[...] = _rms(x2, nf_ref[...])


def _combine(win_start, win_len, stage_off, y_shared, x1, wk_t, pos_t, g2, nf, eo):
    tt = COMBINE_TT
    vec = pl.BlockSpec((1, D_MODEL), lambda i, a, b, c: (0, 0))
    row = pl.BlockSpec((tt, D_MODEL), lambda i, a, b, c: (i, 0))
    slots = pl.BlockSpec((tt, TOP_K), lambda i, a, b, c: (i, 0))
    grid_spec = pltpu.PrefetchScalarGridSpec(
        num_scalar_prefetch=3,
        grid=(N_TILES,),
        in_specs=[row, row, slots, slots, vec, vec, pl.BlockSpec(memory_space=pl.ANY)],
        out_specs=row,
        scratch_shapes=[
            pltpu.VMEM((2 * COMBINE_ROWS, D_MODEL), BF16),
            pltpu.VMEM((tt, D_MODEL), F32),
            pltpu.SemaphoreType.DMA((2,)),
        ],
    )
    return pl.pallas_call(
        _combine_kernel,
        grid_spec=grid_spec,
        out_shape=jax.ShapeDtypeStruct((SEQ, D_MODEL), F32),
        compiler_params=_cparams(("arbitrary",)),
        name="combine_final",
    )(win_start, win_len, stage_off, y_shared, x1, wk_t, pos_t, g2, nf, eo)


def kernel(x, c, w_ada, b_ada, norm1_g, w_in, conv_w, conv_b, dt_bias, a_log, d_skip, ssd_norm_g, gla_w_gate, gla_b_gate, gla_norm_g, w_out, norm2_g, w_router, router_bias, w_e_gate, w_e_up, w_e_down, w_s_gate, w_s_up, w_s_down, normf_g):
    layer = 0
    x2d = x.reshape(SEQ, D_MODEL)
    mod = _ada(c.reshape(D_MODEL, 1), w_ada[layer], b_ada[layer].reshape(1, -1))
    sh1, sc1, g1, sh2, sc2, g2 = [mod[:, i * D_MODEL:(i + 1) * D_MODEL] for i in range(6)]

    w_big, w_small = _pack_w_in(jnp.swapaxes(w_in[layer], 0, 1))

    proj, small = _inproj(x2d, norm1_g[layer].reshape(1, -1), sc1, sh1, w_big, w_small)

    d_exp = jnp.repeat(d_skip[layer], SSD_HEAD_DIM).reshape(1, -1)
    mixed = _mixer(proj, small, conv_w[layer], conv_b[layer].reshape(1, -1),
                   dt_bias[layer].reshape(1, -1), a_log[layer].reshape(1, -1), d_exp,
                   ssd_norm_g[layer].reshape(1, -1), gla_w_gate[layer],
                   gla_b_gate[layer].reshape(1, -1), gla_norm_g[layer].reshape(1, -1))

    x1, h2, logits_t = _outproj(mixed, x2d, w_out[layer].astype(BF16), g1,
                                norm2_g[layer].reshape(1, -1), sc2, sh2, w_router[layer].T)

    ek, wk, rk, counts, cbf, ctf = _route(logits_t, router_bias[layer].reshape(-1, 1))

    counts = counts.reshape(-1)
    padded = (counts + MOE_BM - 1) // MOE_BM * MOE_BM
    pends = jnp.cumsum(padded)
    pstarts = pends - padded
    block_start = jnp.arange(MOE_NB, dtype=I32) * MOE_BM
    block_exp = jnp.minimum(
        jnp.sum((pends[None, :] <= block_start[:, None]).astype(I32), axis=1), N_EXPERTS - 1)
    nused = (pends[-1:] // MOE_BM).astype(I32)
    bidx = jnp.arange(MOE_NB, dtype=I32)
    prev_exp = jnp.concatenate([jnp.full((1,), -1, I32), block_exp[:-1]])
    first = jnp.logical_and(bidx < nused[0], block_exp != prev_exp)
    slot = (jnp.cumsum(first.astype(I32)) - 1) & 1
    later_first = jnp.logical_and(first[None, :], bidx[None, :] > bidx[:, None])
    next_first = jnp.min(jnp.where(later_first, bidx[None, :], MOE_NB), axis=1)
    next_exp = jnp.sum(jnp.where(bidx[None, :] == next_first[:, None], block_exp[None, :], 0), axis=1)
    next_exp = jnp.where(next_first < MOE_NB, next_exp, N_EXPERTS).astype(I32)

    cb = cbf[:, :N_TILES].astype(I32)
    ct = ctf[:, :N_TILES].astype(I32)
    run_start = pstarts[:, None].astype(I32) + cb
    win_start = jnp.where(ct > 0, run_start & -COMBINE_ALIGN, 0)
    win_end = (run_start + ct + COMBINE_ALIGN - 1) & -COMBINE_ALIGN
    win_len = jnp.where(ct > 0, win_end - win_start, COMBINE_ALIGN)
    stage_off = jnp.cumsum(win_len, axis=0) - win_len
    shift = jnp.pad(stage_off - win_start, ((0, 0), (0, TILE_LANES - N_TILES)))
    tile_major = lambda a: a.T.reshape(-1).astype(I32)

    dest, pos = _dest(ek, rk, pstarts.reshape(-1, 1).astype(I32), shift)
    xs_perm, y_shared = _dispatch(
        dest.reshape(-1), (pstarts + counts).astype(I32), (padded - counts).astype(I32), h2,
        w_s_gate[layer].astype(BF16), w_s_up[layer].astype(BF16), w_s_down[layer].astype(BF16))
    eo = _experts(block_exp.astype(I32), first.astype(I32), slot.astype(I32), next_exp, nused, xs_perm,
                  w_e_gate[layer], w_e_up[layer], w_e_down[layer])
    out = _combine(tile_major(win_start), tile_major(win_len), tile_major(stage_off), y_shared, x1,
                   wk.T, pos.T, g2, normf_g.reshape(1, -1), eo)
    return out.reshape(x.shape)
```

```python
import functools

import jax
import jax.numpy as jnp
from jax import lax
from jax.experimental import pallas as pl
from jax.experimental.pallas import tpu as pltpu

F32 = jnp.float32
BF16 = jnp.bfloat16
I32 = jnp.int32

D_MODEL = 2048
SEQ = 8192
NORM_EPS = 1e-6
SSD_HEADS = 32
SSD_HEAD_DIM = 64
SSD_INNER = 2048
SSD_GROUPS = 4
SSD_STATE = 128
SSD_CHUNK = 128
HEADS_PER_GROUP = SSD_HEADS // SSD_GROUPS
GROUP_COLS = HEADS_PER_GROUP * SSD_HEAD_DIM
GLA_HEADS = 4
GLA_KEY_DIM = 1024
GLA_VAL_DIM = 2048
GLA_HEAD_K = 256
GLA_HEAD_V = 512
GLA_GATE_RANK = 16
GLA_GATE_NORM = 16.0
GLA_CHUNK = 64
GLA_STEP = 128
N_EXPERTS = 64
TOP_K = 8
N_EXPERT_GROUPS = 8
TOPK_GROUPS = 4
EXPERT_DIM = 512
ROUTED_SCALE = 2.5

VMEM_LIMIT_BYTES = 56 * 1024 * 1024
SUBLANES = 8

COL_Z, COL_XS, COL_BC, COL_Q, COL_V, COL_R, COL_MA, COL_MB, COL_K = (
    0, 2048, 4096, 5120, 6144, 8192, 10240, 12288, 14336)
PROJ_COLS = 15360
SMALL_COLS = 128

MOE_BM = 512
MOE_NB = SEQ * TOP_K // MOE_BM + N_EXPERTS
MOE_ROWS = MOE_NB * MOE_BM


def _cparams(sem):
    return pltpu.CompilerParams(dimension_semantics=sem, vmem_limit_bytes=VMEM_LIMIT_BYTES)


def _split3(a):
    hi = a.astype(BF16)
    r1 = a - hi.astype(F32)
    mid = r1.astype(BF16)
    lo = (r1 - mid.astype(F32)).astype(BF16)
    return hi, mid, lo


def _dot(a, b):
    return jnp.dot(a, b, preferred_element_type=F32)


def _dot_nt(a, b):
    return lax.dot_general(a, b, (((1,), (1,)), ((), ())), preferred_element_type=F32)


def _dot_tn(a, b):
    return lax.dot_general(a, b, (((0,), (0,)), ((), ())), preferred_element_type=F32)


def _sel_dot(sel_bf16, a):
    hi, mid, lo = _split3(a)
    return _dot(sel_bf16, hi) + _dot(sel_bf16, mid) + _dot(sel_bf16, lo)


def _dot_sel(a, sel_bf16):
    hi, mid, lo = _split3(a)
    return _dot(hi, sel_bf16) + _dot(mid, sel_bf16) + _dot(lo, sel_bf16)


def _dot_sel_stacked(a, sel3):
    hi = a.astype(BF16).astype(F32)
    r1 = a - hi
    mid = r1.astype(BF16).astype(F32)
    lo = (r1 - mid).astype(BF16).astype(F32)
    return _dot(jnp.concatenate([hi, mid, lo], axis=1).astype(BF16), sel3)


def _sel_dot_stacked(sel3, a):
    hi, mid, lo = _split3(a)
    return _dot(sel3, jnp.concatenate([hi, mid, lo], axis=0))


def _dot3(a, b):
    ah = a.astype(BF16)
    al = (a - ah.astype(F32)).astype(BF16)
    bh = b.astype(BF16)
    bl = (b - bh.astype(F32)).astype(BF16)
    return _dot(ah, bh) + _dot(ah, bl) + _dot(al, bh)


def _dot3_nt(a, b):
    ah = a.astype(BF16)
    al = (a - ah.astype(F32)).astype(BF16)
    bh = b.astype(BF16)
    bl = (b - bh.astype(F32)).astype(BF16)
    return _dot_nt(ah, bh) + _dot_nt(ah, bl) + _dot_nt(al, bh)


def _sigmoid(x):
    return 0.5 * jnp.tanh(0.5 * x) + 0.5


def _silu(x):
    h = 0.5 * x
    return h + h * jnp.tanh(h)


def _softplus(x):
    return jnp.maximum(x, 0.0) + jnp.log(1.0 + jnp.exp(-jnp.abs(x)))


def _log_sigmoid(x):
    return jnp.minimum(x, 0.0) - jnp.log(1.0 + jnp.exp(-jnp.abs(x)))


def _rms(x, g):
    ms = jnp.mean(x * x, axis=-1, keepdims=True)
    return x * lax.rsqrt(ms + NORM_EPS) * g


ADA_TN = 1024


def _ada_kernel(c_ref, w_ref, b_ref, o_ref):
    ca = _silu(c_ref[...])
    o_ref[...] = jnp.sum(w_ref[...] * ca, axis=0, keepdims=True) + b_ref[...]


def _ada(c_col, w_ada, b_ada):
    n = w_ada.shape[1]
    return pl.pallas_call(
        _ada_kernel,
        grid=(n // ADA_TN,),
        in_specs=[
            pl.BlockSpec((D_MODEL, 1), lambda j: (0, 0)),
            pl.BlockSpec((D_MODEL, ADA_TN), lambda j: (0, j)),
            pl.BlockSpec((1, ADA_TN), lambda j: (0, j)),
        ],
        out_specs=pl.BlockSpec((1, ADA_TN), lambda j: (0, j)),
        out_shape=jax.ShapeDtypeStruct((1, n), F32),
        compiler_params=_cparams(("arbitrary",)),
        name="ada_mod",
    )(c_col, w_ada, b_ada)


IN_DIM = 15408
PACK_TN = 1024
W_IN_SEGMENTS = ((0, 5120), (5152, 1024), (7200, 4096), (11312, 4096), (6176, 1024))
W_IN_DT_COL = 5120
W_IN_GLR_COL = 11296
PACK_STARTS = tuple(s + t * PACK_TN for s, w in W_IN_SEGMENTS for t in range(w // PACK_TN))


def _pack_kernel(start_ref, dt_ref, glr_ref, wt_hbm, o_ref, small_ref, win_ref, sem):
    j = pl.program_id(0)
    slot = j % 2

    @pl.when(j == 0)
    def _():
        small_ref[...] = jnp.zeros_like(small_ref)
        small_ref[0:SSD_HEADS, :] = dt_ref[...]
        small_ref[SSD_HEADS:SSD_HEADS + GLA_GATE_RANK, :] = glr_ref[...]

    def window_copy(t, s):
        row = pl.multiple_of(start_ref[t], 2 * SUBLANES)
        return pltpu.make_async_copy(wt_hbm.at[pl.ds(row, PACK_TN), :], win_ref.at[s], sem.at[s])

    @pl.when(j == 0)
    def _():
        window_copy(0, 0).start()

    @pl.when(j + 1 < pl.num_programs(0))
    def _():
        window_copy(j + 1, 1 - slot).start()

    window_copy(j, slot).wait()
    o_ref[...] = win_ref[slot].T.astype(BF16)


def _pack_w_in(w_in_t):
    starts = jnp.asarray(PACK_STARTS, I32)
    grid_spec = pltpu.PrefetchScalarGridSpec(
        num_scalar_prefetch=1,
        grid=(len(PACK_STARTS),),
        in_specs=[pl.BlockSpec((SSD_HEADS, D_MODEL), lambda j, s: (W_IN_DT_COL // SSD_HEADS, 0)),
                  pl.BlockSpec((GLA_GATE_RANK, D_MODEL), lambda j, s: (W_IN_GLR_COL // GLA_GATE_RANK, 0)),
                  pl.BlockSpec(memory_space=pl.ANY)],
        out_specs=[pl.BlockSpec((D_MODEL, PACK_TN), lambda j, s: (0, j)),
                   pl.BlockSpec((SMALL_COLS, D_MODEL), lambda j, s: (0, 0))],
        scratch_shapes=[pltpu.VMEM((2, PACK_TN, D_MODEL), F32), pltpu.SemaphoreType.DMA((2,))],
    )
    return pl.pallas_call(
        _pack_kernel,
        grid_spec=grid_spec,
        out_shape=[jax.ShapeDtypeStruct((D_MODEL, PROJ_COLS), BF16),
                   jax.ShapeDtypeStruct((SMALL_COLS, D_MODEL), F32)],
        compiler_params=_cparams(("arbitrary",)),
        name="pack_w_in",
    )(starts, w_in_t, w_in_t, w_in_t)


INPROJ_TM = 1024
INPROJ_TN = 1024


def _inproj_kernel(x_ref, g_ref, sc_ref, sh_ref, wb_ref, ws_ref, o_ref, os_ref, h_ref):
    @pl.when(pl.program_id(1) == 0)
    def _():
        h = _rms(x_ref[...], g_ref[...]) * (1.0 + sc_ref[...]) + sh_ref[...]
        hb = h.astype(BF16)
        h_ref[...] = hb
        os_ref[...] = _dot_nt(hb, ws_ref[...].astype(BF16))

    o_ref[...] = _dot(h_ref[...], wb_ref[...]).astype(BF16)


def _inproj(x2d, g, sc, sh, w_big, w_small):
    vec = pl.BlockSpec((1, D_MODEL), lambda i, j: (0, 0))
    return pl.pallas_call(
        _inproj_kernel,
        grid=(SEQ // INPROJ_TM, PROJ_COLS // INPROJ_TN),
        in_specs=[
            pl.BlockSpec((INPROJ_TM, D_MODEL), lambda i, j: (i, 0)),
            vec, vec, vec,
            pl.BlockSpec((D_MODEL, INPROJ_TN), lambda i, j: (0, j)),
            pl.BlockSpec((SMALL_COLS, D_MODEL), lambda i, j: (0, 0)),
        ],
        out_specs=[
            pl.BlockSpec((INPROJ_TM, INPROJ_TN), lambda i, j: (i, j)),
            pl.BlockSpec((INPROJ_TM, SMALL_COLS), lambda i, j: (i, 0)),
        ],
        out_shape=[
            jax.ShapeDtypeStruct((SEQ, PROJ_COLS), BF16),
            jax.ShapeDtypeStruct((SEQ, SMALL_COLS), F32),
        ],
        scratch_shapes=[pltpu.VMEM((INPROJ_TM, D_MODEL), BF16)],
        compiler_params=_cparams(("arbitrary", "arbitrary")),
        name="inproj",
    )(x2d, g, sc, sh, w_big, w_small)


def _causal_conv_silu(cur, hist_ref, w_ref, b_ref):
    L = cur.shape[0]
    hist_ref[SUBLANES:, :] = cur
    acc = cur * w_ref[3:4, :] + b_ref[...]
    for s in (1, 2, 3):
        acc = acc + hist_ref[SUBLANES - s:SUBLANES - s + L, :] * w_ref[3 - s:4 - s, :]
    hist_ref[:SUBLANES, :] = cur[L - SUBLANES:]
    return _silu(acc)


def _ssd_chunk(xs_ref, bc_ref, z_ref, ma_ref, sm_ref, cwx_ref, cwb_ref, cbx_ref, cbb_ref,
               dtb_ref, alog_ref, dexp_ref, ng_ref, px_ref, pb_ref, st_ref):
    L = SSD_CHUNK
    xs_raw = xs_ref[...].astype(F32)
    bc_raw = bc_ref[...].astype(F32)
    xs = _causal_conv_silu(xs_raw, px_ref, cwx_ref, cbx_ref)
    bc = _causal_conv_silu(bc_raw, pb_ref, cwb_ref, cbb_ref)

    dt = _softplus(sm_ref[:, 0:SSD_HEADS] + dtb_ref[...])
    d_a = dt * (-jnp.exp(alog_ref[...]))
    ri = lax.broadcasted_iota(I32, (L, L), 0)
    ci = lax.broadcasted_iota(I32, (L, L), 1)
    causal = ri >= ci
    tril = jnp.where(causal, 1.0, 0.0).astype(BF16)
    acum = _sel_dot(tril, d_a)
    hi32 = lax.broadcasted_iota(I32, (SSD_HEADS, SSD_HEADS), 0)
    hj32 = lax.broadcasted_iota(I32, (SSD_HEADS, SSD_HEADS), 1)
    eye32 = jnp.where(hi32 == hj32, 1.0, 0.0).astype(BF16)
    a_hi, a_mid, a_lo = _split3(acum)
    acum_t = _dot_nt(eye32, a_hi) + _dot_nt(eye32, a_mid) + _dot_nt(eye32, a_lo)
    last = acum[L - 1:L, :]
    e_a = jnp.exp(acum)
    to_end = jnp.exp(last - acum)
    eh = lax.broadcasted_iota(I32, (SSD_HEADS, SSD_INNER), 0)
    ec = lax.broadcasted_iota(I32, (SSD_HEADS, SSD_INNER), 1)
    expand = jnp.where((ec >> 6) == eh, 1.0, 0.0).astype(BF16)
    expand3 = jnp.concatenate([expand, expand, expand], axis=0)
    dt_x = _dot_sel_stacked(dt, expand3)
    dte_x = _dot_sel_stacked(dt * to_end, expand3)
    ea_x = _dot_sel_stacked(e_a, expand3)
    cd_x = ea_x[L - 1:L, :]

    xdt = xs * dt_x
    lane = lax.broadcasted_iota(I32, xdt.shape, 1)
    left = (lane & 64) == 0
    xdt_l = jnp.where(left, xdt, 0.0).astype(BF16)
    xdt_r = jnp.where(left, 0.0, xdt).astype(BF16)
    xdte = (xs * dte_x).astype(BF16)

    y_diag = []
    y_off = []
    for g in range(SSD_GROUPS):
        bg = bc[:, g * SSD_STATE:(g + 1) * SSD_STATE].astype(BF16)
        cg = bc[:, 512 + g * SSD_STATE:512 + (g + 1) * SSD_STATE].astype(BF16)
        cb = _dot_nt(cg, bg)
        gsl = slice(g * GROUP_COLS, (g + 1) * GROUP_COLS)
        st = st_ref[g]
        y_off.append(_dot(cg, st.astype(BF16)))
        st_ref[g] = st * cd_x[:, gsl] + _dot_tn(bg, xdte[:, gsl])
        for p in range(HEADS_PER_GROUP // 2):
            ms = []
            for h in (g * HEADS_PER_GROUP + 2 * p, g * HEADS_PER_GROUP + 2 * p + 1):
                seg = acum[:, h:h + 1] - acum_t[h:h + 1, :]
                decay = jnp.exp(jnp.where(causal, seg, -jnp.inf))
                ms.append((cb * decay).astype(BF16))
            m_cat = jnp.concatenate(ms, axis=1)
            psl = slice((g * 4 + p) * 128, (g * 4 + p + 1) * 128)
            x_cat = jnp.concatenate([xdt_l[:, psl], xdt_r[:, psl]], axis=0)
            y_diag.append(_dot(m_cat, x_cat))

    y = (jnp.concatenate(y_diag, axis=1) + jnp.concatenate(y_off, axis=1) * ea_x
         + dexp_ref[...] * xs)
    yf = y * _silu(z_ref[...].astype(F32))
    outs = []
    for g in range(SSD_GROUPS):
        seg = yf[:, g * GROUP_COLS:(g + 1) * GROUP_COLS]
        ms = jnp.mean(seg * seg, axis=-1, keepdims=True)
        outs.append(seg * lax.rsqrt(ms + NORM_EPS))
    y_ssd = jnp.concatenate(outs, axis=1) * ng_ref[...]
    return _sigmoid(ma_ref[...].astype(F32)) * y_ssd


def _gla_chunks(q_ref, k_ref, v_ref, r_ref, mb_ref, sm_ref, wg_ref, bg_ref, ng_ref, s_ref, y_ssd, o_ref):
    L = GLA_CHUNK
    ri = lax.broadcasted_iota(I32, (L, L), 0)
    ci = lax.broadcasted_iota(I32, (L, L), 1)
    causal = ri >= ci
    r3 = lax.broadcasted_iota(I32, (L, 3 * L), 0)
    c3 = lax.broadcasted_iota(I32, (L, 3 * L), 1)
    tril3 = jnp.where(r3 >= c3 % L, 1.0, 0.0).astype(BF16)

    for c in range(GLA_STEP // L):
        rs = slice(c * L, (c + 1) * L)
        glr = sm_ref[rs, SSD_HEADS:SSD_HEADS + GLA_GATE_RANK]
        pre = _dot3(glr, wg_ref[...]) + bg_ref[...]
        gk = _log_sigmoid(pre) / GLA_GATE_NORM
        bcum = _sel_dot_stacked(tril3, gk)
        bmid = bcum[L // 2:L // 2 + 1, :]
        blast = bcum[L - 1:L, :]
        q = q_ref[rs, :].astype(F32) * (GLA_HEAD_K ** -0.5)
        k = k_ref[rs, :].astype(F32)
        q_rel = (q * jnp.exp(bcum - bmid)).astype(BF16)
        k_rel = (k * jnp.exp(bmid - bcum)).astype(BF16)
        q_int = (q * jnp.exp(bcum)).astype(BF16)
        k_end = (k * jnp.exp(blast - bcum)).astype(BF16)
        dec = jnp.exp(blast)
        v = v_ref[rs, :]

        outs = []
        for h in range(GLA_HEADS):
            ks = slice(h * GLA_HEAD_K, (h + 1) * GLA_HEAD_K)
            vs = slice(h * GLA_HEAD_V, (h + 1) * GLA_HEAD_V)
            att = jnp.where(causal, _dot_nt(q_rel[:, ks], k_rel[:, ks]), 0.0)
            s_t = s_ref[h]
            o_h = _dot(att.astype(BF16), v[:, vs]) + _dot_nt(q_int[:, ks], s_t.astype(BF16))
            s_ref[h] = s_t * dec[:, ks] + _dot_tn(v[:, vs], k_end[:, ks])
            outs.append(_rms(o_h, ng_ref[...]))
        o = jnp.concatenate(outs, axis=1)
        o = o * _silu(r_ref[rs, :].astype(F32)) * _sigmoid(mb_ref[rs, :].astype(F32))
        o_ref[rs, :] = (y_ssd[rs, :] + o).astype(BF16)


N_SSD_IN = 13
N_GLA_IN = 9


def _mixer_kernel(*refs):
    ssd_in = refs[:N_SSD_IN]
    gla_in = refs[N_SSD_IN:N_SSD_IN + N_GLA_IN]
    o_ref, px_ref, pb_ref, st_ref, s_ref = refs[N_SSD_IN + N_GLA_IN:]

    @pl.when(pl.program_id(0) == 0)
    def _():
        px_ref[...] = jnp.zeros_like(px_ref)
        pb_ref[...] = jnp.zeros_like(pb_ref)
        st_ref[...] = jnp.zeros_like(st_ref)
        s_ref[...] = jnp.zeros_like(s_ref)

    y_ssd = _ssd_chunk(*ssd_in, px_ref, pb_ref, st_ref)
    _gla_chunks(*gla_in, s_ref, y_ssd, o_ref)


def _mixer(proj, small, conv_w, conv_b, dt_bias, a_log, d_exp, ssd_norm_g, w_gate, b_gate, gla_norm_g):
    L = SSD_CHUNK
    full = lambda shape: pl.BlockSpec(shape, lambda i: (0, 0))
    cols = lambda width, col: pl.BlockSpec((L, width), lambda i: (i, col // width))
    ssd_specs = [
        cols(2048, COL_XS), cols(1024, COL_BC), cols(2048, COL_Z), cols(2048, COL_MA),
        pl.BlockSpec((L, SMALL_COLS), lambda i: (i, 0)),
        pl.BlockSpec((4, 2048), lambda i: (0, 0)),
        pl.BlockSpec((4, 1024), lambda i: (0, 2)),
        pl.BlockSpec((1, 2048), lambda i: (0, 0)),
        pl.BlockSpec((1, 1024), lambda i: (0, 2)),
        full((1, SSD_HEADS)), full((1, SSD_HEADS)),
        full((1, SSD_INNER)), full((1, SSD_INNER)),
    ]
    gla_specs = [
        cols(1024, COL_Q), cols(1024, COL_K), cols(2048, COL_V), cols(2048, COL_R), cols(2048, COL_MB),
        pl.BlockSpec((L, SMALL_COLS), lambda i: (i, 0)),
        full((GLA_GATE_RANK, GLA_KEY_DIM)), full((1, GLA_KEY_DIM)), full((1, GLA_HEAD_V)),
    ]
    assert len(ssd_specs) == N_SSD_IN and len(gla_specs) == N_GLA_IN and GLA_STEP == L
    return pl.pallas_call(
        _mixer_kernel,
        grid=(SEQ // L,),
        in_specs=ssd_specs + gla_specs,
        out_specs=pl.BlockSpec((L, D_MODEL), lambda i: (i, 0)),
        out_shape=jax.ShapeDtypeStruct((SEQ, D_MODEL), BF16),
        scratch_shapes=[
            pltpu.VMEM((SUBLANES + L, 2048), F32),
            pltpu.VMEM((SUBLANES + L, 1024), F32),
            pltpu.VMEM((SSD_GROUPS, SSD_STATE, GROUP_COLS), F32),
            pltpu.VMEM((GLA_HEADS, GLA_HEAD_V, GLA_HEAD_K), F32),
        ],
        compiler_params=_cparams(("arbitrary",)),
        name="mixer",
    )(proj, proj, proj, proj, small, conv_w, conv_w, conv_b, conv_b, dt_bias, a_log, d_exp, ssd_norm_g,
      proj, proj, proj, proj, proj, small, w_gate, b_gate, gla_norm_g)


OUTPROJ_TM = 512


def _outproj_kernel(m_ref, x_ref, w_ref, g1_ref, n2_ref, sc_ref, sh_ref, wr_ref,
                    x1_ref, h2_ref, lg_ref):
    x1 = x_ref[...] + g1_ref[...] * _dot(m_ref[...], w_ref[...])
    x1_ref[...] = x1
    h2 = _rms(x1, n2_ref[...]) * (1.0 + sc_ref[...]) + sh_ref[...]
    h2_ref[...] = h2
    lg_ref[...] = _dot3_nt(wr_ref[...], h2)


def _outproj(mixed, x2d, w_out_bf, g1, n2g, sc2, sh2, w_router_t):
    tm = OUTPROJ_TM
    vec = pl.BlockSpec((1, D_MODEL), lambda i: (0, 0))
    row = pl.BlockSpec((tm, D_MODEL), lambda i: (i, 0))
    return pl.pallas_call(
        _outproj_kernel,
        grid=(SEQ // tm,),
        in_specs=[row, row,
                  pl.BlockSpec((D_MODEL, D_MODEL), lambda i: (0, 0), pipeline_mode=pl.Buffered(1)),
                  vec, vec, vec, vec,
                  pl.BlockSpec((N_EXPERTS, D_MODEL), lambda i: (0, 0))],
        out_specs=[row, row, pl.BlockSpec((N_EXPERTS, tm), lambda i: (0, i))],
        out_shape=[jax.ShapeDtypeStruct((SEQ, D_MODEL), F32),
                   jax.ShapeDtypeStruct((SEQ, D_MODEL), F32),
                   jax.ShapeDtypeStruct((N_EXPERTS, SEQ), F32)],
        compiler_params=_cparams(("arbitrary",)),
        name="outproj_router",
    )(mixed, x2d, w_out_bf, g1, n2g, sc2, sh2, w_router_t)


ROUTE_TN = 512
COMBINE_TT = 256
N_TILES = SEQ // COMBINE_TT
TILE_LANES = 128


def _route_kernel(lg_ref, bias_ref, ek_ref, wk_ref, rk_ref, cnt_ref, cb_ref, ct_ref, carry_ref):
    tn = ROUTE_TN
    gsz = N_EXPERTS // N_EXPERT_GROUPS

    @pl.when(pl.program_id(0) == 0)
    def _():
        carry_ref[...] = jnp.zeros_like(carry_ref)
        cb_ref[...] = jnp.zeros_like(cb_ref)
        ct_ref[...] = jnp.zeros_like(ct_ref)

    s = jax.nn.sigmoid(lg_ref[...])
    choice = s + bias_ref[...]
    r8 = lax.broadcasted_iota(I32, (gsz, tn), 0)
    neg = -jnp.inf

    def top1(cur, rows, nrows):
        m = jnp.max(cur, axis=0, keepdims=True)
        idx = jnp.min(jnp.where(cur == m, rows, nrows), axis=0, keepdims=True)
        return m, idx

    gscores = []
    for g in range(N_EXPERT_GROUPS):
        blk = choice[g * gsz:(g + 1) * gsz, :]
        m1, i1 = top1(blk, r8, gsz)
        m2, _ = top1(jnp.where(r8 == i1, neg, blk), r8, gsz)
        gscores.append(m1 + m2)
    cur = jnp.concatenate(gscores, axis=0)
    gsel = jnp.zeros((N_EXPERT_GROUPS, tn), F32)
    for _ in range(TOPK_GROUPS):
        _, idx = top1(cur, r8, N_EXPERT_GROUPS)
        hit = r8 == idx
        gsel = jnp.where(hit, 1.0, gsel)
        cur = jnp.where(hit, neg, cur)
    emask = jnp.concatenate(
        [jnp.broadcast_to(gsel[g:g + 1, :], (gsz, tn)) for g in range(N_EXPERT_GROUPS)], axis=0)
    rows = lax.broadcasted_iota(I32, (N_EXPERTS, tn), 0)
    cur = jnp.where(emask > 0.5, choice, neg)
    sel = jnp.zeros((N_EXPERTS, tn), F32)
    hits, eks, wks = [], [], []
    for _ in range(TOP_K):
        _, idx = top1(cur, rows, N_EXPERTS)
        hit = rows == idx
        hits.append(hit)
        eks.append(idx)
        wks.append(jnp.sum(jnp.where(hit, s, 0.0), axis=0, keepdims=True))
        cur = jnp.where(hit, neg, cur)
        sel = jnp.where(hit, 1.0, sel)
    wsum = wks[0]
    for w in wks[1:]:
        wsum = wsum + w
    scale = ROUTED_SCALE / (wsum + 1e-20)

    ti = lax.broadcasted_iota(I32, (tn, tn), 0)
    tj = lax.broadcasted_iota(I32, (tn, tn), 1)
    before = jnp.where(ti < tj, 1.0, 0.0).astype(BF16)
    rank = _dot(sel.astype(BF16), before) + carry_ref[...]
    rks = [jnp.sum(jnp.where(hit, rank, 0.0), axis=0, keepdims=True) for hit in hits]

    lane = lax.broadcasted_iota(I32, (N_EXPERTS, TILE_LANES), 1)
    carry = carry_ref[...]
    cb = cb_ref[...]
    ct = ct_ref[...]
    for sub in range(tn // COMBINE_TT):
        cnt = jnp.sum(sel[:, sub * COMBINE_TT:(sub + 1) * COMBINE_TT], axis=1, keepdims=True)
        col = pl.program_id(0) * (tn // COMBINE_TT) + sub
        cb = jnp.where(lane == col, carry, cb)
        ct = jnp.where(lane == col, cnt, ct)
        carry = carry + cnt
    cb_ref[...] = cb
    ct_ref[...] = ct
    carry_ref[...] = carry

    ek_ref[...] = jnp.concatenate(eks, axis=0)
    wk_ref[...] = jnp.concatenate(wks, axis=0) * scale
    rk_ref[...] = jnp.concatenate(rks, axis=0).astype(I32)
    cnt_ref[...] = carry_ref[...].astype(I32)


def _route(logits_t, bias_col):
    tn = ROUTE_TN
    kt = pl.BlockSpec((TOP_K, tn), lambda i: (0, i))
    return pl.pallas_call(
        _route_kernel,
        grid=(SEQ // tn,),
        in_specs=[pl.BlockSpec((N_EXPERTS, tn), lambda i: (0, i)),
                  pl.BlockSpec((N_EXPERTS, 1), lambda i: (0, 0))],
        out_specs=[kt, kt, kt, pl.BlockSpec((N_EXPERTS, 1), lambda i: (0, 0)),
                   pl.BlockSpec((N_EXPERTS, TILE_LANES), lambda i: (0, 0)),
                   pl.BlockSpec((N_EXPERTS, TILE_LANES), lambda i: (0, 0))],
        out_shape=[jax.ShapeDtypeStruct((TOP_K, SEQ), I32),
                   jax.ShapeDtypeStruct((TOP_K, SEQ), F32),
                   jax.ShapeDtypeStruct((TOP_K, SEQ), I32),
                   jax.ShapeDtypeStruct((N_EXPERTS, 1), I32),
                   jax.ShapeDtypeStruct((N_EXPERTS, TILE_LANES), F32),
                   jax.ShapeDtypeStruct((N_EXPERTS, TILE_LANES), F32)],
        scratch_shapes=[pltpu.VMEM((N_EXPERTS, 1), F32)],
        compiler_params=_cparams(("arbitrary",)),
        name="route",
    )(logits_t, bias_col)


def _dest_kernel(ek_ref, rk_ref, ps_ref, shift_ref, o_ref, pos_ref):
    tt = COMBINE_TT
    rows = lax.broadcasted_iota(I32, (N_EXPERTS, tt), 0)
    lane = lax.broadcasted_iota(I32, (N_EXPERTS, TILE_LANES), 1)
    dests, poss = [], []
    for sub in range(ROUTE_TN // tt):
        tile = pl.program_id(0) * (ROUTE_TN // tt) + sub
        shift_col = jnp.sum(jnp.where(lane == tile, shift_ref[...], 0), axis=1, keepdims=True)
        ds, ps = [], []
        for k in range(TOP_K):
            hit = rows == ek_ref[k:k + 1, sub * tt:(sub + 1) * tt]
            d = (jnp.sum(jnp.where(hit, ps_ref[...], 0), axis=0, keepdims=True)
                 + rk_ref[k:k + 1, sub * tt:(sub + 1) * tt])
            ds.append(d)
            ps.append(d + jnp.sum(jnp.where(hit, shift_col, 0), axis=0, keepdims=True))
        dests.append(jnp.concatenate(ds, axis=0))
        poss.append(jnp.concatenate(ps, axis=0))
    o_ref[...] = jnp.concatenate(dests, axis=1)
    pos_ref[...] = jnp.concatenate(poss, axis=1)


def _dest(ek, rk, pstarts_col, shift):
    tn = ROUTE_TN
    kt = pl.BlockSpec((TOP_K, tn), lambda i: (0, i))
    return pl.pallas_call(
        _dest_kernel,
        grid=(SEQ // tn,),
        in_specs=[kt, kt, pl.BlockSpec((N_EXPERTS, 1), lambda i: (0, 0)),
                  pl.BlockSpec((N_EXPERTS, TILE_LANES), lambda i: (0, 0))],
        out_specs=[kt, kt],
        out_shape=[jax.ShapeDtypeStruct((TOP_K, SEQ), I32), jax.ShapeDtypeStruct((TOP_K, SEQ), I32)],
        compiler_params=_cparams(("arbitrary",)),
        name="dest_rows",
    )(ek, rk, pstarts_col, shift)


DISPATCH_TT = 256


def _dispatch_kernel(dest_ref, padoff_ref, padlen_ref, h_ref, wg_ref, wu_ref, wd_ref,
                     out_ref, ysh_ref, zeros_ref, sem, zsem):
    tt = DISPATCH_TT
    i = pl.program_id(0)
    base = i * tt

    def issue(t, carry):
        for k in range(TOP_K):
            d = dest_ref[k * SEQ + base + t]
            pltpu.make_async_copy(h_ref.at[pl.ds(t, 1), :], out_ref.at[pl.ds(d, 1), :],
                                  sem).start(priority=k % 2)
        return carry

    lax.fori_loop(0, tt, issue, 0)

    def pad_copy(off, n):
        return pltpu.make_async_copy(zeros_ref.at[pl.ds(0, n), :], out_ref.at[pl.ds(off, n), :], zsem)

    def pad_rows(wait):
        def body(e, carry):
            start = padoff_ref[e]
            head = (-start) & (SUBLANES - 1)
            for j in range(SUBLANES - 1):
                @pl.when(j < head)
                def _():
                    cp = pad_copy(start + j, 1)
                    cp.wait() if wait else cp.start()
            n = pl.multiple_of(padlen_ref[e] - head, SUBLANES)

            @pl.when(n > 0)
            def _():
                cp = pad_copy(pl.multiple_of(start + head, SUBLANES), n)
                cp.wait() if wait else cp.start()
            return carry
        lax.fori_loop(0, N_EXPERTS, body, 0)

    @pl.when(i == 0)
    def _():
        zeros_ref[...] = jnp.zeros_like(zeros_ref)
        pad_rows(wait=False)

    @pl.when(i == pl.num_programs(0) - 1)
    def _():
        pad_rows(wait=True)

    h = h_ref[...].astype(BF16)
    hid = (_silu(_dot(h, wg_ref[...])) * _dot(h, wu_ref[...])).astype(BF16)
    ysh_ref[...] = _dot(hid, wd_ref[...])

    for _ in range(TOP_K):
        pltpu.make_async_copy(h_ref, out_ref.at[pl.ds(0, tt), :], sem).wait()


def _dispatch(dest_flat, pad_off, pad_len, h2, wsg, wsu, wsd):
    tt = DISPATCH_TT
    const = lambda shape: pl.BlockSpec(shape, lambda i, d, po, pn: (0, 0))
    grid_spec = pltpu.PrefetchScalarGridSpec(
        num_scalar_prefetch=3,
        grid=(SEQ // tt,),
        in_specs=[pl.BlockSpec((tt, D_MODEL), lambda i, d, po, pn: (i, 0)),
                  const((D_MODEL, EXPERT_DIM)), const((D_MODEL, EXPERT_DIM)), const((EXPERT_DIM, D_MODEL))],
        out_specs=[pl.BlockSpec(memory_space=pl.ANY),
                   pl.BlockSpec((tt, D_MODEL), lambda i, d, po, pn: (i, 0))],
        scratch_shapes=[
            pltpu.VMEM((MOE_BM, D_MODEL), F32),
            pltpu.SemaphoreType.DMA,
            pltpu.SemaphoreType.DMA,
        ],
    )
    return pl.pallas_call(
        _dispatch_kernel,
        grid_spec=grid_spec,
        out_shape=[jax.ShapeDtypeStruct((MOE_ROWS, D_MODEL), F32),
                   jax.ShapeDtypeStruct((SEQ, D_MODEL), F32)],
        compiler_params=_cparams(("arbitrary",)),
        name="dispatch_shared",
    )(dest_flat, pad_off, pad_len, h2, wsg, wsu, wsd)


def _experts_kernel(bexp_ref, first_ref, slot_ref, next_ref, nused_ref,
                    x_ref, wg_hbm, wu_hbm, wd_hbm, o_ref,
                    wgf_ref, wuf_ref, wdf_ref, wgb_ref, wub_ref, wdb_ref, sem):
    b = pl.program_id(0)
    active = b < nused_ref[0]

    def weight_copies(e, s):
        return (pltpu.make_async_copy(wg_hbm.at[e], wgf_ref.at[s], sem.at[s]),
                pltpu.make_async_copy(wu_hbm.at[e], wuf_ref.at[s], sem.at[s]),
                pltpu.make_async_copy(wd_hbm.at[e], wdf_ref.at[s], sem.at[s]))

    @pl.when(jnp.logical_and(active, b == 0))
    def _():
        for cp in weight_copies(bexp_ref[0], 0):
            cp.start(priority=1)

    @pl.when(jnp.logical_and(active, first_ref[b] == 1))
    def _():
        s = slot_ref[b]
        nxt = next_ref[b]

        @pl.when(nxt < N_EXPERTS)
        def _():
            for cp in weight_copies(nxt, 1 - s):
                cp.start(priority=1)

        for cp in weight_copies(bexp_ref[b], s):
            cp.wait()
        wgb_ref[...] = wgf_ref[s].astype(BF16)
        wub_ref[...] = wuf_ref[s].astype(BF16)
        wdb_ref[...] = wdf_ref[s].astype(BF16)

    @pl.when(active)
    def _():
        x = x_ref[...].astype(BF16)
        gate = _dot(x, wgb_ref[...])
        up = _dot(x, wub_ref[...])
        hid = (_silu(gate) * up).astype(BF16)
        o_ref[...] = _dot(hid, wdb_ref[...]).astype(BF16)


def _experts(block_exp, first, slot, next_exp, nused, xs_perm, w_gate, w_up, w_down):
    bm = MOE_BM

    def xmap(b, be, fi, sl, ne, nu):
        return (jnp.minimum(b, nu[0] - 1), 0)

    hbm = pl.BlockSpec(memory_space=pl.ANY)
    grid_spec = pltpu.PrefetchScalarGridSpec(
        num_scalar_prefetch=5,
        grid=(MOE_NB,),
        in_specs=[pl.BlockSpec((bm, D_MODEL), xmap), hbm, hbm, hbm],
        out_specs=pl.BlockSpec((bm, D_MODEL), xmap),
        scratch_shapes=[
            pltpu.VMEM((2, D_MODEL, EXPERT_DIM), F32),
            pltpu.VMEM((2, D_MODEL, EXPERT_DIM), F32),
            pltpu.VMEM((2, EXPERT_DIM, D_MODEL), F32),
            pltpu.VMEM((D_MODEL, EXPERT_DIM), BF16),
            pltpu.VMEM((D_MODEL, EXPERT_DIM), BF16),
            pltpu.VMEM((EXPERT_DIM, D_MODEL), BF16),
            pltpu.SemaphoreType.DMA((2,)),
        ],
    )
    return pl.pallas_call(
        _experts_kernel,
        grid_spec=grid_spec,
        out_shape=jax.ShapeDtypeStruct((MOE_ROWS, D_MODEL), BF16),
        compiler_params=_cparams(("arbitrary",)),
        name="experts",
    )(block_exp, first, slot, next_exp, nused, xs_perm, w_gate, w_up, w_down)


COMBINE_ALIGN = 16
COMBINE_CHUNK = 512
COMBINE_ROWS = pl.cdiv(COMBINE_TT * TOP_K + 2 * (COMBINE_ALIGN - 1) * N_EXPERTS,
                       COMBINE_CHUNK) * COMBINE_CHUNK
COMBINE_MIN_CHUNKS = pl.cdiv(COMBINE_TT * TOP_K + (COMBINE_ALIGN - 1) * N_EXPERTS, COMBINE_CHUNK)
COMBINE_ISSUE = 32
assert COMBINE_ISSUE * COMBINE_MIN_CHUNKS >= N_EXPERTS


def _combine_kernel(ws_ref, wl_ref, so_ref, ysh_ref, x1_ref, wk_ref, pos_ref, g2_ref, nf_ref, eo_ref,
                    o_ref, buf_ref, acc_ref, sem):
    tt = COMBINE_TT
    i = pl.program_id(0)
    slot = i % 2

    last_tile = pl.num_programs(0) - 1

    def start_window(tile, e, slot_):
        idx = tile * N_EXPERTS + e
        n = pl.multiple_of(wl_ref[idx], COMBINE_ALIGN)
        src = pl.multiple_of(ws_ref[idx], COMBINE_ALIGN)
        dst = pl.multiple_of(slot_ * COMBINE_ROWS + so_ref[idx], COMBINE_ALIGN)
        pltpu.make_async_copy(eo_ref.at[pl.ds(src, n), :], buf_ref.at[pl.ds(dst, n), :],
                              sem.at[slot_]).start()

    def wait_windows(tile, slot_):
        last = tile * N_EXPERTS + N_EXPERTS - 1
        staged_ = pl.multiple_of(so_ref[last] + wl_ref[last], COMBINE_ALIGN)
        base = pl.multiple_of(slot_ * COMBINE_ROWS, COMBINE_ROWS)
        pltpu.make_async_copy(eo_ref.at[pl.ds(0, staged_), :], buf_ref.at[pl.ds(base, staged_), :],
                              sem.at[slot_]).wait()
        return staged_, base

    @pl.when(i == 0)
    def _():
        buf_ref[...] = jnp.zeros_like(buf_ref)

        def body(e, carry):
            start_window(0, e, 0)
            return carry

        lax.fori_loop(0, N_EXPERTS, body, 0)

    nxt = jnp.minimum(i + 1, last_tile)
    staged, sbase = wait_windows(i, slot)
    wk = wk_ref[...]
    pos = pos_ref[...]
    def routed(c):
        col = lax.broadcasted_iota(I32, (tt, COMBINE_CHUNK), 1) + c * COMBINE_CHUNK
        w = jnp.zeros((tt, COMBINE_CHUNK), F32)
        for k in range(TOP_K):
            w = w + jnp.where(col == pos[:, k:k + 1], wk[:, k:k + 1], 0.0)
        w_hi = w.astype(BF16)
        w_lo = (w - w_hi.astype(F32)).astype(BF16)
        rows = buf_ref[pl.ds(sbase + c * COMBINE_CHUNK, COMBINE_CHUNK), :]
        both = _dot(jnp.concatenate([w_hi, w_lo], axis=0), rows)
        return both[:tt] + both[tt:]

    acc = ysh_ref[...]
    for c in range(COMBINE_MIN_CHUNKS):
        for e in range(c * COMBINE_ISSUE, min((c + 1) * COMBINE_ISSUE, N_EXPERTS)):
            start_window(nxt, e, 1 - slot)
        acc = acc + routed(c)
    acc_ref[...] = acc

    @pl.when(i == last_tile)
    def _():
        wait_windows(nxt, 1 - slot)

    for c in range(COMBINE_MIN_CHUNKS, COMBINE_ROWS // COMBINE_CHUNK):
        @pl.when(c * COMBINE_CHUNK < staged)
        def _():
            acc_ref[...] += routed(c)
    x2 = x1_ref[...] + g2_ref[...] * acc_ref[...]
    o_ref[...] = _rms(x2, nf_ref[...])


def _combine(win_start, win_len, stage_off, y_shared, x1, wk_t, pos_t, g2, nf, eo):
    tt = COMBINE_TT
    vec = pl.BlockSpec((1, D_MODEL), lambda i, a, b, c: (0, 0))
    row = pl.BlockSpec((tt, D_MODEL), lambda i, a, b, c: (i, 0))
    slots = pl.BlockSpec((tt, TOP_K), lambda i, a, b, c: (i, 0))
    grid_spec = pltpu.PrefetchScalarGridSpec(
        num_scalar_prefetch=3,
        grid=(N_TILES,),
        in_specs=[row, row, slots, slots, vec, vec, pl.BlockSpec(memory_space=pl.ANY)],
        out_specs=row,
        scratch_shapes=[
            pltpu.VMEM((2 * COMBINE_ROWS, D_MODEL), BF16),
            pltpu.VMEM((tt, D_MODEL), F32),
            pltpu.SemaphoreType.DMA((2,)),
        ],
    )
    return pl.pallas_call(
        _combine_kernel,
        grid_spec=grid_spec,
        out_shape=jax.ShapeDtypeStruct((SEQ, D_MODEL), F32),
        compiler_params=_cparams(("arbitrary",)),
        name="combine_final",
    )(win_start, win_len, stage_off, y_shared, x1, wk_t, pos_t, g2, nf, eo)


def kernel(x, c, w_ada, b_ada, norm1_g, w_in, conv_w, conv_b, dt_bias, a_log, d_skip, ssd_norm_g, gla_w_gate, gla_b_gate, gla_norm_g, w_out, norm2_g, w_router, router_bias, w_e_gate, w_e_up, w_e_down, w_s_gate, w_s_up, w_s_down, normf_g):
    layer = 0
    x2d = x.reshape(SEQ, D_MODEL)
    mod = _ada(c.reshape(D_MODEL, 1), w_ada[layer], b_ada[layer].reshape(1, -1))
    sh1, sc1, g1, sh2, sc2, g2 = [mod[:, i * D_MODEL:(i + 1) * D_MODEL] for i in range(6)]

    w_big, w_small = _pack_w_in(jnp.swapaxes(w_in[layer], 0, 1))

    proj, small = _inproj(x2d, norm1_g[layer].reshape(1, -1), sc1, sh1, w_big, w_small)

    d_exp = jnp.repeat(d_skip[layer], SSD_HEAD_DIM).reshape(1, -1)
    mixed = _mixer(proj, small, conv_w[layer], conv_b[layer].reshape(1, -1),
                   dt_bias[layer].reshape(1, -1), a_log[layer].reshape(1, -1), d_exp,
                   ssd_norm_g[layer].reshape(1, -1), gla_w_gate[layer],
                   gla_b_gate[layer].reshape(1, -1), gla_norm_g[layer].reshape(1, -1))

    x1, h2, logits_t = _outproj(mixed, x2d, w_out[layer].astype(BF16), g1,
                                norm2_g[layer].reshape(1, -1), sc2, sh2, w_router[layer].T)

    ek, wk, rk, counts, cbf, ctf = _route(logits_t, router_bias[layer].reshape(-1, 1))

    counts = counts.reshape(-1)
    padded = (counts + MOE_BM - 1) // MOE_BM * MOE_BM
    pends = jnp.cumsum(padded)
    pstarts = pends - padded
    block_start = jnp.arange(MOE_NB, dtype=I32) * MOE_BM
    block_exp = jnp.minimum(
        jnp.sum((pends[None, :] <= block_start[:, None]).astype(I32), axis=1), N_EXPERTS - 1)
    nused = (pends[-1:] // MOE_BM).astype(I32)
    bidx = jnp.arange(MOE_NB, dtype=I32)
    prev_exp = jnp.concatenate([jnp.full((1,), -1, I32), block_exp[:-1]])
    first = jnp.logical_and(bidx < nused[0], block_exp != prev_exp)
    slot = (jnp.cumsum(first.astype(I32)) - 1) & 1
    later_first = jnp.logical_and(first[None, :], bidx[None, :] > bidx[:, None])
    next_first = jnp.min(jnp.where(later_first, bidx[None, :], MOE_NB), axis=1)
    next_exp = jnp.sum(jnp.where(bidx[None, :] == next_first[:, None], block_exp[None, :], 0), axis=1)
    next_exp = jnp.where(next_first < MOE_NB, next_exp, N_EXPERTS).astype(I32)

    cb = cbf[:, :N_TILES].astype(I32)
    ct = ctf[:, :N_TILES].astype(I32)
    run_start = pstarts[:, None].astype(I32) + cb
    win_start = jnp.where(ct > 0, run_start & -COMBINE_ALIGN, 0)
    win_end = (run_start + ct + COMBINE_ALIGN - 1) & -COMBINE_ALIGN
    win_len = jnp.where(ct > 0, win_end - win_start, COMBINE_ALIGN)
    stage_off = jnp.cumsum(win_len, axis=0) - win_len
    shift = jnp.pad(stage_off - win_start, ((0, 0), (0, TILE_LANES - N_TILES)))
    tile_major = lambda a: a.T.reshape(-1).astype(I32)

    dest, pos = _dest(ek, rk, pstarts.reshape(-1, 1).astype(I32), shift)
    xs_perm, y_shared = _dispatch(
        dest.reshape(-1), (pstarts + counts).astype(I32), (padded - counts).astype(I32), h2,
        w_s_gate[layer].astype(BF16), w_s_up[layer].astype(BF16), w_s_down[layer].astype(BF16))
    eo = _experts(block_exp.astype(I32), first.astype(I32), slot.astype(I32), next_exp, nused, xs_perm,
                  w_e_gate[layer], w_e_up[layer], w_e_down[layer])
    out = _combine(tile_major(win_start), tile_major(win_len), tile_major(stage_off), y_shared, x1,
                   wk.T, pos.T, g2, normf_g.reshape(1, -1), eo)
    return out.reshape(x.shape)
```

```python
import functools

import jax
import jax.numpy as jnp
from jax import lax
from jax.experimental import pallas as pl
from jax.experimental.pallas import tpu as pltpu

F32 = jnp.float32
BF16 = jnp.bfloat16
I32 = jnp.int32

D_MODEL = 2048
SEQ = 8192
NORM_EPS = 1e-6
SSD_HEADS = 32
SSD_HEAD_DIM = 64
SSD_INNER = 2048
SSD_GROUPS = 4
SSD_STATE = 128
SSD_CHUNK = 128
HEADS_PER_GROUP = SSD_HEADS // SSD_GROUPS
GROUP_COLS = HEADS_PER_GROUP * SSD_HEAD_DIM
GLA_HEADS = 4
GLA_KEY_DIM = 1024
GLA_VAL_DIM = 2048
GLA_HEAD_K = 256
GLA_HEAD_V = 512
GLA_GATE_RANK = 16
GLA_GATE_NORM = 16.0
GLA_CHUNK = 64
GLA_STEP = 128
N_EXPERTS = 64
TOP_K = 8
N_EXPERT_GROUPS = 8
TOPK_GROUPS = 4
EXPERT_DIM = 512
ROUTED_SCALE = 2.5

VMEM_LIMIT_BYTES = 56 * 1024 * 1024
SUBLANES = 8

COL_Z, COL_XS, COL_BC, COL_Q, COL_V, COL_R, COL_MA, COL_MB, COL_K = (
    0, 2048, 4096, 5120, 6144, 8192, 10240, 12288, 14336)
PROJ_COLS = 15360
SMALL_COLS = 128

MOE_BM = 512
MOE_NB = SEQ * TOP_K // MOE_BM + N_EXPERTS
MOE_ROWS = MOE_NB * MOE_BM


def _cparams(sem):
    return pltpu.CompilerParams(dimension_semantics=sem, vmem_limit_bytes=VMEM_LIMIT_BYTES)


def _split3(a):
    hi = a.astype(BF16)
    r1 = a - hi.astype(F32)
    mid = r1.astype(BF16)
    lo = (r1 - mid.astype(F32)).astype(BF16)
    return hi, mid, lo


def _dot(a, b):
    return jnp.dot(a, b, preferred_element_type=F32)


def _dot_nt(a, b):
    return lax.dot_general(a, b, (((1,), (1,)), ((), ())), preferred_element_type=F32)


def _dot_tn(a, b):
    return lax.dot_general(a, b, (((0,), (0,)), ((), ())), preferred_element_type=F32)


def _sel_dot(sel_bf16, a):
    hi, mid, lo = _split3(a)
    return _dot(sel_bf16, hi) + _dot(sel_bf16, mid) + _dot(sel_bf16, lo)


def _dot_sel(a, sel_bf16):
    hi, mid, lo = _split3(a)
    return _dot(hi, sel_bf16) + _dot(mid, sel_bf16) + _dot(lo, sel_bf16)


def _dot_sel_stacked(a, sel3):
    hi = a.astype(BF16).astype(F32)
    r1 = a - hi
    mid = r1.astype(BF16).astype(F32)
    lo = (r1 - mid).astype(BF16).astype(F32)
    return _dot(jnp.concatenate([hi, mid, lo], axis=1).astype(BF16), sel3)


def _sel_dot_stacked(sel3, a):
    hi, mid, lo = _split3(a)
    return _dot(sel3, jnp.concatenate([hi, mid, lo], axis=0))


def _dot3(a, b):
    ah = a.astype(BF16)
    al = (a - ah.astype(F32)).astype(BF16)
    bh = b.astype(BF16)
    bl = (b - bh.astype(F32)).astype(BF16)
    return _dot(ah, bh) + _dot(ah, bl) + _dot(al, bh)


def _dot3_nt(a, b):
    ah = a.astype(BF16)
    al = (a - ah.astype(F32)).astype(BF16)
    bh = b.astype(BF16)
    bl = (b - bh.astype(F32)).astype(BF16)
    return _dot_nt(ah, bh) + _dot_nt(ah, bl) + _dot_nt(al, bh)


def _sigmoid(x):
    return 0.5 * jnp.tanh(0.5 * x) + 0.5


def _silu(x):
    h = 0.5 * x
    return h + h * jnp.tanh(h)


def _softplus(x):
    return jnp.maximum(x, 0.0) + jnp.log(1.0 + jnp.exp(-jnp.abs(x)))


def _log_sigmoid(x):
    return jnp.minimum(x, 0.0) - jnp.log(1.0 + jnp.exp(-jnp.abs(x)))


def _rms(x, g):
    ms = jnp.mean(x * x, axis=-1, keepdims=True)
    return x * lax.rsqrt(ms + NORM_EPS) * g


ADA_TN = 1024


def _ada_kernel(c_ref, w_ref, b_ref, o_ref):
    ca = _silu(c_ref[...])
    o_ref[...] = jnp.sum(w_ref[...] * ca, axis=0, keepdims=True) + b_ref[...]


def _ada(c_col, w_ada, b_ada):
    n = w_ada.shape[1]
    return pl.pallas_call(
        _ada_kernel,
        grid=(n // ADA_TN,),
        in_specs=[
            pl.BlockSpec((D_MODEL, 1), lambda j: (0, 0)),
            pl.BlockSpec((D_MODEL, ADA_TN), lambda j: (0, j)),
            pl.BlockSpec((1, ADA_TN), lambda j: (0, j)),
        ],
        out_specs=pl.BlockSpec((1, ADA_TN), lambda j: (0, j)),
        out_shape=jax.ShapeDtypeStruct((1, n), F32),
        compiler_params=_cparams(("arbitrary",)),
        name="ada_mod",
    )(c_col, w_ada, b_ada)


IN_DIM = 15408
PACK_TN = 1024
W_IN_SEGMENTS = ((0, 5120), (5152, 1024), (7200, 4096), (11312, 4096), (6176, 1024))
W_IN_DT_COL = 5120
W_IN_GLR_COL = 11296
PACK_STARTS = tuple(s + t * PACK_TN for s, w in W_IN_SEGMENTS for t in range(w // PACK_TN))


def _pack_kernel(start_ref, dt_ref, glr_ref, wt_hbm, o_ref, small_ref, win_ref, sem):
    j = pl.program_id(0)
    slot = j % 2

    @pl.when(j == 0)
    def _():
        small_ref[...] = jnp.zeros_like(small_ref)
        small_ref[0:SSD_HEADS, :] = dt_ref[...]
        small_ref[SSD_HEADS:SSD_HEADS + GLA_GATE_RANK, :] = glr_ref[...]

    def window_copy(t, s):
        row = pl.multiple_of(start_ref[t], 2 * SUBLANES)
        return pltpu.make_async_copy(wt_hbm.at[pl.ds(row, PACK_TN), :], win_ref.at[s], sem.at[s])

    @pl.when(j == 0)
    def _():
        window_copy(0, 0).start()

    @pl.when(j + 1 < pl.num_programs(0))
    def _():
        window_copy(j + 1, 1 - slot).start()

    window_copy(j, slot).wait()
    o_ref[...] = win_ref[slot].T.astype(BF16)


def _pack_w_in(w_in_t):
    starts = jnp.asarray(PACK_STARTS, I32)
    grid_spec = pltpu.PrefetchScalarGridSpec(
        num_scalar_prefetch=1,
        grid=(len(PACK_STARTS),),
        in_specs=[pl.BlockSpec((SSD_HEADS, D_MODEL), lambda j, s: (W_IN_DT_COL // SSD_HEADS, 0)),
                  pl.BlockSpec((GLA_GATE_RANK, D_MODEL), lambda j, s: (W_IN_GLR_COL // GLA_GATE_RANK, 0)),
                  pl.BlockSpec(memory_space=pl.ANY)],
        out_specs=[pl.BlockSpec((D_MODEL, PACK_TN), lambda j, s: (0, j)),
                   pl.BlockSpec((SMALL_COLS, D_MODEL), lambda j, s: (0, 0))],
        scratch_shapes=[pltpu.VMEM((2, PACK_TN, D_MODEL), F32), pltpu.SemaphoreType.DMA((2,))],
    )
    return pl.pallas_call(
        _pack_kernel,
        grid_spec=grid_spec,
        out_shape=[jax.ShapeDtypeStruct((D_MODEL, PROJ_COLS), BF16),
                   jax.ShapeDtypeStruct((SMALL_COLS, D_MODEL), F32)],
        compiler_params=_cparams(("arbitrary",)),
        name="pack_w_in",
    )(starts, w_in_t, w_in_t, w_in_t)


INPROJ_TM = 1024
INPROJ_TN = 1024


def _inproj_kernel(x_ref, g_ref, sc_ref, sh_ref, wb_ref, ws_ref, o_ref, os_ref, h_ref):
    @pl.when(pl.program_id(1) == 0)
    def _():
        h = _rms(x_ref[...], g_ref[...]) * (1.0 + sc_ref[...]) + sh_ref[...]
        hb = h.astype(BF16)
        h_ref[...] = hb
        os_ref[...] = _dot_nt(hb, ws_ref[...].astype(BF16))

    o_ref[...] = _dot(h_ref[...], wb_ref[...]).astype(BF16)


def _inproj(x2d, g, sc, sh, w_big, w_small):
    vec = pl.BlockSpec((1, D_MODEL), lambda i, j: (0, 0))
    return pl.pallas_call(
        _inproj_kernel,
        grid=(SEQ // INPROJ_TM, PROJ_COLS // INPROJ_TN),
        in_specs=[
            pl.BlockSpec((INPROJ_TM, D_MODEL), lambda i, j: (i, 0)),
            vec, vec, vec,
            pl.BlockSpec((D_MODEL, INPROJ_TN), lambda i, j: (0, j)),
            pl.BlockSpec((SMALL_COLS, D_MODEL), lambda i, j: (0, 0)),
        ],
        out_specs=[
            pl.BlockSpec((INPROJ_TM, INPROJ_TN), lambda i, j: (i, j)),
            pl.BlockSpec((INPROJ_TM, SMALL_COLS), lambda i, j: (i, 0)),
        ],
        out_shape=[
            jax.ShapeDtypeStruct((SEQ, PROJ_COLS), BF16),
            jax.ShapeDtypeStruct((SEQ, SMALL_COLS), F32),
        ],
        scratch_shapes=[pltpu.VMEM((INPROJ_TM, D_MODEL), BF16)],
        compiler_params=_cparams(("arbitrary", "arbitrary")),
        name="inproj",
    )(x2d, g, sc, sh, w_big, w_small)


def _causal_conv_silu(cur, hist_ref, w_ref, b_ref):
    L = cur.shape[0]
    hist_ref[SUBLANES:, :] = cur
    acc = cur * w_ref[3:4, :] + b_ref[...]
    for s in (1, 2, 3):
        acc = acc + hist_ref[SUBLANES - s:SUBLANES - s + L, :] * w_ref[3 - s:4 - s, :]
    hist_ref[:SUBLANES, :] = cur[L - SUBLANES:]
    return _silu(acc)


def _ssd_chunk(xs_ref, bc_ref, z_ref, ma_ref, sm_ref, cwx_ref, cwb_ref, cbx_ref, cbb_ref,
               dtb_ref, alog_ref, dexp_ref, ng_ref, px_ref, pb_ref, st_ref):
    L = SSD_CHUNK
    xs_raw = xs_ref[...].astype(F32)
    bc_raw = bc_ref[...].astype(F32)
    xs = _causal_conv_silu(xs_raw, px_ref, cwx_ref, cbx_ref)
    bc = _causal_conv_silu(bc_raw, pb_ref, cwb_ref, cbb_ref)

    dt = _softplus(sm_ref[:, 0:SSD_HEADS] + dtb_ref[...])
    d_a = dt * (-jnp.exp(alog_ref[...]))
    ri = lax.broadcasted_iota(I32, (L, L), 0)
    ci = lax.broadcasted_iota(I32, (L, L), 1)
    causal = ri >= ci
    tril = jnp.where(causal, 1.0, 0.0).astype(BF16)
    acum = _sel_dot(tril, d_a)
    hi32 = lax.broadcasted_iota(I32, (SSD_HEADS, SSD_HEADS), 0)
    hj32 = lax.broadcasted_iota(I32, (SSD_HEADS, SSD_HEADS), 1)
    eye32 = jnp.where(hi32 == hj32, 1.0, 0.0).astype(BF16)
    a_hi, a_mid, a_lo = _split3(acum)
    acum_t = _dot_nt(eye32, a_hi) + _dot_nt(eye32, a_mid) + _dot_nt(eye32, a_lo)
    last = acum[L - 1:L, :]
    e_a = jnp.exp(acum)
    to_end = jnp.exp(last - acum)
    eh = lax.broadcasted_iota(I32, (SSD_HEADS, SSD_INNER), 0)
    ec = lax.broadcasted_iota(I32, (SSD_HEADS, SSD_INNER), 1)
    expand = jnp.where((ec >> 6) == eh, 1.0, 0.0).astype(BF16)
    expand3 = jnp.concatenate([expand, expand, expand], axis=0)
    dt_x = _dot_sel_stacked(dt, expand3)
    dte_x = _dot_sel_stacked(dt * to_end, expand3)
    ea_x = _dot_sel_stacked(e_a, expand3)
    cd_x = ea_x[L - 1:L, :]

    xdt = xs * dt_x
    lane = lax.broadcasted_iota(I32, xdt.shape, 1)
    left = (lane & 64) == 0
    xdt_l = jnp.where(left, xdt, 0.0).astype(BF16)
    xdt_r = jnp.where(left, 0.0, xdt).astype(BF16)
    xdte = (xs * dte_x).astype(BF16)

    y_diag = []
    y_off = []
    for g in range(SSD_GROUPS):
        bg = bc[:, g * SSD_STATE:(g + 1) * SSD_STATE].astype(BF16)
        cg = bc[:, 512 + g * SSD_STATE:512 + (g + 1) * SSD_STATE].astype(BF16)
        cb = _dot_nt(cg, bg)
        gsl = slice(g * GROUP_COLS, (g + 1) * GROUP_COLS)
        st = st_ref[g]
        y_off.append(_dot(cg, st.astype(BF16)))
        st_ref[g] = st * cd_x[:, gsl] + _dot_tn(bg, xdte[:, gsl])
        for p in range(HEADS_PER_GROUP // 2):
            ms = []
            for h in (g * HEADS_PER_GROUP + 2 * p, g * HEADS_PER_GROUP + 2 * p + 1):
                seg = acum[:, h:h + 1] - acum_t[h:h + 1, :]
                decay = jnp.exp(jnp.where(causal, seg, -jnp.inf))
                ms.append((cb * decay).astype(BF16))
            m_cat = jnp.concatenate(ms, axis=1)
            psl = slice((g * 4 + p) * 128, (g * 4 + p + 1) * 128)
            x_cat = jnp.concatenate([xdt_l[:, psl], xdt_r[:, psl]], axis=0)
            y_diag.append(_dot(m_cat, x_cat))

    y = (jnp.concatenate(y_diag, axis=1) + jnp.concatenate(y_off, axis=1) * ea_x
         + dexp_ref[...] * xs)
    yf = y * _silu(z_ref[...].astype(F32))
    outs = []
    for g in range(SSD_GROUPS):
        seg = yf[:, g * GROUP_COLS:(g + 1) * GROUP_COLS]
        ms = jnp.mean(seg * seg, axis=-1, keepdims=True)
        outs.append(seg * lax.rsqrt(ms + NORM_EPS))
    y_ssd = jnp.concatenate(outs, axis=1) * ng_ref[...]
    return _sigmoid(ma_ref[...].astype(F32)) * y_ssd


def _gla_chunks(q_ref, k_ref, v_ref, r_ref, mb_ref, sm_ref, wg_ref, bg_ref, ng_ref, s_ref, y_ssd, o_ref):
    L = GLA_CHUNK
    ri = lax.broadcasted_iota(I32, (L, L), 0)
    ci = lax.broadcasted_iota(I32, (L, L), 1)
    causal = ri >= ci
    r3 = lax.broadcasted_iota(I32, (L, 3 * L), 0)
    c3 = lax.broadcasted_iota(I32, (L, 3 * L), 1)
    tril3 = jnp.where(r3 >= c3 % L, 1.0, 0.0).astype(BF16)

    for c in range(GLA_STEP // L):
        rs = slice(c * L, (c + 1) * L)
        glr = sm_ref[rs, SSD_HEADS:SSD_HEADS + GLA_GATE_RANK]
        pre = _dot3(glr, wg_ref[...]) + bg_ref[...]
        gk = _log_sigmoid(pre) / GLA_GATE_NORM
        bcum = _sel_dot_stacked(tril3, gk)
        bmid = bcum[L // 2:L // 2 + 1, :]
        blast = bcum[L - 1:L, :]
        q = q_ref[rs, :].astype(F32) * (GLA_HEAD_K ** -0.5)
        k = k_ref[rs, :].astype(F32)
        q_rel = (q * jnp.exp(bcum - bmid)).astype(BF16)
        k_rel = (k * jnp.exp(bmid - bcum)).astype(BF16)
        q_int = (q * jnp.exp(bcum)).astype(BF16)
        k_end = (k * jnp.exp(blast - bcum)).astype(BF16)
        dec = jnp.exp(blast)
        v = v_ref[rs, :]

        outs = []
        for h in range(GLA_HEADS):
            ks = slice(h * GLA_HEAD_K, (h + 1) * GLA_HEAD_K)
            vs = slice(h * GLA_HEAD_V, (h + 1) * GLA_HEAD_V)
            att = jnp.where(causal, _dot_nt(q_rel[:, ks], k_rel[:, ks]), 0.0)
            s_t = s_ref[h]
            o_h = _dot(att.astype(BF16), v[:, vs]) + _dot_nt(q_int[:, ks], s_t.astype(BF16))
            s_ref[h] = s_t * dec[:, ks] + _dot_tn(v[:, vs], k_end[:, ks])
            outs.append(_rms(o_h, ng_ref[...]))
        o = jnp.concatenate(outs, axis=1)
        o = o * _silu(r_ref[rs, :].astype(F32)) * _sigmoid(mb_ref[rs, :].astype(F32))
        o_ref[rs, :] = (y_ssd[rs, :] + o).astype(BF16)


N_SSD_IN = 13
N_GLA_IN = 9


def _mixer_kernel(*refs):
    ssd_in = refs[:N_SSD_IN]
    gla_in = refs[N_SSD_IN:N_SSD_IN + N_GLA_IN]
    o_ref, px_ref, pb_ref, st_ref, s_ref = refs[N_SSD_IN + N_GLA_IN:]

    @pl.when(pl.program_id(0) == 0)
    def _():
        px_ref[...] = jnp.zeros_like(px_ref)
        pb_ref[...] = jnp.zeros_like(pb_ref)
        st_ref[...] = jnp.zeros_like(st_ref)
        s_ref[...] = jnp.zeros_like(s_ref)

    y_ssd = _ssd_chunk(*ssd_in, px_ref, pb_ref, st_ref)
    _gla_chunks(*gla_in, s_ref, y_ssd, o_ref)


def _mixer(proj, small, conv_w, conv_b, dt_bias, a_log, d_exp, ssd_norm_g, w_gate, b_gate, gla_norm_g):
    L = SSD_CHUNK
    full = lambda shape: pl.BlockSpec(shape, lambda i: (0, 0))
    cols = lambda width, col: pl.BlockSpec((L, width), lambda i: (i, col // width))
    ssd_specs = [
        cols(2048, COL_XS), cols(1024, COL_BC), cols(2048, COL_Z), cols(2048, COL_MA),
        pl.BlockSpec((L, SMALL_COLS), lambda i: (i, 0)),
        pl.BlockSpec((4, 2048), lambda i: (0, 0)),
        pl.BlockSpec((4, 1024), lambda i: (0, 2)),
        pl.BlockSpec((1, 2048), lambda i: (0, 0)),
        pl.BlockSpec((1, 1024), lambda i: (0, 2)),
        full((1, SSD_HEADS)), full((1, SSD_HEADS)),
        full((1, SSD_INNER)), full((1, SSD_INNER)),
    ]
    gla_specs = [
        cols(1024, COL_Q), cols(1024, COL_K), cols(2048, COL_V), cols(2048, COL_R), cols(2048, COL_MB),
        pl.BlockSpec((L, SMALL_COLS), lambda i: (i, 0)),
        full((GLA_GATE_RANK, GLA_KEY_DIM)), full((1, GLA_KEY_DIM)), full((1, GLA_HEAD_V)),
    ]
    assert len(ssd_specs) == N_SSD_IN and len(gla_specs) == N_GLA_IN and GLA_STEP == L
    return pl.pallas_call(
        _mixer_kernel,
        grid=(SEQ // L,),
        in_specs=ssd_specs + gla_specs,
        out_specs=pl.BlockSpec((L, D_MODEL), lambda i: (i, 0)),
        out_shape=jax.ShapeDtypeStruct((SEQ, D_MODEL), BF16),
        scratch_shapes=[
            pltpu.VMEM((SUBLANES + L, 2048), F32),
            pltpu.VMEM((SUBLANES + L, 1024), F32),
            pltpu.VMEM((SSD_GROUPS, SSD_STATE, GROUP_COLS), F32),
            pltpu.VMEM((GLA_HEADS, GLA_HEAD_V, GLA_HEAD_K), F32),
        ],
        compiler_params=_cparams(("arbitrary",)),
        name="mixer",
    )(proj, proj, proj, proj, small, conv_w, conv_w, conv_b, conv_b, dt_bias, a_log, d_exp, ssd_norm_g,
      proj, proj, proj, proj, proj, small, w_gate, b_gate, gla_norm_g)


OUTPROJ_TM = 512


def _outproj_kernel(m_ref, x_ref, w_ref, g1_ref, n2_ref, sc_ref, sh_ref, wr_ref,
                    x1_ref, h2_ref, lg_ref):
    x1 = x_ref[...] + g1_ref[...] * _dot(m_ref[...], w_ref[...])
    x1_ref[...] = x1
    h2 = _rms(x1, n2_ref[...]) * (1.0 + sc_ref[...]) + sh_ref[...]
    h2_ref[...] = h2
    lg_ref[...] = _dot3_nt(wr_ref[...], h2)


def _outproj(mixed, x2d, w_out_bf, g1, n2g, sc2, sh2, w_router_t):
    tm = OUTPROJ_TM
    vec = pl.BlockSpec((1, D_MODEL), lambda i: (0, 0))
    row = pl.BlockSpec((tm, D_MODEL), lambda i: (i, 0))
    return pl.pallas_call(
        _outproj_kernel,
        grid=(SEQ // tm,),
        in_specs=[row, row,
                  pl.BlockSpec((D_MODEL, D_MODEL), lambda i: (0, 0), pipeline_mode=pl.Buffered(1)),
                  vec, vec, vec, vec,
                  pl.BlockSpec((N_EXPERTS, D_MODEL), lambda i: (0, 0))],
        out_specs=[row, row, pl.BlockSpec((N_EXPERTS, tm), lambda i: (0, i))],
        out_shape=[jax.ShapeDtypeStruct((SEQ, D_MODEL), F32),
                   jax.ShapeDtypeStruct((SEQ, D_MODEL), F32),
                   jax.ShapeDtypeStruct((N_EXPERTS, SEQ), F32)],
        compiler_params=_cparams(("arbitrary",)),
        name="outproj_router",
    )(mixed, x2d, w_out_bf, g1, n2g, sc2, sh2, w_router_t)


ROUTE_TN = 512
COMBINE_TT = 128
N_TILES = SEQ // COMBINE_TT
TILE_LANES = 128


def _route_kernel(lg_ref, bias_ref, ek_ref, wk_ref, rk_ref, cnt_ref, cb_ref, ct_ref, carry_ref):
    tn = ROUTE_TN
    gsz = N_EXPERTS // N_EXPERT_GROUPS

    @pl.when(pl.program_id(0) == 0)
    def _():
        carry_ref[...] = jnp.zeros_like(carry_ref)
        cb_ref[...] = jnp.zeros_like(cb_ref)
        ct_ref[...] = jnp.zeros_like(ct_ref)

    s = jax.nn.sigmoid(lg_ref[...])
    choice = s + bias_ref[...]
    r8 = lax.broadcasted_iota(I32, (gsz, tn), 0)
    neg = -jnp.inf

    def top1(cur, rows, nrows):
        m = jnp.max(cur, axis=0, keepdims=True)
        idx = jnp.min(jnp.where(cur == m, rows, nrows), axis=0, keepdims=True)
        return m, idx

    gscores = []
    for g in range(N_EXPERT_GROUPS):
        blk = choice[g * gsz:(g + 1) * gsz, :]
        m1, i1 = top1(blk, r8, gsz)
        m2, _ = top1(jnp.where(r8 == i1, neg, blk), r8, gsz)
        gscores.append(m1 + m2)
    cur = jnp.concatenate(gscores, axis=0)
    gsel = jnp.zeros((N_EXPERT_GROUPS, tn), F32)
    for _ in range(TOPK_GROUPS):
        _, idx = top1(cur, r8, N_EXPERT_GROUPS)
        hit = r8 == idx
        gsel = jnp.where(hit, 1.0, gsel)
        cur = jnp.where(hit, neg, cur)
    emask = jnp.concatenate(
        [jnp.broadcast_to(gsel[g:g + 1, :], (gsz, tn)) for g in range(N_EXPERT_GROUPS)], axis=0)
    rows = lax.broadcasted_iota(I32, (N_EXPERTS, tn), 0)
    cur = jnp.where(emask > 0.5, choice, neg)
    sel = jnp.zeros((N_EXPERTS, tn), F32)
    hits, eks, wks = [], [], []
    for _ in range(TOP_K):
        _, idx = top1(cur, rows, N_EXPERTS)
        hit = rows == idx
        hits.append(hit)
        eks.append(idx)
        wks.append(jnp.sum(jnp.where(hit, s, 0.0), axis=0, keepdims=True))
        cur = jnp.where(hit, neg, cur)
        sel = jnp.where(hit, 1.0, sel)
    wsum = wks[0]
    for w in wks[1:]:
        wsum = wsum + w
    scale = ROUTED_SCALE / (wsum + 1e-20)

    ti = lax.broadcasted_iota(I32, (tn, tn), 0)
    tj = lax.broadcasted_iota(I32, (tn, tn), 1)
    before = jnp.where(ti < tj, 1.0, 0.0).astype(BF16)
    rank = _dot(sel.astype(BF16), before) + carry_ref[...]
    rks = [jnp.sum(jnp.where(hit, rank, 0.0), axis=0, keepdims=True) for hit in hits]

    lane = lax.broadcasted_iota(I32, (N_EXPERTS, TILE_LANES), 1)
    carry = carry_ref[...]
    cb = cb_ref[...]
    ct = ct_ref[...]
    for sub in range(tn // COMBINE_TT):
        cnt = jnp.sum(sel[:, sub * COMBINE_TT:(sub + 1) * COMBINE_TT], axis=1, keepdims=True)
        col = pl.program_id(0) * (tn // COMBINE_TT) + sub
        cb = jnp.where(lane == col, carry, cb)
        ct = jnp.where(lane == col, cnt, ct)
        carry = carry + cnt
    cb_ref[...] = cb
    ct_ref[...] = ct
    carry_ref[...] = carry

    ek_ref[...] = jnp.concatenate(eks, axis=0)
    wk_ref[...] = jnp.concatenate(wks, axis=0) * scale
    rk_ref[...] = jnp.concatenate(rks, axis=0).astype(I32)
    cnt_ref[...] = carry_ref[...].astype(I32)


def _route(logits_t, bias_col):
    tn = ROUTE_TN
    kt = pl.BlockSpec((TOP_K, tn), lambda i: (0, i))
    return pl.pallas_call(
        _route_kernel,
        grid=(SEQ // tn,),
        in_specs=[pl.BlockSpec((N_EXPERTS, tn), lambda i: (0, i)),
                  pl.BlockSpec((N_EXPERTS, 1), lambda i: (0, 0))],
        out_specs=[kt, kt, kt, pl.BlockSpec((N_EXPERTS, 1), lambda i: (0, 0)),
                   pl.BlockSpec((N_EXPERTS, TILE_LANES), lambda i: (0, 0)),
                   pl.BlockSpec((N_EXPERTS, TILE_LANES), lambda i: (0, 0))],
        out_shape=[jax.ShapeDtypeStruct((TOP_K, SEQ), I32),
                   jax.ShapeDtypeStruct((TOP_K, SEQ), F32),
                   jax.ShapeDtypeStruct((TOP_K, SEQ), I32),
                   jax.ShapeDtypeStruct((N_EXPERTS, 1), I32),
                   jax.ShapeDtypeStruct((N_EXPERTS, TILE_LANES), F32),
                   jax.ShapeDtypeStruct((N_EXPERTS, TILE_LANES), F32)],
        scratch_shapes=[pltpu.VMEM((N_EXPERTS, 1), F32)],
        compiler_params=_cparams(("arbitrary",)),
        name="route",
    )(logits_t, bias_col)


def _dest_kernel(ek_ref, rk_ref, ps_ref, shift_ref, o_ref, pos_ref):
    tt = COMBINE_TT
    rows = lax.broadcasted_iota(I32, (N_EXPERTS, tt), 0)
    lane = lax.broadcasted_iota(I32, (N_EXPERTS, TILE_LANES), 1)
    dests, poss = [], []
    for sub in range(ROUTE_TN // tt):
        tile = pl.program_id(0) * (ROUTE_TN // tt) + sub
        shift_col = jnp.sum(jnp.where(lane == tile, shift_ref[...], 0), axis=1, keepdims=True)
        ds, ps = [], []
        for k in range(TOP_K):
            hit = rows == ek_ref[k:k + 1, sub * tt:(sub + 1) * tt]
            d = (jnp.sum(jnp.where(hit, ps_ref[...], 0), axis=0, keepdims=True)
                 + rk_ref[k:k + 1, sub * tt:(sub + 1) * tt])
            ds.append(d)
            ps.append(d + jnp.sum(jnp.where(hit, shift_col, 0), axis=0, keepdims=True))
        dests.append(jnp.concatenate(ds, axis=0))
        poss.append(jnp.concatenate(ps, axis=0))
    o_ref[...] = jnp.concatenate(dests, axis=1)
    pos_ref[...] = jnp.concatenate(poss, axis=1)


def _dest(ek, rk, pstarts_col, shift):
    tn = ROUTE_TN
    kt = pl.BlockSpec((TOP_K, tn), lambda i: (0, i))
    return pl.pallas_call(
        _dest_kernel,
        grid=(SEQ // tn,),
        in_specs=[kt, kt, pl.BlockSpec((N_EXPERTS, 1), lambda i: (0, 0)),
                  pl.BlockSpec((N_EXPERTS, TILE_LANES), lambda i: (0, 0))],
        out_specs=[kt, kt],
        out_shape=[jax.ShapeDtypeStruct((TOP_K, SEQ), I32), jax.ShapeDtypeStruct((TOP_K, SEQ), I32)],
        compiler_params=_cparams(("arbitrary",)),
        name="dest_rows",
    )(ek, rk, pstarts_col, shift)


DISPATCH_TT = 256


def _dispatch_kernel(dest_ref, padoff_ref, padlen_ref, h_ref, wg_ref, wu_ref, wd_ref,
                     out_ref, ysh_ref, zeros_ref, sem, zsem):
    tt = DISPATCH_TT
    i = pl.program_id(0)
    base = i * tt

    def issue(t, carry):
        for k in range(TOP_K):
            d = dest_ref[k * SEQ + base + t]
            pltpu.make_async_copy(h_ref.at[pl.ds(t, 1), :], out_ref.at[pl.ds(d, 1), :],
                                  sem).start(priority=k % 2)
        return carry

    lax.fori_loop(0, tt, issue, 0)

    def pad_copy(off, n):
        return pltpu.make_async_copy(zeros_ref.at[pl.ds(0, n), :], out_ref.at[pl.ds(off, n), :], zsem)

    def pad_rows(wait):
        def body(e, carry):
            start = padoff_ref[e]
            head = (-start) & (SUBLANES - 1)
            for j in range(SUBLANES - 1):
                @pl.when(j < head)
                def _():
                    cp = pad_copy(start + j, 1)
                    cp.wait() if wait else cp.start()
            n = pl.multiple_of(padlen_ref[e] - head, SUBLANES)

            @pl.when(n > 0)
            def _():
                cp = pad_copy(pl.multiple_of(start + head, SUBLANES), n)
                cp.wait() if wait else cp.start()
            return carry
        lax.fori_loop(0, N_EXPERTS, body, 0)

    @pl.when(i == 0)
    def _():
        zeros_ref[...] = jnp.zeros_like(zeros_ref)
        pad_rows(wait=False)

    @pl.when(i == pl.num_programs(0) - 1)
    def _():
        pad_rows(wait=True)

    h = h_ref[...].astype(BF16)
    hid = (_silu(_dot(h, wg_ref[...])) * _dot(h, wu_ref[...])).astype(BF16)
    ysh_ref[...] = _dot(hid, wd_ref[...]).astype(BF16)

    for _ in range(TOP_K):
        pltpu.make_async_copy(h_ref, out_ref.at[pl.ds(0, tt), :], sem).wait()


def _dispatch(dest_flat, pad_off, pad_len, h2, wsg, wsu, wsd):
    tt = DISPATCH_TT
    const = lambda shape: pl.BlockSpec(shape, lambda i, d, po, pn: (0, 0))
    grid_spec = pltpu.PrefetchScalarGridSpec(
        num_scalar_prefetch=3,
        grid=(SEQ // tt,),
        in_specs=[pl.BlockSpec((tt, D_MODEL), lambda i, d, po, pn: (i, 0)),
                  const((D_MODEL, EXPERT_DIM)), const((D_MODEL, EXPERT_DIM)), const((EXPERT_DIM, D_MODEL))],
        out_specs=[pl.BlockSpec(memory_space=pl.ANY),
                   pl.BlockSpec((tt, D_MODEL), lambda i, d, po, pn: (i, 0))],
        scratch_shapes=[
            pltpu.VMEM((MOE_BM, D_MODEL), F32),
            pltpu.SemaphoreType.DMA,
            pltpu.SemaphoreType.DMA,
        ],
    )
    return pl.pallas_call(
        _dispatch_kernel,
        grid_spec=grid_spec,
        out_shape=[jax.ShapeDtypeStruct((MOE_ROWS, D_MODEL), F32),
                   jax.ShapeDtypeStruct((SEQ, D_MODEL), BF16)],
        compiler_params=_cparams(("arbitrary",)),
        name="dispatch_shared",
    )(dest_flat, pad_off, pad_len, h2, wsg, wsu, wsd)


def _experts_kernel(bexp_ref, first_ref, slot_ref, next_ref, nused_ref,
                    x_ref, wg_hbm, wu_hbm, wd_hbm, o_ref,
                    wgf_ref, wuf_ref, wdf_ref, wgb_ref, wub_ref, wdb_ref, sem):
    b = pl.program_id(0)
    active = b < nused_ref[0]

    def weight_copies(e, s):
        return (pltpu.make_async_copy(wg_hbm.at[e], wgf_ref.at[s], sem.at[s]),
                pltpu.make_async_copy(wu_hbm.at[e], wuf_ref.at[s], sem.at[s]),
                pltpu.make_async_copy(wd_hbm.at[e], wdf_ref.at[s], sem.at[s]))

    @pl.when(jnp.logical_and(active, b == 0))
    def _():
        for cp in weight_copies(bexp_ref[0], 0):
            cp.start(priority=1)

    @pl.when(jnp.logical_and(active, first_ref[b] == 1))
    def _():
        s = slot_ref[b]
        nxt = next_ref[b]

        @pl.when(nxt < N_EXPERTS)
        def _():
            for cp in weight_copies(nxt, 1 - s):
                cp.start(priority=1)

        for cp in weight_copies(bexp_ref[b], s):
            cp.wait()
        wgb_ref[...] = wgf_ref[s].astype(BF16)
        wub_ref[...] = wuf_ref[s].astype(BF16)
        wdb_ref[...] = wdf_ref[s].astype(BF16)

    @pl.when(active)
    def _():
        x = x_ref[...].astype(BF16)
        gate = _dot(x, wgb_ref[...])
        up = _dot(x, wub_ref[...])
        hid = (_silu(gate) * up).astype(BF16)
        o_ref[...] = _dot(hid, wdb_ref[...]).astype(BF16)


def _experts(block_exp, first, slot, next_exp, nused, xs_perm, w_gate, w_up, w_down):
    bm = MOE_BM

    def xmap(b, be, fi, sl, ne, nu):
        return (jnp.minimum(b, nu[0] - 1), 0)

    hbm = pl.BlockSpec(memory_space=pl.ANY)
    grid_spec = pltpu.PrefetchScalarGridSpec(
        num_scalar_prefetch=5,
        grid=(MOE_NB,),
        in_specs=[pl.BlockSpec((bm, D_MODEL), xmap), hbm, hbm, hbm],
        out_specs=pl.BlockSpec((bm, D_MODEL), xmap),
        scratch_shapes=[
            pltpu.VMEM((2, D_MODEL, EXPERT_DIM), F32),
            pltpu.VMEM((2, D_MODEL, EXPERT_DIM), F32),
            pltpu.VMEM((2, EXPERT_DIM, D_MODEL), F32),
            pltpu.VMEM((D_MODEL, EXPERT_DIM), BF16),
            pltpu.VMEM((D_MODEL, EXPERT_DIM), BF16),
            pltpu.VMEM((EXPERT_DIM, D_MODEL), BF16),
            pltpu.SemaphoreType.DMA((2,)),
        ],
    )
    return pl.pallas_call(
        _experts_kernel,
        grid_spec=grid_spec,
        out_shape=jax.ShapeDtypeStruct((MOE_ROWS, D_MODEL), BF16),
        compiler_params=_cparams(("arbitrary",)),
        name="experts",
    )(block_exp, first, slot, next_exp, nused, xs_perm, w_gate, w_up, w_down)


COMBINE_ALIGN = 16
COMBINE_CHUNK = 512
COMBINE_ROWS = pl.cdiv(COMBINE_TT * TOP_K + 2 * (COMBINE_ALIGN - 1) * N_EXPERTS,
                       COMBINE_CHUNK) * COMBINE_CHUNK
COMBINE_MIN_CHUNKS = pl.cdiv(COMBINE_TT * TOP_K + (COMBINE_ALIGN - 1) * N_EXPERTS, COMBINE_CHUNK)
COMBINE_ISSUE = 32
assert COMBINE_ISSUE * COMBINE_MIN_CHUNKS >= N_EXPERTS


def _combine_kernel(ws_ref, wl_ref, so_ref, ysh_ref, x1_ref, wk_ref, pos_ref, g2_ref, nf_ref, eo_ref,
                    o_ref, buf_ref, acc_ref, sem):
    tt = COMBINE_TT
    i = pl.program_id(0)
    slot = i % 2

    last_tile = pl.num_programs(0) - 1

    def start_window(tile, e, slot_):
        idx = tile * N_EXPERTS + e
        n = pl.multiple_of(wl_ref[idx], COMBINE_ALIGN)
        src = pl.multiple_of(ws_ref[idx], COMBINE_ALIGN)
        dst = pl.multiple_of(slot_ * COMBINE_ROWS + so_ref[idx], COMBINE_ALIGN)
        pltpu.make_async_copy(eo_ref.at[pl.ds(src, n), :], buf_ref.at[pl.ds(dst, n), :],
                              sem.at[slot_]).start()

    def wait_windows(tile, slot_):
        last = tile * N_EXPERTS + N_EXPERTS - 1
        staged_ = pl.multiple_of(so_ref[last] + wl_ref[last], COMBINE_ALIGN)
        base = pl.multiple_of(slot_ * COMBINE_ROWS, COMBINE_ROWS)
        pltpu.make_async_copy(eo_ref.at[pl.ds(0, staged_), :], buf_ref.at[pl.ds(base, staged_), :],
                              sem.at[slot_]).wait()
        return staged_, base

    @pl.when(i == 0)
    def _():
        buf_ref[...] = jnp.zeros_like(buf_ref)

        def body(e, carry):
            start_window(0, e, 0)
            return carry

        lax.fori_loop(0, N_EXPERTS, body, 0)

    nxt = jnp.minimum(i + 1, last_tile)
    staged, sbase = wait_windows(i, slot)
    wk = wk_ref[...]
    pos = pos_ref[...]
    def routed(c):
        col = lax.broadcasted_iota(I32, (tt, COMBINE_CHUNK), 1) + c * COMBINE_CHUNK
        w = jnp.zeros((tt, COMBINE_CHUNK), F32)
        for k in range(TOP_K):
            w = w + jnp.where(col == pos[:, k:k + 1], wk[:, k:k + 1], 0.0)
        w_hi = w.astype(BF16)
        w_lo = (w - w_hi.astype(F32)).astype(BF16)
        rows = buf_ref[pl.ds(sbase + c * COMBINE_CHUNK, COMBINE_CHUNK), :]
        both = _dot(jnp.concatenate([w_hi, w_lo], axis=0), rows)
        return both[:tt] + both[tt:]

    acc = ysh_ref[...].astype(F32)
    for c in range(COMBINE_MIN_CHUNKS):
        for e in range(c * COMBINE_ISSUE, min((c + 1) * COMBINE_ISSUE, N_EXPERTS)):
            start_window(nxt, e, 1 - slot)
        acc = acc + routed(c)
    acc_ref[...] = acc

    @pl.when(i == last_tile)
    def _():
        wait_windows(nxt, 1 - slot)

    for c in range(COMBINE_MIN_CHUNKS, COMBINE_ROWS // COMBINE_CHUNK):
        @pl.when(c * COMBINE_CHUNK < staged)
        def _():
            acc_ref[...] += routed(c)
    x2 = x1_ref[...] + g2_ref[...] * acc_ref[...]
    o_ref[...] = _rms(x2, nf_ref[...])


def _combine(win_start, win_len, stage_off, y_shared, x1, wk_t, pos_t, g2, nf, eo):
    tt = COMBINE_TT
    vec = pl.BlockSpec((1, D_MODEL), lambda i, a, b, c: (0, 0))
    row = pl.BlockSpec((tt, D_MODEL), lambda i, a, b, c: (i, 0))
    slots = pl.BlockSpec((tt, TOP_K), lambda i, a, b, c: (i, 0))
    grid_spec = pltpu.PrefetchScalarGridSpec(
        num_scalar_prefetch=3,
        grid=(N_TILES,),
        in_specs=[row, row, slots, slots, vec, vec, pl.BlockSpec(memory_space=pl.ANY)],
        out_specs=row,
        scratch_shapes=[
            pltpu.VMEM((2 * COMBINE_ROWS, D_MODEL), BF16),
            pltpu.VMEM((tt, D_MODEL), F32),
            pltpu.SemaphoreType.DMA((2,)),
        ],
    )
    return pl.pallas_call(
        _combine_kernel,
        grid_spec=grid_spec,
        out_shape=jax.ShapeDtypeStruct((SEQ, D_MODEL), F32),
        compiler_params=_cparams(("arbitrary",)),
        name="combine_final",
    )(win_start, win_len, stage_off, y_shared, x1, wk_t, pos_t, g2, nf, eo)


def kernel(x, c, w_ada, b_ada, norm1_g, w_in, conv_w, conv_b, dt_bias, a_log, d_skip, ssd_norm_g, gla_w_gate, gla_b_gate, gla_norm_g, w_out, norm2_g, w_router, router_bias, w_e_gate, w_e_up, w_e_down, w_s_gate, w_s_up, w_s_down, normf_g):
    layer = 0
    x2d = x.reshape(SEQ, D_MODEL)
    mod = _ada(c.reshape(D_MODEL, 1), w_ada[layer], b_ada[layer].reshape(1, -1))
    sh1, sc1, g1, sh2, sc2, g2 = [mod[:, i * D_MODEL:(i + 1) * D_MODEL] for i in range(6)]

    w_big, w_small = _pack_w_in(jnp.swapaxes(w_in[layer], 0, 1))

    proj, small = _inproj(x2d, norm1_g[layer].reshape(1, -1), sc1, sh1, w_big, w_small)

    d_exp = jnp.repeat(d_skip[layer], SSD_HEAD_DIM).reshape(1, -1)
    mixed = _mixer(proj, small, conv_w[layer], conv_b[layer].reshape(1, -1),
                   dt_bias[layer].reshape(1, -1), a_log[layer].reshape(1, -1), d_exp,
                   ssd_norm_g[layer].reshape(1, -1), gla_w_gate[layer],
                   gla_b_gate[layer].reshape(1, -1), gla_norm_g[layer].reshape(1, -1))

    x1, h2, logits_t = _outproj(mixed, x2d, w_out[layer].astype(BF16), g1,
                                norm2_g[layer].reshape(1, -1), sc2, sh2, w_router[layer].T)

    ek, wk, rk, counts, cbf, ctf = _route(logits_t, router_bias[layer].reshape(-1, 1))

    counts = counts.reshape(-1)
    padded = (counts + MOE_BM - 1) // MOE_BM * MOE_BM
    pends = jnp.cumsum(padded)
    pstarts = pends - padded
    block_start = jnp.arange(MOE_NB, dtype=I32) * MOE_BM
    block_exp = jnp.minimum(
        jnp.sum((pends[None, :] <= block_start[:, None]).astype(I32), axis=1), N_EXPERTS - 1)
    nused = (pends[-1:] // MOE_BM).astype(I32)
    bidx = jnp.arange(MOE_NB, dtype=I32)
    prev_exp = jnp.concatenate([jnp.full((1,), -1, I32), block_exp[:-1]])
    first = jnp.logical_and(bidx < nused[0], block_exp != prev_exp)
    slot = (jnp.cumsum(first.astype(I32)) - 1) & 1
    later_first = jnp.logical_and(first[None, :], bidx[None, :] > bidx[:, None])
    next_first = jnp.min(jnp.where(later_first, bidx[None, :], MOE_NB), axis=1)
    next_exp = jnp.sum(jnp.where(bidx[None, :] == next_first[:, None], block_exp[None, :], 0), axis=1)
    next_exp = jnp.where(next_first < MOE_NB, next_exp, N_EXPERTS).astype(I32)

    cb = cbf[:, :N_TILES].astype(I32)
    ct = ctf[:, :N_TILES].astype(I32)
    run_start = pstarts[:, None].astype(I32) + cb
    win_start = jnp.where(ct > 0, run_start & -COMBINE_ALIGN, 0)
    win_end = (run_start + ct + COMBINE_ALIGN - 1) & -COMBINE_ALIGN
    win_len = jnp.where(ct > 0, win_end - win_start, COMBINE_ALIGN)
    stage_off = jnp.cumsum(win_len, axis=0) - win_len
    shift = jnp.pad(stage_off - win_start, ((0, 0), (0, TILE_LANES - N_TILES)))
    tile_major = lambda a: a.T.reshape(-1).astype(I32)

    dest, pos = _dest(ek, rk, pstarts.reshape(-1, 1).astype(I32), shift)
    xs_perm, y_shared = _dispatch(
        dest.reshape(-1), (pstarts + counts).astype(I32), (padded - counts).astype(I32), h2,
        w_s_gate[layer].astype(BF16), w_s_up[layer].astype(BF16), w_s_down[layer].astype(BF16))
    eo = _experts(block_exp.astype(I32), first.astype(I32), slot.astype(I32), next_exp, nused, xs_perm,
                  w_e_gate[layer], w_e_up[layer], w_e_down[layer])
    out = _combine(tile_major(win_start), tile_major(win_len), tile_major(stage_off), y_shared, x1,
                   wk.T, pos.T, g2, normf_g.reshape(1, -1), eo)
    return out.reshape(x.shape)
```
